```python
import math
import jax, jax.numpy as jnp
from jax import lax
import numpy as np

D_MODEL = 1024
BATCH = 8
SEQ = 2048
DEPTH = 2

D_FF = 2816
EPS = 1e-6
ROPE_THETA = 10000.0
Q_BLOCK = 128
N_BRANCH = 3
SB_HEADS = 8
SB_HD = 64
MLA_HEADS = 8
MLA_NOPE = 64
MLA_ROPE = 32
MLA_V = 64
MLA_Q_LORA = 256
MLA_KV_LORA = 128
NSA_HEADS = 8
NSA_GROUPS = 2
NSA_REP = NSA_HEADS // NSA_GROUPS
NSA_HD = 64
CMP_LEN = 32
CMP_STRIDE = 16
CMP_HIDDEN = 128
SLC_LEN = 64
SLC_TOPK = 16
SLC_Q_CHUNK = 64
WIN = 512
FORCE_SCORE = 1e3
NEG_INF = -1e30
SB_W = SB_HEADS * SB_HD
NSA_KV_W = NSA_GROUPS * NSA_HD
IN_SIZES = (SB_W, SB_W, SB_W,
            MLA_Q_LORA, MLA_KV_LORA, MLA_ROPE,
            NSA_HEADS * NSA_HD,
            NSA_KV_W, NSA_KV_W, NSA_KV_W, NSA_KV_W, NSA_KV_W, NSA_KV_W,
            N_BRANCH * NSA_HEADS,
            N_BRANCH * D_MODEL)
N_IN = sum(IN_SIZES)

kernel_name = 'hybrid_sb_mla_nsa_macaron'


def rms_norm(x, g):
    xf = x.astype(jnp.float32)
    y = xf * lax.rsqrt(jnp.mean(xf * xf, axis=-1, keepdims=True) + EPS)
    return (y * g.astype(jnp.float32)).astype(x.dtype)


def rope(x, pos):
    d = x.shape[-1]
    half = d // 2
    inv = jnp.exp(-math.log(ROPE_THETA) * jnp.arange(half, dtype=jnp.float32) * (2.0 / d))
    ang = pos.astype(jnp.float32)[:, None] * inv[None, :]
    cos = jnp.cos(ang)[None, :, None, :]
    sin = jnp.sin(ang)[None, :, None, :]
    xf = x.astype(jnp.float32)
    x1, x2 = xf[..., :half], xf[..., half:]
    return jnp.concatenate([x1 * cos - x2 * sin, x1 * sin + x2 * cos], axis=-1).astype(x.dtype)


def masked_softmax(s, mask):
    s = jnp.where(mask, s.astype(jnp.float32), NEG_INF)
    m = jnp.max(s, axis=-1, keepdims=True)
    e = jnp.where(mask, jnp.exp(s - m), 0.0)
    return e / jnp.maximum(jnp.sum(e, axis=-1, keepdims=True), 1e-30)


def swiglu_ffn(x, g, wi, wo):
    a, b = jnp.split(rms_norm(x, g) @ wi, 2, axis=-1)
    return (jax.nn.silu(a) * b) @ wo


def to_blocks(x, blk):
    B, S = x.shape[:2]
    return jnp.moveaxis(x.reshape((B, S // blk, blk) + x.shape[2:]), 1, 0)


def from_blocks(y):
    y = jnp.moveaxis(y, 0, 1)
    return y.reshape((y.shape[0], -1) + y.shape[3:])


def stick_breaking_attention(q, k, v):
    B, S, H, d = q.shape
    scale = 1.0 / math.sqrt(d)
    kpos = jnp.arange(S)

    def block(args):
        qb, i = args
        qpos = i * Q_BLOCK + jnp.arange(Q_BLOCK)
        z = jnp.einsum('bqhd,bkhd->bhqk', qb, k).astype(jnp.float32) * scale
        strict = kpos[None, :] < qpos[:, None]
        log_one_minus = jnp.where(strict, -jax.nn.softplus(z), 0.0)
        suffix = lax.cumsum(log_one_minus, axis=3, reverse=True) - log_one_minus
        w = jnp.where(strict, jnp.exp(jax.nn.log_sigmoid(z) + suffix), 0.0)
        return jnp.einsum('bhqk,bkhd->bqhd', w.astype(v.dtype), v)

    return from_blocks(lax.map(block, (to_blocks(q, Q_BLOCK), jnp.arange(S // Q_BLOCK))))


def causal_attention(q, k, v):
    B, S, H, d = q.shape
    scale = 1.0 / math.sqrt(d)
    kpos = jnp.arange(S)

    def block(args):
        qb, i = args
        qpos = i * Q_BLOCK + jnp.arange(Q_BLOCK)
        s = jnp.einsum('bqhd,bkhd->bhqk', qb, k).astype(jnp.float32) * scale
        p = masked_softmax(s, kpos[None, :] <= qpos[:, None])
        return jnp.einsum('bhqk,bkhd->bqhd', p.astype(v.dtype), v)

    return from_blocks(lax.map(block, (to_blocks(q, Q_BLOCK), jnp.arange(S // Q_BLOCK))))


def mla_attention(c_q, c_kv, k_r, q_norm, w_uq, kv_norm, w_ukv, gain_q, gain_k, pos):
    B, S, _ = c_q.shape
    q = (rms_norm(c_q, q_norm) @ w_uq).reshape(B, S, MLA_HEADS, MLA_NOPE + MLA_ROPE)
    kv = (rms_norm(c_kv, kv_norm) @ w_ukv).reshape(B, S, MLA_HEADS, MLA_NOPE + MLA_V)
    k_nope, v = kv[..., :MLA_NOPE], kv[..., MLA_NOPE:]
    k = jnp.concatenate([k_nope, jnp.broadcast_to(k_r[:, :, None, :], (B, S, MLA_HEADS, MLA_ROPE))], axis=-1)
    q = rms_norm(q, gain_q)
    k = rms_norm(k, gain_k)
    q = jnp.concatenate([q[..., :MLA_NOPE], rope(q[..., MLA_NOPE:], pos)], axis=-1)
    k = jnp.concatenate([k[..., :MLA_NOPE], rope(k[..., MLA_NOPE:], pos)], axis=-1)
    return causal_attention(q, k, v).reshape(B, S, MLA_HEADS * MLA_V)


def nsa_attention(q_in, cmp_k_in, cmp_v_in, slc_k_in, slc_v_in, win_k_in, win_v_in, gate_logits,
                  q_gain, k_gain, cmp_pos_k, cmp_pos_v, cmp_wk1, cmp_wk2, cmp_wv1, cmp_wv2, pos):
    B, S, _ = q_in.shape
    G, R, hd = NSA_GROUPS, NSA_REP, NSA_HD
    scale = 1.0 / math.sqrt(hd)
    q = rope(rms_norm(q_in.reshape(B, S, NSA_HEADS, hd), q_gain), pos).reshape(B, S, G, R, hd)

    n_cmp = (S - CMP_LEN) // CMP_STRIDE + 1
    starts = jnp.arange(n_cmp) * CMP_STRIDE
    ends = starts + (CMP_LEN - 1)
    idx = starts[:, None] + jnp.arange(CMP_LEN)[None, :]

    def compress(t, pe, w1, w2):
        tb = t.reshape(B, S, G, hd)[:, idx] + pe[None, None, :, None, :]
        tb = jnp.moveaxis(tb, 3, 2).reshape(B, n_cmp, G, CMP_LEN * hd)
        return jax.nn.gelu(tb @ w1) @ w2

    k_cmp = rope(rms_norm(compress(cmp_k_in, cmp_pos_k, cmp_wk1, cmp_wk2), k_gain), ends)
    v_cmp = compress(cmp_v_in, cmp_pos_v, cmp_wv1, cmp_wv2)
    s_cmp = jnp.einsum('bsgrd,bngd->bgrsn', q, k_cmp).astype(jnp.float32) * scale
    p_cmp = masked_softmax(s_cmp, ends[None, :] <= pos[:, None])
    o_cmp = jnp.einsum('bgrsn,bngd->bsgrd', p_cmp.astype(v_cmp.dtype), v_cmp)

    n_slc = S // SLC_LEN
    n_sel = min(SLC_TOPK, n_slc)
    c0 = np.arange(n_cmp)[:, None] * CMP_STRIDE
    s0 = np.arange(n_slc)[None, :] * SLC_LEN
    overlap = np.clip(np.minimum(c0 + CMP_LEN, s0 + SLC_LEN) - np.maximum(c0, s0), 0, None) / CMP_LEN
    importance = jnp.einsum('bgrsn,nj->bgsj', p_cmp, jnp.asarray(overlap, jnp.float32))
    blk = jnp.arange(n_slc)[None, :]
    cur = (pos // SLC_LEN)[:, None]
    forced = (blk == 0) | (blk == cur) | (blk == cur - 1)
    score = jnp.where(blk > cur, -1.0, jnp.where(forced, FORCE_SCORE, importance))
    _, sel = lax.top_k(score, n_sel)

    k_s = rope(rms_norm(slc_k_in.reshape(B, S, G, hd), k_gain), pos)
    kb = jnp.moveaxis(k_s.reshape(B, n_slc, SLC_LEN, G, hd), 3, 1)
    vb = jnp.moveaxis(slc_v_in.reshape(B, n_slc, SLC_LEN, G, hd), 3, 1)
    bi = jnp.arange(B)[:, None, None, None]
    gi = jnp.arange(G)[None, :, None, None]
    n_chunk = S // SLC_Q_CHUNK
    n_tok = n_sel * SLC_LEN
    sel_c = jnp.moveaxis(sel.reshape(B, G, n_chunk, SLC_Q_CHUNK, n_sel), 2, 0)

    def slc_chunk(args):
        qc, sc, i = args
        qpos = i * SLC_Q_CHUNK + jnp.arange(SLC_Q_CHUNK)
        kg = kb[bi, gi, sc].reshape(B, G, SLC_Q_CHUNK, n_tok, hd)
        vg = vb[bi, gi, sc].reshape(B, G, SLC_Q_CHUNK, n_tok, hd)
        tok = (sc[..., None] * SLC_LEN + jnp.arange(SLC_LEN)).reshape(B, G, 1, SLC_Q_CHUNK, n_tok)
        s = jnp.einsum('bqgrd,bgqkd->bgrqk', qc, kg).astype(jnp.float32) * scale
        p = masked_softmax(s, tok <= qpos[:, None])
        return jnp.einsum('bgrqk,bgqkd->bqgrd', p.astype(vg.dtype), vg)

    o_slc = from_blocks(lax.map(slc_chunk, (to_blocks(q, SLC_Q_CHUNK), sel_c, jnp.arange(n_chunk))))

    k_w = rope(rms_norm(win_k_in.reshape(B, S, G, hd), k_gain), pos)
    pad = ((0, 0), (WIN, 0), (0, 0), (0, 0))
    kp = jnp.pad(k_w, pad)
    vp = jnp.pad(win_v_in.reshape(B, S, G, hd), pad)

    def win_block(args):
        qb, i = args
        start = i * Q_BLOCK
        kk = lax.dynamic_slice_in_dim(kp, start, Q_BLOCK + WIN, axis=1)
        vv = lax.dynamic_slice_in_dim(vp, start, Q_BLOCK + WIN, axis=1)
        qpos = start + jnp.arange(Q_BLOCK)
        kpos = start - WIN + jnp.arange(Q_BLOCK + WIN)
        dist = qpos[:, None] - kpos[None, :]
        mask = (dist >= 0) & (dist < WIN) & (kpos[None, :] >= 0)
        s = jnp.einsum('bqgrd,bkgd->bgrqk', qb, kk).astype(jnp.float32) * scale
        p = masked_softmax(s, mask)
        return jnp.einsum('bgrqk,bkgd->bqgrd', p.astype(vv.dtype), vv)

    o_win = from_blocks(lax.map(win_block, (to_blocks(q, Q_BLOCK), jnp.arange(S // Q_BLOCK))))

    g = jax.nn.sigmoid(gate_logits.astype(jnp.float32)).reshape(B, S, N_BRANCH, G, R, 1).astype(q.dtype)
    o = g[:, :, 0] * o_cmp + g[:, :, 1] * o_slc + g[:, :, 2] * o_win
    return o.reshape(B, S, NSA_HEADS * hd)


def setup_inputs(seed: int = 0) -> dict:
    key = jax.random.key(seed)
    ks = jax.random.split(key, 28)
    L = DEPTH
    cmp_in = CMP_LEN * NSA_HD

    def nrm(k, shape, scale):
        return jax.random.normal(k, shape, jnp.float32) * scale

    def gain(k, n):
        return 1.0 + 0.05 * jax.random.normal(k, (L, n), jnp.float32)

    return {
        'x': nrm(ks[0], (BATCH, SEQ, D_MODEL), 1.0),
        'ffn1_norm': gain(ks[1], D_MODEL),
        'ffn1_wi': nrm(ks[2], (L, D_MODEL, 2 * D_FF), D_MODEL ** -0.5),
        'ffn1_wo': nrm(ks[3], (L, D_FF, D_MODEL), D_FF ** -0.5),
        'mix_norm': gain(ks[4], D_MODEL),
        'w_in': nrm(ks[5], (L, D_MODEL, N_IN), D_MODEL ** -0.5),
        'mla_q_norm': gain(ks[6], MLA_Q_LORA),
        'mla_w_uq': nrm(ks[7], (L, MLA_Q_LORA, MLA_HEADS * (MLA_NOPE + MLA_ROPE)), MLA_Q_LORA ** -0.5),
        'mla_kv_norm': gain(ks[8], MLA_KV_LORA),
        'mla_w_ukv': nrm(ks[9], (L, MLA_KV_LORA, MLA_HEADS * (MLA_NOPE + MLA_V)), MLA_KV_LORA ** -0.5),
        'mla_qk_gain_q': gain(ks[10], MLA_NOPE + MLA_ROPE),
        'mla_qk_gain_k': gain(ks[11], MLA_NOPE + MLA_ROPE),
        'nsa_q_gain': gain(ks[12], NSA_HD),
        'nsa_k_gain': gain(ks[13], NSA_HD),
        'cmp_pos_k': nrm(ks[14], (L, CMP_LEN, NSA_HD), 0.5),
        'cmp_pos_v': nrm(ks[15], (L, CMP_LEN, NSA_HD), 0.5),
        'cmp_wk1': nrm(ks[16], (L, cmp_in, CMP_HIDDEN), cmp_in ** -0.5),
        'cmp_wk2': nrm(ks[17], (L, CMP_HIDDEN, NSA_HD), CMP_HIDDEN ** -0.5),
        'cmp_wv1': nrm(ks[18], (L, cmp_in, CMP_HIDDEN), cmp_in ** -0.5),
        'cmp_wv2': nrm(ks[19], (L, CMP_HIDDEN, NSA_HD), CMP_HIDDEN ** -0.5),
        'proj_sb': nrm(ks[20], (L, SB_W, D_MODEL), SB_W ** -0.5),
        'proj_mla': nrm(ks[21], (L, MLA_HEADS * MLA_V, D_MODEL), (MLA_HEADS * MLA_V) ** -0.5),
        'proj_nsa': nrm(ks[22], (L, NSA_HEADS * NSA_HD, D_MODEL), (NSA_HEADS * NSA_HD) ** -0.5),
        'w_out': nrm(ks[23], (L, D_MODEL, D_MODEL), D_MODEL ** -0.5),
        'ffn2_norm': gain(ks[24], D_MODEL),
        'ffn2_wi': nrm(ks[25], (L, D_MODEL, 2 * D_FF), D_MODEL ** -0.5),
        'ffn2_wo': nrm(ks[26], (L, D_FF, D_MODEL), D_FF ** -0.5),
    }


def reference(x, ffn1_norm, ffn1_wi, ffn1_wo, mix_norm, w_in, mla_q_norm, mla_w_uq, mla_kv_norm,
              mla_w_ukv, mla_qk_gain_q, mla_qk_gain_k, nsa_q_gain, nsa_k_gain, cmp_pos_k, cmp_pos_v,
              cmp_wk1, cmp_wk2, cmp_wv1, cmp_wv2, proj_sb, proj_mla, proj_nsa, w_out,
              ffn2_norm, ffn2_wi, ffn2_wo):
    B, S, _ = x.shape
    pos = jnp.arange(S, dtype=jnp.int32)
    offsets = np.cumsum(IN_SIZES)[:-1].tolist()
    for l in range(DEPTH):
        x = x + 0.5 * swiglu_ffn(x, ffn1_norm[l], ffn1_wi[l], ffn1_wo[l])
        h = rms_norm(x, mix_norm[l])
        (sb_q, sb_k, sb_v, mla_cq, mla_ckv, mla_kr, nsa_q, ck, cv, sk, sv, wk, wv,
         nsa_gate, merge_gate) = jnp.split(h @ w_in[l], offsets, axis=-1)
        o_sb = stick_breaking_attention(sb_q.reshape(B, S, SB_HEADS, SB_HD),
                                        sb_k.reshape(B, S, SB_HEADS, SB_HD),
                                        sb_v.reshape(B, S, SB_HEADS, SB_HD)).reshape(B, S, SB_W)
        o_mla = mla_attention(mla_cq, mla_ckv, mla_kr, mla_q_norm[l], mla_w_uq[l], mla_kv_norm[l],
                              mla_w_ukv[l], mla_qk_gain_q[l], mla_qk_gain_k[l], pos)
        o_nsa = nsa_attention(nsa_q, ck, cv, sk, sv, wk, wv, nsa_gate, nsa_q_gain[l], nsa_k_gain[l],
                              cmp_pos_k[l], cmp_pos_v[l], cmp_wk1[l], cmp_wk2[l], cmp_wv1[l], cmp_wv2[l], pos)
        gates = jax.nn.sigmoid(merge_gate.astype(jnp.float32)).reshape(B, S, N_BRANCH, D_MODEL).astype(x.dtype)
        y = (gates[:, :, 0] * (o_sb @ proj_sb[l])
             + gates[:, :, 1] * (o_mla @ proj_mla[l])
             + gates[:, :, 2] * (o_nsa @ proj_nsa[l]))
        x = x + y @ w_out[l]
        x = x + 0.5 * swiglu_ffn(x, ffn2_norm[l], ffn2_wi[l], ffn2_wo[l])
    return x
```

```python
import functools
import math

import numpy as np
import jax
import jax.numpy as jnp
from jax import lax
from jax.experimental import pallas as pl
from jax.experimental.pallas import tpu as pltpu

D_MODEL = 1024
D_FF = 2816
EPS = 1e-6
ROPE_THETA = 10000.0
N_BRANCH = 3
SB_HEADS = 8
SB_HD = 64
MLA_HEADS = 8
MLA_NOPE = 64
MLA_ROPE = 32
MLA_V = 64
MLA_Q_LORA = 256
MLA_KV_LORA = 128
MLA_QK = MLA_NOPE + MLA_ROPE
NSA_HEADS = 8
NSA_GROUPS = 2
NSA_REP = NSA_HEADS // NSA_GROUPS
NSA_HD = 64
CMP_LEN = 32
CMP_STRIDE = 16
CMP_HIDDEN = 128
SLC_LEN = 64
SLC_TOPK = 16
WIN = 512
FORCE_SCORE = 1e3
NEG_INF = -1e30
SB_W = SB_HEADS * SB_HD
NSA_KV_W = NSA_GROUPS * NSA_HD
IN_SIZES = (SB_W, SB_W, SB_W, MLA_Q_LORA, MLA_KV_LORA, MLA_ROPE, NSA_HEADS * NSA_HD,
            NSA_KV_W, NSA_KV_W, NSA_KV_W, NSA_KV_W, NSA_KV_W, NSA_KV_W,
            N_BRANCH * NSA_HEADS, N_BRANCH * D_MODEL)

LANE = 128
V7X_VMEM_LIMIT = 56 * 1024 * 1024

FFN_TM = 1024
FFN_TF = 256
PROJ_TM = 512
PREP_TS = 512
ATT_T = 128
SB_T = 256
MLA_T = 256
MERGE_TM = 512

Z2_CQ = 0
Z2_CKV = 256
Z2_KR = 384
Z2_NQ = 512
Z2_CK = 1024
Z2_CV = 1152
Z2_SK = 1280
Z2_SV = 1536
Z2_WK = 1792
Z2_WV = 2048
Z2_GATE = 2304
Z2_W = 2560

BF16 = jnp.bfloat16
F32 = jnp.float32


def _cparams(sem, vmem=V7X_VMEM_LIMIT):
    return pltpu.CompilerParams(dimension_semantics=sem, vmem_limit_bytes=vmem)


def _dot(a, b):
    return jnp.dot(a, b, preferred_element_type=F32)


def _dot_nt(a, b):
    return lax.dot_general(a, b, (((1,), (1,)), ((), ())), preferred_element_type=F32)


def _rms(x, g):
    y = x * lax.rsqrt(jnp.mean(x * x, axis=-1, keepdims=True) + EPS)
    return y * g


def _ffn_kernel(x_ref, g_ref, wa_ref, wb_ref, wo_ref, o_ref, h_ref, acc_ref):
    j = pl.program_id(1)

    @pl.when(j == 0)
    def _():
        h_ref[...] = _rms(x_ref[...], g_ref[...]).astype(BF16)
        acc_ref[...] = jnp.zeros_like(acc_ref)

    h = h_ref[...]
    a = _dot(h, wa_ref[...])
    b = _dot(h, wb_ref[...])
    u = (jax.nn.silu(a) * b).astype(BF16)
    acc_ref[...] += _dot(u, wo_ref[...])

    @pl.when(j == pl.num_programs(1) - 1)
    def _():
        o_ref[...] = x_ref[...] + 0.5 * acc_ref[...]


def _ffn(x, g, wi, wo):
    T, D = x.shape
    F = wo.shape[0]
    tm, tf = FFN_TM, FFN_TF
    nf = F // tf
    return pl.pallas_call(
        _ffn_kernel,
        grid=(T // tm, nf),
        in_specs=[
            pl.BlockSpec((tm, D), lambda i, j: (i, 0)),
            pl.BlockSpec((1, D), lambda i, j: (0, 0)),
            pl.BlockSpec((D, tf), lambda i, j: (0, j)),
            pl.BlockSpec((D, tf), lambda i, j: (0, j + nf)),
            pl.BlockSpec((tf, D), lambda i, j: (j, 0)),
        ],
        out_specs=pl.BlockSpec((tm, D), lambda i, j: (i, 0)),
        out_shape=jax.ShapeDtypeStruct((T, D), F32),
        scratch_shapes=[pltpu.VMEM((tm, D), BF16), pltpu.VMEM((tm, D), F32)],
        compiler_params=_cparams(("parallel", "arbitrary")),
        name="ffn",
    )(x, g, wi, wi, wo)


def _inproj_kernel(x_ref, g_ref, wa_ref, wb_ref, oa_ref, ob_ref):
    h = _rms(x_ref[...], g_ref[...]).astype(BF16)
    oa_ref[...] = _dot(h, wa_ref[...]).astype(oa_ref.dtype)
    ob_ref[...] = _dot(h, wb_ref[...]).astype(ob_ref.dtype)


def _inproj(x, g, wa, wb):
    T, D = x.shape
    tm = PROJ_TM
    na, nb = wa.shape[1], wb.shape[1]
    return pl.pallas_call(
        _inproj_kernel,
        grid=(T // tm,),
        in_specs=[
            pl.BlockSpec((tm, D), lambda i: (i, 0)),
            pl.BlockSpec((1, D), lambda i: (0, 0)),
            pl.BlockSpec((D, na), lambda i: (0, 0)),
            pl.BlockSpec((D, nb), lambda i: (0, 0)),
        ],
        out_specs=[pl.BlockSpec((tm, na), lambda i: (i, 0)),
                   pl.BlockSpec((tm, nb), lambda i: (i, 0))],
        out_shape=[jax.ShapeDtypeStruct((T, na), BF16), jax.ShapeDtypeStruct((T, nb), F32)],
        compiler_params=_cparams(("parallel",)),
        name="inproj",
    )(x, g, wa, wb)


def _rope_mix(y, c, s1, s2, half):
    return y * c + pltpu.roll(y, LANE - half, 1) * s1 + pltpu.roll(y, half, 1) * s2


def _norm_rope_pair(x, gain, c, s1, s2, scale):
    lane = lax.broadcasted_iota(jnp.int32, x.shape, 1)
    lo = lane < NSA_HD
    sq = x * x
    ss_lo = jnp.sum(jnp.where(lo, sq, 0.0), axis=-1, keepdims=True)
    ss_hi = jnp.sum(jnp.where(lo, 0.0, sq), axis=-1, keepdims=True)
    r = jnp.where(lo, lax.rsqrt(ss_lo * (1.0 / NSA_HD) + EPS), lax.rsqrt(ss_hi * (1.0 / NSA_HD) + EPS))
    y = x * r * gain
    y = _rope_mix(y, c, s1, s2, NSA_HD // 2)
    return y * scale if scale != 1.0 else y


def _norm_rope_mla(x, gain, c, s1, s2, scale):
    ss = jnp.sum(x * x, axis=-1, keepdims=True)
    y = x * lax.rsqrt(ss * (1.0 / MLA_QK) + EPS) * gain
    y = _rope_mix(y, c, s1, s2, MLA_ROPE // 2)
    return y * scale if scale != 1.0 else y


def _prep_kernel(z_ref, qn_ref, kvn_ref, wuq_ref, wuk_ref, wuv_ref, gq_ref, gk_ref,
                 mc_ref, ms1_ref, ms2_ref, nqg_ref, nkg_ref, nc_ref, ns1_ref, ns2_ref,
                 qm_ref, km_ref, vm_ref, qn_out, sk_out, sv_out, wk_out, wv_out, gate_out):
    zs = lambda start, width: z_ref[0, :, start:start + width]
    mc, ms1, ms2 = mc_ref[...], ms1_ref[...], ms2_ref[...]
    nc, ns1, ns2 = nc_ref[...], ns1_ref[...], ns2_ref[...]
    cq = _rms(zs(Z2_CQ, MLA_Q_LORA), qn_ref[...]).astype(BF16)
    ckv = _rms(zs(Z2_CKV, MLA_KV_LORA), kvn_ref[...]).astype(BF16)
    kr = zs(Z2_KR, LANE)
    q = _dot(cq, wuq_ref[...])
    kn = _dot(ckv, wuk_ref[...])
    vm_ref[0] = _dot(ckv, wuv_ref[...]).astype(BF16)
    q_scale = 1.0 / math.sqrt(MLA_QK)
    for h in range(MLA_HEADS):
        sl = slice(h * LANE, (h + 1) * LANE)
        qm_ref[0, :, sl] = _norm_rope_mla(q[:, sl], gq_ref[...], mc, ms1, ms2, q_scale).astype(BF16)
        km_ref[0, :, sl] = _norm_rope_mla(kn[:, sl] + kr, gk_ref[...], mc, ms1, ms2, 1.0).astype(BF16)
    nq_scale = 1.0 / math.sqrt(NSA_HD)
    for b in range(NSA_HEADS * NSA_HD // LANE):
        sl = slice(b * LANE, (b + 1) * LANE)
        qn_out[0, :, sl] = _norm_rope_pair(zs(Z2_NQ + b * LANE, LANE),
                                           nqg_ref[...], nc, ns1, ns2, nq_scale).astype(BF16)
    for b in range(2):
        sl = slice(b * LANE, (b + 1) * LANE)
        sk_out[0, :, sl] = _norm_rope_pair(zs(Z2_SK + b * LANE, LANE),
                                           nkg_ref[...], nc, ns1, ns2, 1.0).astype(BF16)
        wk_out[0, :, sl] = _norm_rope_pair(zs(Z2_WK + b * LANE, LANE),
                                           nkg_ref[...], nc, ns1, ns2, 1.0).astype(BF16)
    sv_out[0] = zs(Z2_SV, 2 * LANE).astype(BF16)
    wv_out[0] = zs(Z2_WV, 2 * LANE).astype(BF16)
    gate_out[0] = jax.nn.sigmoid(zs(Z2_GATE, 2 * LANE))


def _prep(z2, w, tabs):
    B, S, _ = z2.shape
    ts = PREP_TS
    full = lambda shape: pl.BlockSpec(shape, lambda b, i: (0,) * len(shape))
    tab = pl.BlockSpec((ts, LANE), lambda b, i: (i, 0))
    tok = lambda width: pl.BlockSpec((1, ts, width), lambda b, i: (b, i, 0))
    out_w = (MLA_HEADS * LANE, MLA_HEADS * LANE, MLA_HEADS * MLA_V, NSA_HEADS * NSA_HD,
             2 * LANE, 2 * LANE, 2 * LANE, 2 * LANE, 2 * LANE)
    out_dt = (BF16,) * 8 + (F32,)
    return pl.pallas_call(
        _prep_kernel,
        grid=(B, S // ts),
        in_specs=[
            tok(Z2_W),
            full((1, MLA_Q_LORA)), full((1, MLA_KV_LORA)),
            full((MLA_Q_LORA, MLA_HEADS * LANE)), full((MLA_KV_LORA, MLA_HEADS * LANE)),
            full((MLA_KV_LORA, MLA_HEADS * MLA_V)),
            full((1, LANE)), full((1, LANE)),
            tab, tab, tab,
            full((1, LANE)), full((1, LANE)),
            tab, tab, tab,
        ],
        out_specs=[tok(wd) for wd in out_w],
        out_shape=[jax.ShapeDtypeStruct((B, S, wd), dt) for wd, dt in zip(out_w, out_dt)],
        compiler_params=_cparams(("parallel", "parallel")),
        name="prep",
    )(z2, w["mla_q_norm"], w["mla_kv_norm"], w["wuq"], w["wuk"], w["wuv"], w["mla_gq"], w["mla_gk"],
      tabs["mla_c"], tabs["mla_s1"], tabs["mla_s2"], w["nsa_qg"], w["nsa_kg"],
      tabs["nsa_c"], tabs["nsa_s1"], tabs["nsa_s2"])


def _compress_kernel(ck_ref, cv_ref, pek_ref, pev_ref, w1kl_ref, w1kh_ref, w2k_ref,
                     w1vl_ref, w1vh_ref, w2v_ref, kg_ref, c_ref, s1_ref, s2_ref,
                     kc_ref, vc_ref, *, nchunk):
    def branch(t_ref, pe_ref, w1l_ref, w1h_ref, w2_ref):
        lo = jnp.zeros((nchunk, 2 * CMP_HIDDEN), F32)
        hi = jnp.zeros((nchunk, 2 * CMP_HIDDEN), F32)
        for l in range(CMP_STRIDE):
            x = t_ref[0, pl.ds(l, nchunk, stride=CMP_STRIDE), :]
            lo = lo + _dot((x + pe_ref[l:l + 1, :]).astype(BF16), w1l_ref[l])
            hi = hi + _dot((x + pe_ref[CMP_STRIDE + l:CMP_STRIDE + l + 1, :]).astype(BF16), w1h_ref[l])
        pre = lo + pltpu.roll(hi, nchunk - 1, 0)
        return _dot(jax.nn.gelu(pre).astype(BF16), w2_ref[...])

    kc = branch(ck_ref, pek_ref, w1kl_ref, w1kh_ref, w2k_ref)
    vc = branch(cv_ref, pev_ref, w1vl_ref, w1vh_ref, w2v_ref)
    c, s1, s2 = c_ref[...], s1_ref[...], s2_ref[...]
    for b in range(2):
        sl = slice(b * LANE, (b + 1) * LANE)
        kc_ref[0, :, sl] = _norm_rope_pair(kc[:, sl], kg_ref[...], c, s1, s2, 1.0).astype(BF16)
    vc_ref[0] = vc.astype(BF16)


def _compress(z2, w, tabs):
    B, S, _ = z2.shape
    nchunk = S // CMP_STRIDE
    full = lambda shape: pl.BlockSpec(shape, lambda b: (0,) * len(shape))
    tab = full((nchunk, LANE))
    return pl.pallas_call(
        functools.partial(_compress_kernel, nchunk=nchunk),
        grid=(B,),
        in_specs=[
            pl.BlockSpec((1, S, LANE), lambda b: (b, 0, Z2_CK // LANE)),
            pl.BlockSpec((1, S, LANE), lambda b: (b, 0, Z2_CV // LANE)),
            full((CMP_LEN, LANE)), full((CMP_LEN, LANE)),
            full((CMP_STRIDE, LANE, 2 * CMP_HIDDEN)), full((CMP_STRIDE, LANE, 2 * CMP_HIDDEN)),
            full((2 * CMP_HIDDEN, 2 * LANE)),
            full((CMP_STRIDE, LANE, 2 * CMP_HIDDEN)), full((CMP_STRIDE, LANE, 2 * CMP_HIDDEN)),
            full((2 * CMP_HIDDEN, 2 * LANE)),
            full((1, LANE)), tab, tab, tab,
        ],
        out_specs=[pl.BlockSpec((1, nchunk, 2 * LANE), lambda b: (b, 0, 0))] * 2,
        out_shape=[jax.ShapeDtypeStruct((B, nchunk, 2 * LANE), BF16)] * 2,
        compiler_params=_cparams(("parallel",)),
        name="compress",
    )(z2, z2, w["pe_k"], w["pe_v"], w["w1k_lo"], w["w1k_hi"], w["w2k"],
      w["w1v_lo"], w["w1v_hi"], w["w2v"], w["nsa_kg"], tabs["cmp_c"], tabs["cmp_s1"], tabs["cmp_s2"])


def _head_lanes(lane, half):
    return (lane >= NSA_HD * half) & (lane < NSA_HD * (half + 1))


def _sb_kernel(q_ref, k_ref, v_ref, o_ref, *, t):
    i = pl.program_id(2)
    lane = lax.broadcasted_iota(jnp.int32, (t, LANE), 1)
    row = lax.broadcasted_iota(jnp.int32, (t, t), 0)
    col = lax.broadcasted_iota(jnp.int32, (t, t), 1)
    strict = col < row
    upper = jnp.where(row > col, 1.0, 0.0).astype(BF16)
    q = q_ref[0]
    outs = []
    for h in range(2):
        qm = jnp.where(_head_lanes(lane, h), q, jnp.zeros_like(q))

        def tile(j, carry, acc, masked):
            off = pl.multiple_of(j * t, t)
            kt = k_ref[0, pl.ds(off, t), :]
            vt = v_ref[0, pl.ds(off, t), :]
            z = _dot_nt(qm, kt)
            lg = jnp.log1p(jnp.exp(-jnp.abs(z)))
            lom = -(jnp.maximum(z, 0.0) + lg)
            logsig = jnp.minimum(z, 0.0) - lg
            if masked:
                lom = jnp.where(strict, lom, 0.0)
            hi = lom.astype(BF16)
            lo = (lom - hi.astype(F32)).astype(BF16)
            suffix = _dot(hi, upper) + _dot(lo, upper)
            wgt = jnp.exp(logsig + suffix + carry)
            if masked:
                wgt = jnp.where(strict, wgt, 0.0)
            acc = acc + _dot(wgt.astype(BF16), vt)
            carry = carry + jnp.sum(lom, axis=-1, keepdims=True)
            return carry, acc

        carry, acc = tile(i, jnp.zeros((t, 1), F32), jnp.zeros((t, LANE), F32), True)
        carry, acc = lax.fori_loop(0, i, lambda s, c: tile(i - 1 - s, c[0], c[1], False), (carry, acc))
        outs.append(acc)
    o_ref[0] = jnp.where(lane < NSA_HD, outs[0], outs[1]).astype(o_ref.dtype)


def _sb_attention(zsb):
    B, S, _ = zsb.shape
    t = SB_T
    npair = SB_W // LANE
    return pl.pallas_call(
        functools.partial(_sb_kernel, t=t),
        grid=(B, npair, S // t),
        in_specs=[
            pl.BlockSpec((1, t, LANE), lambda b, p, i: (b, i, p)),
            pl.BlockSpec((1, S, LANE), lambda b, p, i: (b, 0, npair + p)),
            pl.BlockSpec((1, S, LANE), lambda b, p, i: (b, 0, 2 * npair + p)),
        ],
        out_specs=pl.BlockSpec((1, t, LANE), lambda b, p, i: (b, i, p)),
        out_shape=jax.ShapeDtypeStruct((B, S, SB_W), BF16),
        compiler_params=_cparams(("parallel", "parallel", "arbitrary")),
        name="sb_attn",
    )(zsb, zsb, zsb)


def _softmax_tile(s, vt, m, l, acc, mask):
    if mask is not None:
        s = jnp.where(mask, s, NEG_INF)
    m_new = jnp.maximum(m, jnp.max(s, axis=-1, keepdims=True))
    alpha = jnp.exp(m - m_new)
    p = jnp.exp(s - m_new)
    if mask is not None:
        p = jnp.where(mask, p, 0.0)
    l = alpha * l + jnp.sum(p, axis=-1, keepdims=True)
    acc = alpha * acc + _dot(p.astype(BF16), vt)
    return m_new, l, acc


def _mla_kernel(q_ref, k_ref, v_ref, o_ref, *, t):
    i = pl.program_id(2)
    lane = lax.broadcasted_iota(jnp.int32, (t, LANE), 1)
    row = lax.broadcasted_iota(jnp.int32, (t, t), 0)
    col = lax.broadcasted_iota(jnp.int32, (t, t), 1)
    causal = col <= row
    outs = []
    for h in range(2):
        qh = q_ref[0, :, h * LANE:(h + 1) * LANE]

        def tile(j, c, mask):
            off = pl.multiple_of(j * t, t)
            kt = k_ref[0, pl.ds(off, t), h * LANE:(h + 1) * LANE]
            vt = v_ref[0, pl.ds(off, t), :]
            return _softmax_tile(_dot_nt(qh, kt), vt, c[0], c[1], c[2], mask)

        init = (jnp.full((t, 1), NEG_INF, F32), jnp.zeros((t, 1), F32), jnp.zeros((t, LANE), F32))
        c = lax.fori_loop(0, i, lambda j, c: tile(j, c, None), init)
        m, l, acc = tile(i, c, causal)
        outs.append(acc / jnp.maximum(l, 1e-30))
    o_ref[0] = jnp.where(lane < MLA_V, outs[0], outs[1]).astype(o_ref.dtype)


def _mla_attention(qm, km, vm):
    B, S, _ = qm.shape
    t = MLA_T
    npair = MLA_HEADS // 2
    return pl.pallas_call(
        functools.partial(_mla_kernel, t=t),
        grid=(B, npair, S // t),
        in_specs=[
            pl.BlockSpec((1, t, 2 * LANE), lambda b, p, i: (b, i, p)),
            pl.BlockSpec((1, S, 2 * LANE), lambda b, p, i: (b, 0, p)),
            pl.BlockSpec((1, S, LANE), lambda b, p, i: (b, 0, p)),
        ],
        out_specs=pl.BlockSpec((1, t, LANE), lambda b, p, i: (b, i, p)),
        out_shape=jax.ShapeDtypeStruct((B, S, MLA_HEADS * MLA_V), BF16),
        compiler_params=_cparams(("parallel", "parallel", "arbitrary")),
        name="mla_attn",
    )(qm, km, vm)


def _stack_heads(q, lane):
    parts = []
    for h in range(NSA_REP):
        blk = q[:, (h // 2) * LANE:(h // 2 + 1) * LANE]
        parts.append(jnp.where(_head_lanes(lane, h % 2), blk, jnp.zeros_like(blk)))
    return parts


def _pair_lanes(lane, a, b):
    return jnp.where(lane < NSA_HD, a, b)


def _nsa_cmp_kernel(q_ref, kc_ref, vc_ref, ov_ref, ocmp_ref, bias_ref, *, t, n_slc):
    i = pl.program_id(2)
    lane = lax.broadcasted_iota(jnp.int32, (t, LANE), 1)
    row = lax.broadcasted_iota(jnp.int32, (t, LANE), 0)
    pos = i * t + row
    visible = lane * CMP_STRIDE + (CMP_LEN - 1) <= pos
    kc = kc_ref[0]
    vc = vc_ref[0]
    psum = jnp.zeros((t, LANE), F32)
    outs = []
    for qh in _stack_heads(q_ref[0], lane):
        s = jnp.where(visible, _dot_nt(qh, kc), NEG_INF)
        m = jnp.max(s, axis=-1, keepdims=True)
        e = jnp.where(visible, jnp.exp(s - m), 0.0)
        p = e / jnp.maximum(jnp.sum(e, axis=-1, keepdims=True), 1e-30)
        psum = psum + p
        outs.append(_dot(p.astype(BF16), vc))
    ocmp_ref[0, :, 0:LANE] = _pair_lanes(lane, outs[0], outs[1])
    ocmp_ref[0, :, LANE:2 * LANE] = _pair_lanes(lane, outs[2], outs[3])
    hi = psum.astype(BF16)
    lo = (psum - hi.astype(F32)).astype(BF16)
    imp = _dot(hi, ov_ref[...]) + _dot(lo, ov_ref[...])
    cur = pos // SLC_LEN
    forced = (lane == 0) | (lane == cur) | (lane == cur - 1)
    score = jnp.where(lane > cur, -1.0, jnp.where(forced, FORCE_SCORE, imp))
    score = jnp.where(lane < n_slc, score, -2.0)
    rank = jnp.zeros((t, LANE), F32)
    for c in range(n_slc):
        sc = score[:, c:c + 1]
        before = (sc > score) | ((sc == score) & (lane > c))
        rank = rank + jnp.where(before, 1.0, 0.0)
    sel = (rank < float(min(SLC_TOPK, n_slc))) & (lane < n_slc)
    bias_ref[0, 0] = jnp.where(sel, 0.0, NEG_INF).astype(BF16)


def _nsa_cmp(qn, kc, vc, overlap):
    B, S, _ = qn.shape
    t = ATT_T
    gw = NSA_REP * NSA_HD
    return pl.pallas_call(
        functools.partial(_nsa_cmp_kernel, t=t, n_slc=S // SLC_LEN),
        grid=(B, NSA_GROUPS, S // t),
        in_specs=[
            pl.BlockSpec((1, t, gw), lambda b, g, i: (b, i, g)),
            pl.BlockSpec((1, kc.shape[1], LANE), lambda b, g, i: (b, 0, g)),
            pl.BlockSpec((1, vc.shape[1], LANE), lambda b, g, i: (b, 0, g)),
            pl.BlockSpec((LANE, LANE), lambda b, g, i: (0, 0)),
        ],
        out_specs=[pl.BlockSpec((1, t, gw), lambda b, g, i: (b, i, g)),
                   pl.BlockSpec((1, 1, t, LANE), lambda b, g, i: (b, g, i, 0))],
        out_shape=[jax.ShapeDtypeStruct((B, S, NSA_HEADS * NSA_HD), F32),
                   jax.ShapeDtypeStruct((B, NSA_GROUPS, S, LANE), BF16)],
        compiler_params=_cparams(("parallel", "parallel", "parallel")),
        name="nsa_cmp",
    )(qn, kc, vc, overlap)


def _nsa_attn_kernel(q_ref, sk_ref, sv_ref, wk_ref, wv_ref, bias_ref, blk_ref, ocmp_ref, gate_ref,
                     o_ref, *, t):
    i = pl.program_id(2)
    r4 = NSA_REP * t
    lane = lax.broadcasted_iota(jnp.int32, (t, LANE), 1)
    row = lax.broadcasted_iota(jnp.int32, (r4, t), 0) & (t - 1)
    col = lax.broadcasted_iota(jnp.int32, (r4, t), 1)
    causal = col <= row
    q4 = jnp.concatenate(_stack_heads(q_ref[0], lane), axis=0)
    bias4 = jnp.concatenate([bias_ref[0, 0]] * NSA_REP, axis=0)
    q4b = jnp.concatenate([q4, bias4], axis=1)
    init = (jnp.full((r4, 1), NEG_INF, F32), jnp.zeros((r4, 1), F32), jnp.zeros((r4, LANE), F32))

    def slc_tile(j, c, mask):
        off = pl.multiple_of(j * t, t)
        kt = jnp.concatenate([sk_ref[0, pl.ds(off, t), :], blk_ref[pl.ds(off, t), :]], axis=1)
        return _softmax_tile(_dot_nt(q4b, kt), sv_ref[0, pl.ds(off, t), :], c[0], c[1], c[2], mask)

    c = lax.fori_loop(0, i, lambda j, c: slc_tile(j, c, None), init)
    _, l_s, acc_s = slc_tile(i, c, causal)
    o_s = acc_s / jnp.maximum(l_s, 1e-30)

    nback = WIN // t

    def win_tile(j, c, mask):
        off = pl.multiple_of(j * t, t)
        return _softmax_tile(_dot_nt(q4, wk_ref[0, pl.ds(off, t), :]), wv_ref[0, pl.ds(off, t), :],
                             c[0], c[1], c[2], mask)

    far = (col - row) > jnp.where(i >= nback, 0, t)
    c = win_tile(jnp.maximum(i - nback, 0), init, far)
    c = lax.fori_loop(jnp.maximum(i - nback + 1, 0), i, lambda j, c: win_tile(j, c, None), c)
    _, l_w, acc_w = win_tile(i, c, causal)
    o_w = acc_w / jnp.maximum(l_w, 1e-30)

    gate = gate_ref[0]
    ocmp = ocmp_ref[0]
    for p in range(NSA_REP // 2):
        def g(br):
            c0 = br * NSA_REP + 2 * p
            return _pair_lanes(lane, gate[:, c0:c0 + 1], gate[:, c0 + 1:c0 + 2])
        a, b = 2 * p, 2 * p + 1
        o = (g(0) * ocmp[:, p * LANE:(p + 1) * LANE]
             + g(1) * _pair_lanes(lane, o_s[a * t:(a + 1) * t], o_s[b * t:(b + 1) * t])
             + g(2) * _pair_lanes(lane, o_w[a * t:(a + 1) * t], o_w[b * t:(b + 1) * t]))
        o_ref[0, :, p * LANE:(p + 1) * LANE] = o.astype(o_ref.dtype)


def _nsa_attention(qn, sk, sv, wk, wv, bias, blk_onehot, ocmp, gates):
    B, S, _ = qn.shape
    t = ATT_T
    gw = NSA_REP * NSA_HD
    kv = pl.BlockSpec((1, S, LANE), lambda b, g, i: (b, 0, g))
    return pl.pallas_call(
        functools.partial(_nsa_attn_kernel, t=t),
        grid=(B, NSA_GROUPS, S // t),
        in_specs=[
            pl.BlockSpec((1, t, gw), lambda b, g, i: (b, i, g)),
            kv, kv, kv, kv,
            pl.BlockSpec((1, 1, t, LANE), lambda b, g, i: (b, g, i, 0)),
            pl.BlockSpec((S, LANE), lambda b, g, i: (0, 0)),
            pl.BlockSpec((1, t, gw), lambda b, g, i: (b, i, g)),
            pl.BlockSpec((1, t, LANE), lambda b, g, i: (b, i, g)),
        ],
        out_specs=pl.BlockSpec((1, t, gw), lambda b, g, i: (b, i, g)),
        out_shape=jax.ShapeDtypeStruct((B, S, NSA_HEADS * NSA_HD), BF16),
        compiler_params=_cparams(("parallel", "parallel", "arbitrary")),
        name="nsa_attn",
    )(qn, sk, sv, wk, wv, bias, blk_onehot, ocmp, gates)


def _merge_kernel(x_ref, g_ref, osb_ref, omla_ref, onsa_ref, wg_ref, psb_ref, pmla_ref, pnsa_ref,
                  wout_ref, o_ref):
    x = x_ref[...]
    h = _rms(x, g_ref[...]).astype(BF16)
    y = jnp.zeros(x.shape, F32)
    for br, (o_r, p_r) in enumerate(((osb_ref, psb_ref), (omla_ref, pmla_ref), (onsa_ref, pnsa_ref))):
        gate = jax.nn.sigmoid(_dot(h, wg_ref[br]))
        y = y + gate * _dot(o_r[...], p_r[...])
    o_ref[...] = x + _dot(y.astype(BF16), wout_ref[...])


def _merge(x, g, osb, omla, onsa, w):
    T, D = x.shape
    tm = MERGE_TM
    tok = lambda width: pl.BlockSpec((tm, width), lambda i: (i, 0))
    full = lambda shape: pl.BlockSpec(shape, lambda i: (0,) * len(shape))
    return pl.pallas_call(
        _merge_kernel,
        grid=(T // tm,),
        in_specs=[tok(D), full((1, D)), tok(osb.shape[1]), tok(omla.shape[1]), tok(onsa.shape[1]),
                  full((N_BRANCH, D, D)), full(w["proj_sb"].shape), full(w["proj_mla"].shape),
                  full(w["proj_nsa"].shape), full((D, D))],
        out_specs=tok(D),
        out_shape=jax.ShapeDtypeStruct((T, D), F32),
        compiler_params=_cparams(("parallel",)),
        name="merge",
    )(x, g, osb, omla, onsa, w["w_gate"], w["proj_sb"], w["proj_mla"], w["proj_nsa"], w["w_out"])


def _rope_tables(pos, d, seg_starts):
    half = d // 2
    inv = jnp.exp(-math.log(ROPE_THETA) * jnp.arange(half, dtype=F32) * (2.0 / d))
    ang = pos.astype(F32)[:, None] * inv[None, :]
    cos, sin = jnp.cos(ang), jnp.sin(ang)
    n = pos.shape[0]
    zeros = jnp.zeros((n, half), F32)
    covered = sorted(seg_starts)
    c_parts, s1_parts, s2_parts = [], [], []
    lane = 0
    for st in covered:
        if st > lane:
            fill = jnp.ones((n, st - lane), F32)
            c_parts.append(fill)
            s1_parts.append(0.0 * fill)
            s2_parts.append(0.0 * fill)
        c_parts += [cos, cos]
        s1_parts += [-sin, zeros]
        s2_parts += [zeros, sin]
        lane = st + d
    if lane < LANE:
        fill = jnp.zeros((n, LANE - lane), F32)
        c_parts.append(fill)
        s1_parts.append(fill)
        s2_parts.append(fill)
    cat = lambda parts: jnp.concatenate(parts, axis=1)
    return cat(c_parts), cat(s1_parts), cat(s2_parts)


def _tables(S):
    pos = jnp.arange(S, dtype=jnp.int32)
    nchunk = S // CMP_STRIDE
    ends = jnp.arange(nchunk, dtype=jnp.int32) * CMP_STRIDE + (CMP_LEN - 1)
    t = {}
    t["nsa_c"], t["nsa_s1"], t["nsa_s2"] = _rope_tables(pos, NSA_HD, (0, NSA_HD))
    t["cmp_c"], t["cmp_s1"], t["cmp_s2"] = _rope_tables(ends, NSA_HD, (0, NSA_HD))
    t["mla_c"], t["mla_s1"], t["mla_s2"] = _rope_tables(pos, MLA_ROPE, (MLA_NOPE,))
    t["blk_onehot"] = (pos[:, None] // SLC_LEN == jnp.arange(LANE, dtype=jnp.int32)[None, :]).astype(BF16)
    n_cmp = (S - CMP_LEN) // CMP_STRIDE + 1
    n_slc = S // SLC_LEN
    c0 = np.arange(n_cmp)[:, None] * CMP_STRIDE
    s0 = np.arange(n_slc)[None, :] * SLC_LEN
    ov = np.clip(np.minimum(c0 + CMP_LEN, s0 + SLC_LEN) - np.maximum(c0, s0), 0, None) / CMP_LEN
    full = np.zeros((LANE, LANE), np.float32)
    full[:n_cmp, :n_slc] = ov
    t["overlap"] = jnp.asarray(full, BF16)
    return t


def _pad_cols(w, width):
    return jnp.pad(w, ((0, 0), (0, width - w.shape[1])))


def _dup_groups(w):
    parts = []
    for g in range(NSA_GROUPS):
        blk = w[:, g * NSA_HD:(g + 1) * NSA_HD]
        parts += [blk, blk]
    return jnp.concatenate(parts, axis=1)


def _blockdiag2(a):
    z = jnp.zeros_like(a)
    return jnp.concatenate([jnp.concatenate([a, z], axis=1), jnp.concatenate([z, a], axis=1)], axis=0)


def _layer_weights(l, p):
    w = {}
    row = lambda v: v[l][None, :].astype(F32)
    offs = np.concatenate([[0], np.cumsum(IN_SIZES)])
    cols = [p["w_in"][l][:, offs[k]:offs[k + 1]] for k in range(len(IN_SIZES))]
    (sb_q, sb_k, sb_v, cq, ckv, kr, nq, ck, cv, sk, sv, wk, wv, ngate, mgate) = cols
    sb_scale = 1.0 / math.sqrt(SB_HD)
    w["w_sb"] = jnp.concatenate([sb_q * sb_scale, sb_k, sb_v], axis=1).astype(BF16)
    kr_blk = jnp.pad(kr, ((0, 0), (MLA_NOPE, LANE - MLA_NOPE - MLA_ROPE)))
    ng = ngate.reshape(D_MODEL, N_BRANCH, NSA_GROUPS, NSA_REP)
    ng = jnp.concatenate([_pad_cols(ng[:, :, g, :].reshape(D_MODEL, N_BRANCH * NSA_REP), LANE)
                          for g in range(NSA_GROUPS)], axis=1)
    w["w_z2"] = jnp.concatenate([cq, ckv, kr_blk, nq, ck, cv, _dup_groups(sk), _dup_groups(sv),
                                 _dup_groups(wk), _dup_groups(wv), ng], axis=1).astype(BF16)
    w["w_gate"] = mgate.reshape(D_MODEL, N_BRANCH, D_MODEL).transpose(1, 0, 2).astype(BF16)
    w["mix_norm"] = row(p["mix_norm"])
    w["mla_q_norm"] = row(p["mla_q_norm"])
    w["mla_kv_norm"] = row(p["mla_kv_norm"])
    wuq = p["mla_w_uq"][l].reshape(MLA_Q_LORA, MLA_HEADS, MLA_QK)
    w["wuq"] = jnp.pad(wuq, ((0, 0), (0, 0), (0, LANE - MLA_QK))).reshape(MLA_Q_LORA, -1).astype(BF16)
    wukv = p["mla_w_ukv"][l].reshape(MLA_KV_LORA, MLA_HEADS, MLA_NOPE + MLA_V)
    w["wuk"] = jnp.pad(wukv[:, :, :MLA_NOPE], ((0, 0), (0, 0), (0, LANE - MLA_NOPE))).reshape(MLA_KV_LORA, -1).astype(BF16)
    w["wuv"] = wukv[:, :, MLA_NOPE:].reshape(MLA_KV_LORA, -1).astype(BF16)
    w["mla_gq"] = _pad_cols(row(p["mla_qk_gain_q"]), LANE)
    w["mla_gk"] = _pad_cols(row(p["mla_qk_gain_k"]), LANE)
    w["nsa_qg"] = jnp.tile(row(p["nsa_q_gain"]), (1, 2))
    w["nsa_kg"] = jnp.tile(row(p["nsa_k_gain"]), (1, 2))
    w["pe_k"] = jnp.tile(p["cmp_pos_k"][l], (1, 2)).astype(F32)
    w["pe_v"] = jnp.tile(p["cmp_pos_v"][l], (1, 2)).astype(F32)
    for nm, w1, w2 in (("k", p["cmp_wk1"][l], p["cmp_wk2"][l]), ("v", p["cmp_wv1"][l], p["cmp_wv2"][l])):
        w1 = w1.reshape(CMP_LEN, NSA_HD, CMP_HIDDEN)
        bd = jax.vmap(_blockdiag2)(w1).astype(BF16)
        w["w1%s_lo" % nm] = bd[:CMP_STRIDE]
        w["w1%s_hi" % nm] = bd[CMP_STRIDE:]
        z = jnp.zeros_like(w2)
        w["w2" + nm] = jnp.concatenate([jnp.concatenate([w2, w2, z, z], axis=1),
                                        jnp.concatenate([z, z, w2, w2], axis=1)], axis=0).astype(BF16)
    for nm in ("proj_sb", "proj_mla", "proj_nsa", "w_out"):
        w[nm] = p[nm][l].astype(BF16)
    for nm in ("ffn1", "ffn2"):
        w[nm + "_norm"] = row(p[nm + "_norm"])
        w[nm + "_wi"] = p[nm + "_wi"][l].astype(BF16)
        w[nm + "_wo"] = p[nm + "_wo"][l].astype(BF16)
    return w


def kernel(x, ffn1_norm, ffn1_wi, ffn1_wo, mix_norm, w_in, mla_q_norm, mla_w_uq, mla_kv_norm, mla_w_ukv, mla_qk_gain_q, mla_qk_gain_k, nsa_q_gain, nsa_k_gain, cmp_pos_k, cmp_pos_v, cmp_wk1, cmp_wk2, cmp_wv1, cmp_wv2, proj_sb, proj_mla, proj_nsa, w_out, ffn2_norm, ffn2_wi, ffn2_wo):
    p = dict(ffn1_norm=ffn1_norm, ffn1_wi=ffn1_wi, ffn1_wo=ffn1_wo, mix_norm=mix_norm, w_in=w_in,
             mla_q_norm=mla_q_norm, mla_w_uq=mla_w_uq, mla_kv_norm=mla_kv_norm, mla_w_ukv=mla_w_ukv,
             mla_qk_gain_q=mla_qk_gain_q, mla_qk_gain_k=mla_qk_gain_k, nsa_q_gain=nsa_q_gain,
             nsa_k_gain=nsa_k_gain, cmp_pos_k=cmp_pos_k, cmp_pos_v=cmp_pos_v, cmp_wk1=cmp_wk1,
             cmp_wk2=cmp_wk2, cmp_wv1=cmp_wv1, cmp_wv2=cmp_wv2, proj_sb=proj_sb, proj_mla=proj_mla,
             proj_nsa=proj_nsa, w_out=w_out, ffn2_norm=ffn2_norm, ffn2_wi=ffn2_wi, ffn2_wo=ffn2_wo)
    B, S, D = x.shape
    assert S // CMP_STRIDE == LANE and S // SLC_LEN <= LANE and D == D_MODEL
    tabs = _tables(S)
    xt = x.reshape(B * S, D)
    for l in range(ffn1_wi.shape[0]):
        w = _layer_weights(l, p)
        xt = _ffn(xt, w["ffn1_norm"], w["ffn1_wi"], w["ffn1_wo"])
        zsb, z2 = _inproj(xt, w["mix_norm"], w["w_sb"], w["w_z2"])
        zsb = zsb.reshape(B, S, -1)
        z2 = z2.reshape(B, S, -1)
        qm, km, vm, qn, sk, sv, wk, wv, gates = _prep(z2, w, tabs)
        kc, vc = _compress(z2, w, tabs)
        o_sb = _sb_attention(zsb)
        o_mla = _mla_attention(qm, km, vm)
        o_cmp, bias = _nsa_cmp(qn, kc, vc, tabs["overlap"])
        o_nsa = _nsa_attention(qn, sk, sv, wk, wv, bias, tabs["blk_onehot"], o_cmp, gates)
        xt = _merge(xt, w["mix_norm"], o_sb.reshape(B * S, -1), o_mla.reshape(B * S, -1),
                    o_nsa.reshape(B * S, -1), w)
        xt = _ffn(xt, w["ffn2_norm"], w["ffn2_wi"], w["ffn2_wo"])
    return xt.reshape(B, S, D)
```

```python
import functools
import math

import numpy as np
import jax
import jax.numpy as jnp
from jax import lax
from jax.experimental import pallas as pl
from jax.experimental.pallas import tpu as pltpu

D_MODEL = 1024
D_FF = 2816
EPS = 1e-6
ROPE_THETA = 10000.0
N_BRANCH = 3
SB_HEADS = 8
SB_HD = 64
MLA_HEADS = 8
MLA_NOPE = 64
MLA_ROPE = 32
MLA_V = 64
MLA_Q_LORA = 256
MLA_KV_LORA = 128
MLA_QK = MLA_NOPE + MLA_ROPE
NSA_HEADS = 8
NSA_GROUPS = 2
NSA_REP = NSA_HEADS // NSA_GROUPS
NSA_HD = 64
CMP_LEN = 32
CMP_STRIDE = 16
CMP_HIDDEN = 128
SLC_LEN = 64
SLC_TOPK = 16
WIN = 512
FORCE_SCORE = 1e3
NEG_INF = -1e30
SB_W = SB_HEADS * SB_HD
NSA_KV_W = NSA_GROUPS * NSA_HD
IN_SIZES = (SB_W, SB_W, SB_W, MLA_Q_LORA, MLA_KV_LORA, MLA_ROPE, NSA_HEADS * NSA_HD,
            NSA_KV_W, NSA_KV_W, NSA_KV_W, NSA_KV_W, NSA_KV_W, NSA_KV_W,
            N_BRANCH * NSA_HEADS, N_BRANCH * D_MODEL)

LANE = 128
V7X_VMEM_LIMIT = 56 * 1024 * 1024

FFN_TM = 1024
FFN_TF = 256
PROJ_TM = 512
PREP_TS = 512
ATT_T = 128
NSA_TK = 512
SB_T = 256
MLA_TQ = 256
MLA_TK = 512
MERGE_TM = 512

Z2_CQ = 0
Z2_CKV = 256
Z2_KR = 384
Z2_NQ = 512
Z2_CK = 1024
Z2_CV = 1152
Z2_SK = 1280
Z2_SV = 1536
Z2_WK = 1792
Z2_WV = 2048
Z2_GATE = 2304
Z2_W = 2560

BF16 = jnp.bfloat16
F32 = jnp.float32


def _cparams(sem, vmem=V7X_VMEM_LIMIT):
    return pltpu.CompilerParams(dimension_semantics=sem, vmem_limit_bytes=vmem)


def _dot(a, b):
    return jnp.dot(a, b, preferred_element_type=F32)


def _dot_nt(a, b):
    return lax.dot_general(a, b, (((1,), (1,)), ((), ())), preferred_element_type=F32)


def _rms(x, g):
    y = x * lax.rsqrt(jnp.mean(x * x, axis=-1, keepdims=True) + EPS)
    return y * g


def _ffn_kernel(x_ref, g_ref, wa_ref, wb_ref, wo_ref, o_ref, h_ref, acc_ref):
    j = pl.program_id(1)

    @pl.when(j == 0)
    def _():
        h_ref[...] = _rms(x_ref[...], g_ref[...]).astype(BF16)
        acc_ref[...] = jnp.zeros_like(acc_ref)

    h = h_ref[...]
    a = _dot(h, wa_ref[...])
    b = _dot(h, wb_ref[...])
    u = (jax.nn.silu(a) * b).astype(BF16)
    acc_ref[...] += _dot(u, wo_ref[...])

    @pl.when(j == pl.num_programs(1) - 1)
    def _():
        o_ref[...] = x_ref[...] + 0.5 * acc_ref[...]


def _ffn(x, g, wi, wo):
    T, D = x.shape
    F = wo.shape[0]
    tm, tf = FFN_TM, FFN_TF
    nf = F // tf
    return pl.pallas_call(
        _ffn_kernel,
        grid=(T // tm, nf),
        in_specs=[
            pl.BlockSpec((tm, D), lambda i, j: (i, 0)),
            pl.BlockSpec((1, D), lambda i, j: (0, 0)),
            pl.BlockSpec((D, tf), lambda i, j: (0, j)),
            pl.BlockSpec((D, tf), lambda i, j: (0, j + nf)),
            pl.BlockSpec((tf, D), lambda i, j: (j, 0)),
        ],
        out_specs=pl.BlockSpec((tm, D), lambda i, j: (i, 0)),
        out_shape=jax.ShapeDtypeStruct((T, D), F32),
        scratch_shapes=[pltpu.VMEM((tm, D), BF16), pltpu.VMEM((tm, D), F32)],
        compiler_params=_cparams(("parallel", "arbitrary")),
        name="ffn",
    )(x, g, wi, wi, wo)


def _inproj_kernel(x_ref, g_ref, wa_ref, wb_ref, oa_ref, ob_ref):
    h = _rms(x_ref[...], g_ref[...]).astype(BF16)
    oa_ref[...] = _dot(h, wa_ref[...]).astype(oa_ref.dtype)
    ob_ref[...] = _dot(h, wb_ref[...]).astype(ob_ref.dtype)


def _inproj(x, g, wa, wb):
    T, D = x.shape
    tm = PROJ_TM
    na, nb = wa.shape[1], wb.shape[1]
    return pl.pallas_call(
        _inproj_kernel,
        grid=(T // tm,),
        in_specs=[
            pl.BlockSpec((tm, D), lambda i: (i, 0)),
            pl.BlockSpec((1, D), lambda i: (0, 0)),
            pl.BlockSpec((D, na), lambda i: (0, 0)),
            pl.BlockSpec((D, nb), lambda i: (0, 0)),
        ],
        out_specs=[pl.BlockSpec((tm, na), lambda i: (i, 0)),
                   pl.BlockSpec((tm, nb), lambda i: (i, 0))],
        out_shape=[jax.ShapeDtypeStruct((T, na), BF16), jax.ShapeDtypeStruct((T, nb), F32)],
        compiler_params=_cparams(("parallel",)),
        name="inproj",
    )(x, g, wa, wb)


def _rope_mix(y, c, s1, s2, half):
    return y * c + pltpu.roll(y, LANE - half, 1) * s1 + pltpu.roll(y, half, 1) * s2


def _norm_rope_pair(x, gain, c, s1, s2, scale):
    lane = lax.broadcasted_iota(jnp.int32, x.shape, 1)
    lo = lane < NSA_HD
    sq = x * x
    ss_lo = jnp.sum(jnp.where(lo, sq, 0.0), axis=-1, keepdims=True)
    ss_hi = jnp.sum(jnp.where(lo, 0.0, sq), axis=-1, keepdims=True)
    r = jnp.where(lo, lax.rsqrt(ss_lo * (1.0 / NSA_HD) + EPS), lax.rsqrt(ss_hi * (1.0 / NSA_HD) + EPS))
    y = x * r * gain
    y = _rope_mix(y, c, s1, s2, NSA_HD // 2)
    return y * scale if scale != 1.0 else y


def _norm_rope_mla(x, gain, c, s1, s2, scale):
    ss = jnp.sum(x * x, axis=-1, keepdims=True)
    y = x * lax.rsqrt(ss * (1.0 / MLA_QK) + EPS) * gain
    y = _rope_mix(y, c, s1, s2, MLA_ROPE // 2)
    return y * scale if scale != 1.0 else y


def _prep_kernel(z_ref, qn_ref, kvn_ref, wuq_ref, wuk_ref, wuv_ref, gq_ref, gk_ref,
                 mc_ref, ms1_ref, ms2_ref, nqg_ref, nkg_ref, nc_ref, ns1_ref, ns2_ref,
                 qm_ref, km_ref, vm_ref, qn_out, sk_out, sv_out, wk_out, wv_out, gate_out):
    zs = lambda start, width: z_ref[0, :, start:start + width]
    mc, ms1, ms2 = mc_ref[...], ms1_ref[...], ms2_ref[...]
    nc, ns1, ns2 = nc_ref[...], ns1_ref[...], ns2_ref[...]
    cq = _rms(zs(Z2_CQ, MLA_Q_LORA), qn_ref[...]).astype(BF16)
    ckv = _rms(zs(Z2_CKV, MLA_KV_LORA), kvn_ref[...]).astype(BF16)
    kr = zs(Z2_KR, LANE)
    q = _dot(cq, wuq_ref[...])
    kn = _dot(ckv, wuk_ref[...])
    v = _dot(ckv, wuv_ref[...])
    vlane = lax.broadcasted_iota(jnp.int32, (v.shape[0], LANE), 1)
    q_scale = 1.0 / math.sqrt(MLA_QK)
    for h in range(MLA_HEADS):
        sl = slice(h * LANE, (h + 1) * LANE)
        vm_ref[0, :, sl] = jnp.where(vlane < MLA_V, v[:, sl], 1.0).astype(BF16)
        qm_ref[0, :, sl] = _norm_rope_mla(q[:, sl], gq_ref[...], mc, ms1, ms2, q_scale).astype(BF16)
        km_ref[0, :, sl] = _norm_rope_mla(kn[:, sl] + kr, gk_ref[...], mc, ms1, ms2, 1.0).astype(BF16)
    nq_scale = 1.0 / math.sqrt(NSA_HD)
    for b in range(NSA_HEADS * NSA_HD // LANE):
        sl = slice(b * LANE, (b + 1) * LANE)
        qn_out[0, :, sl] = _norm_rope_pair(zs(Z2_NQ + b * LANE, LANE),
                                           nqg_ref[...], nc, ns1, ns2, nq_scale).astype(BF16)
    for b in range(2):
        sl = slice(b * LANE, (b + 1) * LANE)
        sk_out[0, :, sl] = _norm_rope_pair(zs(Z2_SK + b * LANE, LANE),
                                           nkg_ref[...], nc, ns1, ns2, 1.0).astype(BF16)
        wk_out[0, :, sl] = _norm_rope_pair(zs(Z2_WK + b * LANE, LANE),
                                           nkg_ref[...], nc, ns1, ns2, 1.0).astype(BF16)
    glane = lax.broadcasted_iota(jnp.int32, (z_ref.shape[1], 2 * LANE), 1) & (LANE - 1)
    sv_out[0] = jnp.where(glane < NSA_HD, zs(Z2_SV, 2 * LANE), 1.0).astype(BF16)
    wv_out[0] = jnp.where(glane < NSA_HD, zs(Z2_WV, 2 * LANE), 1.0).astype(BF16)
    gate_out[0] = jax.nn.sigmoid(zs(Z2_GATE, 2 * LANE))


def _prep(z2, w, tabs):
    B, S, _ = z2.shape
    ts = PREP_TS
    full = lambda shape: pl.BlockSpec(shape, lambda b, i: (0,) * len(shape))
    tab = pl.BlockSpec((ts, LANE), lambda b, i: (i, 0))
    tok = lambda width: pl.BlockSpec((1, ts, width), lambda b, i: (b, i, 0))
    out_w = (MLA_HEADS * LANE, MLA_HEADS * LANE, MLA_HEADS * LANE, NSA_HEADS * NSA_HD,
             2 * LANE, 2 * LANE, 2 * LANE, 2 * LANE, 2 * LANE)
    out_dt = (BF16,) * 8 + (F32,)
    return pl.pallas_call(
        _prep_kernel,
        grid=(B, S // ts),
        in_specs=[
            tok(Z2_W),
            full((1, MLA_Q_LORA)), full((1, MLA_KV_LORA)),
            full((MLA_Q_LORA, MLA_HEADS * LANE)), full((MLA_KV_LORA, MLA_HEADS * LANE)),
            full((MLA_KV_LORA, MLA_HEADS * LANE)),
            full((1, LANE)), full((1, LANE)),
            tab, tab, tab,
            full((1, LANE)), full((1, LANE)),
            tab, tab, tab,
        ],
        out_specs=[tok(wd) for wd in out_w],
        out_shape=[jax.ShapeDtypeStruct((B, S, wd), dt) for wd, dt in zip(out_w, out_dt)],
        compiler_params=_cparams(("parallel", "parallel")),
        name="prep",
    )(z2, w["mla_q_norm"], w["mla_kv_norm"], w["wuq"], w["wuk"], w["wuv"], w["mla_gq"], w["mla_gk"],
      tabs["mla_c"], tabs["mla_s1"], tabs["mla_s2"], w["nsa_qg"], w["nsa_kg"],
      tabs["nsa_c"], tabs["nsa_s1"], tabs["nsa_s2"])


def _compress_kernel(ck_ref, cv_ref, pek_ref, pev_ref, w1kl_ref, w1kh_ref, w2k_ref,
                     w1vl_ref, w1vh_ref, w2v_ref, kg_ref, c_ref, s1_ref, s2_ref,
                     kc_ref, vc_ref, *, nchunk):
    def branch(t_ref, pe_ref, w1l_ref, w1h_ref, w2_ref):
        lo = jnp.zeros((nchunk, 2 * CMP_HIDDEN), F32)
        hi = jnp.zeros((nchunk, 2 * CMP_HIDDEN), F32)
        for l in range(CMP_STRIDE):
            x = t_ref[0, pl.ds(l, nchunk, stride=CMP_STRIDE), :]
            lo = lo + _dot((x + pe_ref[l:l + 1, :]).astype(BF16), w1l_ref[l])
            hi = hi + _dot((x + pe_ref[CMP_STRIDE + l:CMP_STRIDE + l + 1, :]).astype(BF16), w1h_ref[l])
        pre = lo + pltpu.roll(hi, nchunk - 1, 0)
        return _dot(jax.nn.gelu(pre).astype(BF16), w2_ref[...])

    kc = branch(ck_ref, pek_ref, w1kl_ref, w1kh_ref, w2k_ref)
    vc = branch(cv_ref, pev_ref, w1vl_ref, w1vh_ref, w2v_ref)
    c, s1, s2 = c_ref[...], s1_ref[...], s2_ref[...]
    for b in range(2):
        sl = slice(b * LANE, (b + 1) * LANE)
        kc_ref[0, :, sl] = _norm_rope_pair(kc[:, sl], kg_ref[...], c, s1, s2, 1.0).astype(BF16)
    vc_ref[0] = vc.astype(BF16)


def _compress(z2, w, tabs):
    B, S, _ = z2.shape
    nchunk = S // CMP_STRIDE
    full = lambda shape: pl.BlockSpec(shape, lambda b: (0,) * len(shape))
    tab = full((nchunk, LANE))
    return pl.pallas_call(
        functools.partial(_compress_kernel, nchunk=nchunk),
        grid=(B,),
        in_specs=[
            pl.BlockSpec((1, S, LANE), lambda b: (b, 0, Z2_CK // LANE)),
            pl.BlockSpec((1, S, LANE), lambda b: (b, 0, Z2_CV // LANE)),
            full((CMP_LEN, LANE)), full((CMP_LEN, LANE)),
            full((CMP_STRIDE, LANE, 2 * CMP_HIDDEN)), full((CMP_STRIDE, LANE, 2 * CMP_HIDDEN)),
            full((2 * CMP_HIDDEN, 2 * LANE)),
            full((CMP_STRIDE, LANE, 2 * CMP_HIDDEN)), full((CMP_STRIDE, LANE, 2 * CMP_HIDDEN)),
            full((2 * CMP_HIDDEN, 2 * LANE)),
            full((1, LANE)), tab, tab, tab,
        ],
        out_specs=[pl.BlockSpec((1, nchunk, 2 * LANE), lambda b: (b, 0, 0))] * 2,
        out_shape=[jax.ShapeDtypeStruct((B, nchunk, 2 * LANE), BF16)] * 2,
        compiler_params=_cparams(("parallel",)),
        name="compress",
    )(z2, z2, w["pe_k"], w["pe_v"], w["w1k_lo"], w["w1k_hi"], w["w2k"],
      w["w1v_lo"], w["w1v_hi"], w["w2v"], w["nsa_kg"], tabs["cmp_c"], tabs["cmp_s1"], tabs["cmp_s2"])


def _head_lanes(lane, half):
    return (lane >= NSA_HD * half) & (lane < NSA_HD * (half + 1))


def _sb_kernel(q_ref, k_ref, v_ref, o_ref, *, t):
    i = pl.program_id(2)
    lane = lax.broadcasted_iota(jnp.int32, (t, LANE), 1)
    row = lax.broadcasted_iota(jnp.int32, (t, t), 0)
    col = lax.broadcasted_iota(jnp.int32, (t, t), 1)
    upper = jnp.where(row > col, 1.0, 0.0).astype(BF16)
    strict = (lax.broadcasted_iota(jnp.int32, (2 * t, t), 1)
              < (lax.broadcasted_iota(jnp.int32, (2 * t, t), 0) & (t - 1)))
    q = q_ref[0]
    q2 = jnp.concatenate([jnp.where(_head_lanes(lane, h), q, jnp.zeros_like(q)) for h in range(2)], axis=0)

    def tile(j, carry, acc, masked):
        off = pl.multiple_of(j * t, t)
        z = _dot_nt(q2, k_ref[0, pl.ds(off, t), :])
        lg = jnp.log(1.0 + jnp.exp(-jnp.abs(z)))
        logsig = jnp.minimum(z, 0.0) - lg
        lom = logsig - z
        if masked:
            lom = jnp.where(strict, lom, 0.0)
        hi = lom.astype(BF16)
        lo = (lom - hi.astype(F32)).astype(BF16)
        suffix = _dot(hi, upper) + _dot(lo, upper)
        wgt = jnp.exp(logsig + suffix + carry)
        if masked:
            wgt = jnp.where(strict, wgt, 0.0)
        acc = acc + _dot(wgt.astype(BF16), v_ref[0, pl.ds(off, t), :])
        carry = carry + jnp.sum(lom, axis=-1, keepdims=True)
        return carry, acc

    carry, acc = tile(i, jnp.zeros((2 * t, 1), F32), jnp.zeros((2 * t, LANE), F32), True)
    carry, acc = lax.fori_loop(0, i, lambda s, c: tile(i - 1 - s, c[0], c[1], False), (carry, acc))
    o_ref[0] = jnp.where(lane < NSA_HD, acc[:t], acc[t:]).astype(o_ref.dtype)


def _sb_attention(zsb):
    B, S, _ = zsb.shape
    t = SB_T
    npair = SB_W // LANE
    return pl.pallas_call(
        functools.partial(_sb_kernel, t=t),
        grid=(B, npair, S // t),
        in_specs=[
            pl.BlockSpec((1, t, LANE), lambda b, p, i: (b, i, p)),
            pl.BlockSpec((1, S, LANE), lambda b, p, i: (b, 0, npair + p)),
            pl.BlockSpec((1, S, LANE), lambda b, p, i: (b, 0, 2 * npair + p)),
        ],
        out_specs=pl.BlockSpec((1, t, LANE), lambda b, p, i: (b, i, p)),
        out_shape=jax.ShapeDtypeStruct((B, S, SB_W), BF16),
        compiler_params=_cparams(("parallel", "parallel", "arbitrary")),
        name="sb_attn",
    )(zsb, zsb, zsb)


def _softmax_tile(s, vt, m, l, acc, mask):
    if mask is not None:
        s = jnp.where(mask, s, NEG_INF)
    m_new = jnp.maximum(m, jnp.max(s, axis=-1, keepdims=True))
    alpha = jnp.exp(m - m_new)
    p = jnp.exp(s - m_new)
    if mask is not None:
        p = jnp.where(mask, p, 0.0)
    l = alpha * l + jnp.sum(p, axis=-1, keepdims=True)
    acc = alpha * acc + _dot(p.astype(BF16), vt)
    return m_new, l, acc


def _flash_step(s, vt_ones, m, acc):
    m_new = jnp.maximum(m, jnp.max(s, axis=-1, keepdims=True))
    alpha = jnp.exp(m - m_new)
    p = jnp.exp(s - m_new).astype(BF16)
    return m_new, alpha * acc + _dot(p, vt_ones)


def _flash_finish(acc):
    return acc / jnp.maximum(pltpu.roll(acc, NSA_HD, 1), 1e-30)


def _mla_kernel(q_ref, k_ref, v_ref, o_ref, *, tq, tk):
    i = pl.program_id(2)
    lane = lax.broadcasted_iota(jnp.int32, (tq, LANE), 1)
    col_minus_row = (lax.broadcasted_iota(jnp.int32, (tq, tk), 1)
                     - lax.broadcasted_iota(jnp.int32, (tq, tk), 0))
    tail = (i * tq) // tk
    thr = i * tq - tail * tk

    def tile(j, c, masked):
        off = pl.multiple_of(j * tk, tk)
        out = []
        for h in range(2):
            sl = slice(h * LANE, (h + 1) * LANE)
            s = _dot_nt(q_ref[0, :, sl], k_ref[0, pl.ds(off, tk), sl])
            if masked:
                s = jnp.where(col_minus_row <= thr, s, NEG_INF)
            out += list(_flash_step(s, v_ref[0, pl.ds(off, tk), sl], c[2 * h], c[2 * h + 1]))
        return tuple(out)

    m0 = jnp.full((tq, 1), NEG_INF, F32)
    a0 = jnp.zeros((tq, LANE), F32)
    c = lax.fori_loop(0, tail, lambda j, c: tile(j, c, False), (m0, a0, m0, a0))
    c = tile(tail, c, True)
    o_ref[0] = jnp.where(lane < MLA_V, _flash_finish(c[1]),
                         pltpu.roll(_flash_finish(c[3]), MLA_V, 1)).astype(o_ref.dtype)


def _mla_attention(qm, km, vm):
    B, S, _ = qm.shape
    t = MLA_TQ
    npair = MLA_HEADS // 2
    return pl.pallas_call(
        functools.partial(_mla_kernel, tq=t, tk=MLA_TK),
        grid=(B, npair, S // t),
        in_specs=[
            pl.BlockSpec((1, t, 2 * LANE), lambda b, p, i: (b, i, p)),
            pl.BlockSpec((1, S, 2 * LANE), lambda b, p, i: (b, 0, p)),
            pl.BlockSpec((1, S, 2 * LANE), lambda b, p, i: (b, 0, p)),
        ],
        out_specs=pl.BlockSpec((1, t, LANE), lambda b, p, i: (b, i, p)),
        out_shape=jax.ShapeDtypeStruct((B, S, MLA_HEADS * MLA_V), BF16),
        compiler_params=_cparams(("parallel", "parallel", "arbitrary")),
        name="mla_attn",
    )(qm, km, vm)


def _stack_heads(q, lane):
    parts = []
    for h in range(NSA_REP):
        blk = q[:, (h // 2) * LANE:(h // 2 + 1) * LANE]
        parts.append(jnp.where(_head_lanes(lane, h % 2), blk, jnp.zeros_like(blk)))
    return parts


def _pair_lanes(lane, a, b):
    return jnp.where(lane < NSA_HD, a, b)


def _nsa_cmp_kernel(q_ref, kc_ref, vc_ref, ov_ref, ocmp_ref, bias_ref, *, t, n_slc):
    i = pl.program_id(2)
    lane = lax.broadcasted_iota(jnp.int32, (t, LANE), 1)
    row = lax.broadcasted_iota(jnp.int32, (t, LANE), 0)
    pos = i * t + row
    visible = lane * CMP_STRIDE + (CMP_LEN - 1) <= pos
    kc = kc_ref[0]
    vc = vc_ref[0]
    psum = jnp.zeros((t, LANE), F32)
    outs = []
    for qh in _stack_heads(q_ref[0], lane):
        s = jnp.where(visible, _dot_nt(qh, kc), NEG_INF)
        m = jnp.max(s, axis=-1, keepdims=True)
        e = jnp.where(visible, jnp.exp(s - m), 0.0)
        p = e / jnp.maximum(jnp.sum(e, axis=-1, keepdims=True), 1e-30)
        psum = psum + p
        outs.append(_dot(p.astype(BF16), vc))
    ocmp_ref[0, :, 0:LANE] = _pair_lanes(lane, outs[0], outs[1])
    ocmp_ref[0, :, LANE:2 * LANE] = _pair_lanes(lane, outs[2], outs[3])
    hi = psum.astype(BF16)
    lo = (psum - hi.astype(F32)).astype(BF16)
    imp = _dot(hi, ov_ref[...]) + _dot(lo, ov_ref[...])
    cur = pos // SLC_LEN
    forced = (lane == 0) | (lane == cur) | (lane == cur - 1)
    score = jnp.where(lane > cur, -1.0, jnp.where(forced, FORCE_SCORE, imp))
    score = jnp.where(lane < n_slc, score, -2.0)
    rank = jnp.zeros((t, LANE), F32)
    for c in range(n_slc):
        sc = score[:, c:c + 1]
        before = (sc > score) | ((sc == score) & (lane > c))
        rank = rank + jnp.where(before, 1.0, 0.0)
    sel = (rank < float(min(SLC_TOPK, n_slc))) & (lane < n_slc)
    bias_ref[0, 0] = jnp.where(sel, 0.0, NEG_INF).astype(BF16)


def _nsa_cmp(qn, kc, vc, overlap):
    B, S, _ = qn.shape
    t = ATT_T
    gw = NSA_REP * NSA_HD
    return pl.pallas_call(
        functools.partial(_nsa_cmp_kernel, t=t, n_slc=S // SLC_LEN),
        grid=(B, NSA_GROUPS, S // t),
        in_specs=[
            pl.BlockSpec((1, t, gw), lambda b, g, i: (b, i, g)),
            pl.BlockSpec((1, kc.shape[1], LANE), lambda b, g, i: (b, 0, g)),
            pl.BlockSpec((1, vc.shape[1], LANE), lambda b, g, i: (b, 0, g)),
            pl.BlockSpec((LANE, LANE), lambda b, g, i: (0, 0)),
        ],
        out_specs=[pl.BlockSpec((1, t, gw), lambda b, g, i: (b, i, g)),
                   pl.BlockSpec((1, 1, t, LANE), lambda b, g, i: (b, g, i, 0))],
        out_shape=[jax.ShapeDtypeStruct((B, S, NSA_HEADS * NSA_HD), F32),
                   jax.ShapeDtypeStruct((B, NSA_GROUPS, S, LANE), BF16)],
        compiler_params=_cparams(("parallel", "parallel", "parallel")),
        name="nsa_cmp",
    )(qn, kc, vc, overlap)


def _nsa_attn_kernel(q_ref, sk_ref, sv_ref, wk_ref, wv_ref, bias_ref, blk_ref, ocmp_ref, gate_ref,
                     o_ref, *, t, tk):
    i = pl.program_id(2)
    r4 = NSA_REP * t
    lane = lax.broadcasted_iota(jnp.int32, (t, LANE), 1)
    q4 = jnp.concatenate(_stack_heads(q_ref[0], lane), axis=0)
    bias4 = jnp.concatenate([bias_ref[0, 0]] * NSA_REP, axis=0)
    q4b = jnp.concatenate([q4, bias4], axis=1)

    def row_minus_col(width):
        row = lax.broadcasted_iota(jnp.int32, (r4, width), 0) & (t - 1)
        return row - lax.broadcasted_iota(jnp.int32, (r4, width), 1)

    tail = (i * t) // tk
    thr = i * t - tail * tk

    def slc_tile(j, c, masked):
        off = pl.multiple_of(j * tk, tk)
        kt = jnp.concatenate([sk_ref[0, pl.ds(off, tk), :], blk_ref[pl.ds(off, tk), :]], axis=1)
        s = _dot_nt(q4b, kt)
        if masked:
            s = jnp.where(row_minus_col(tk) >= -thr, s, NEG_INF)
        return _flash_step(s, sv_ref[0, pl.ds(off, tk), :], c[0], c[1])

    init = (jnp.full((r4, 1), NEG_INF, F32), jnp.zeros((r4, LANE), F32))
    c = lax.fori_loop(0, tail, lambda j, c: slc_tile(j, c, False), init)
    o_s = _flash_finish(slc_tile(tail, c, True)[1])

    wkeys = WIN + t
    start = pl.multiple_of(jnp.maximum(i * t - WIN, 0), t)
    dist = row_minus_col(wkeys) + (i * t - start)
    s = _dot_nt(q4, wk_ref[0, pl.ds(start, wkeys), :])
    s = jnp.where((dist >= 0) & (dist < WIN), s, NEG_INF)
    p = jnp.exp(s - jnp.max(s, axis=-1, keepdims=True)).astype(BF16)
    o_w = _flash_finish(_dot(p, wv_ref[0, pl.ds(start, wkeys), :]))

    gate = gate_ref[0]
    ocmp = ocmp_ref[0]
    for p in range(NSA_REP // 2):
        def g(br):
            c0 = br * NSA_REP + 2 * p
            return _pair_lanes(lane, gate[:, c0:c0 + 1], gate[:, c0 + 1:c0 + 2])

        def pair(o):
            a, b = 2 * p, 2 * p + 1
            return _pair_lanes(lane, o[a * t:(a + 1) * t], pltpu.roll(o[b * t:(b + 1) * t], NSA_HD, 1))

        o = g(0) * ocmp[:, p * LANE:(p + 1) * LANE] + g(1) * pair(o_s) + g(2) * pair(o_w)
        o_ref[0, :, p * LANE:(p + 1) * LANE] = o.astype(o_ref.dtype)


def _nsa_attention(qn, sk, sv, wk, wv, bias, blk_onehot, ocmp, gates):
    B, S, _ = qn.shape
    t = ATT_T
    gw = NSA_REP * NSA_HD
    kv = pl.BlockSpec((1, S, LANE), lambda b, g, i: (b, 0, g))
    return pl.pallas_call(
        functools.partial(_nsa_attn_kernel, t=t, tk=NSA_TK),
        grid=(B, NSA_GROUPS, S // t),
        in_specs=[
            pl.BlockSpec((1, t, gw), lambda b, g, i: (b, i, g)),
            kv, kv, kv, kv,
            pl.BlockSpec((1, 1, t, LANE), lambda b, g, i: (b, g, i, 0)),
            pl.BlockSpec((S, LANE), lambda b, g, i: (0, 0)),
            pl.BlockSpec((1, t, gw), lambda b, g, i: (b, i, g)),
            pl.BlockSpec((1, t, LANE), lambda b, g, i: (b, i, g)),
        ],
        out_specs=pl.BlockSpec((1, t, gw), lambda b, g, i: (b, i, g)),
        out_shape=jax.ShapeDtypeStruct((B, S, NSA_HEADS * NSA_HD), BF16),
        compiler_params=_cparams(("parallel", "parallel", "arbitrary")),
        name="nsa_attn",
    )(qn, sk, sv, wk, wv, bias, blk_onehot, ocmp, gates)


def _merge_kernel(x_ref, g_ref, osb_ref, omla_ref, onsa_ref, wg_ref, psb_ref, pmla_ref, pnsa_ref,
                  wout_ref, o_ref):
    x = x_ref[...]
    h = _rms(x, g_ref[...]).astype(BF16)
    y = jnp.zeros(x.shape, F32)
    for br, (o_r, p_r) in enumerate(((osb_ref, psb_ref), (omla_ref, pmla_ref), (onsa_ref, pnsa_ref))):
        gate = jax.nn.sigmoid(_dot(h, wg_ref[br]))
        y = y + gate * _dot(o_r[...], p_r[...])
    o_ref[...] = x + _dot(y.astype(BF16), wout_ref[...])


def _merge(x, g, osb, omla, onsa, w):
    T, D = x.shape
    tm = MERGE_TM
    tok = lambda width: pl.BlockSpec((tm, width), lambda i: (i, 0))
    full = lambda shape: pl.BlockSpec(shape, lambda i: (0,) * len(shape))
    return pl.pallas_call(
        _merge_kernel,
        grid=(T // tm,),
        in_specs=[tok(D), full((1, D)), tok(osb.shape[1]), tok(omla.shape[1]), tok(onsa.shape[1]),
                  full((N_BRANCH, D, D)), full(w["proj_sb"].shape), full(w["proj_mla"].shape),
                  full(w["proj_nsa"].shape), full((D, D))],
        out_specs=tok(D),
        out_shape=jax.ShapeDtypeStruct((T, D), F32),
        compiler_params=_cparams(("parallel",)),
        name="merge",
    )(x, g, osb, omla, onsa, w["w_gate"], w["proj_sb"], w["proj_mla"], w["proj_nsa"], w["w_out"])


def _rope_tables(pos, d, seg_starts):
    half = d // 2
    inv = jnp.exp(-math.log(ROPE_THETA) * jnp.arange(half, dtype=F32) * (2.0 / d))
    ang = pos.astype(F32)[:, None] * inv[None, :]
    cos, sin = jnp.cos(ang), jnp.sin(ang)
    n = pos.shape[0]
    zeros = jnp.zeros((n, half), F32)
    covered = sorted(seg_starts)
    c_parts, s1_parts, s2_parts = [], [], []
    lane = 0
    for st in covered:
        if st > lane:
            fill = jnp.ones((n, st - lane), F32)
            c_parts.append(fill)
            s1_parts.append(0.0 * fill)
            s2_parts.append(0.0 * fill)
        c_parts += [cos, cos]
        s1_parts += [-sin, zeros]
        s2_parts += [zeros, sin]
        lane = st + d
    if lane < LANE:
        fill = jnp.zeros((n, LANE - lane), F32)
        c_parts.append(fill)
        s1_parts.append(fill)
        s2_parts.append(fill)
    cat = lambda parts: jnp.concatenate(parts, axis=1)
    return cat(c_parts), cat(s1_parts), cat(s2_parts)


def _tables(S):
    pos = jnp.arange(S, dtype=jnp.int32)
    nchunk = S // CMP_STRIDE
    ends = jnp.arange(nchunk, dtype=jnp.int32) * CMP_STRIDE + (CMP_LEN - 1)
    t = {}
    t["nsa_c"], t["nsa_s1"], t["nsa_s2"] = _rope_tables(pos, NSA_HD, (0, NSA_HD))
    t["cmp_c"], t["cmp_s1"], t["cmp_s2"] = _rope_tables(ends, NSA_HD, (0, NSA_HD))
    t["mla_c"], t["mla_s1"], t["mla_s2"] = _rope_tables(pos, MLA_ROPE, (MLA_NOPE,))
    t["blk_onehot"] = (pos[:, None] // SLC_LEN == jnp.arange(LANE, dtype=jnp.int32)[None, :]).astype(BF16)
    n_cmp = (S - CMP_LEN) // CMP_STRIDE + 1
    n_slc = S // SLC_LEN
    c0 = np.arange(n_cmp)[:, None] * CMP_STRIDE
    s0 = np.arange(n_slc)[None, :] * SLC_LEN
    ov = np.clip(np.minimum(c0 + CMP_LEN, s0 + SLC_LEN) - np.maximum(c0, s0), 0, None) / CMP_LEN
    full = np.zeros((LANE, LANE), np.float32)
    full[:n_cmp, :n_slc] = ov
    t["overlap"] = jnp.asarray(full, BF16)
    return t


def _pad_cols(w, width):
    return jnp.pad(w, ((0, 0), (0, width - w.shape[1])))


def _dup_groups(w):
    parts = []
    for g in range(NSA_GROUPS):
        blk = w[:, g * NSA_HD:(g + 1) * NSA_HD]
        parts += [blk, blk]
    return jnp.concatenate(parts, axis=1)


def _blockdiag2(a):
    z = jnp.zeros_like(a)
    return jnp.concatenate([jnp.concatenate([a, z], axis=1), jnp.concatenate([z, a], axis=1)], axis=0)


def _layer_weights(l, p):
    w = {}
    row = lambda v: v[l][None, :].astype(F32)
    offs = np.concatenate([[0], np.cumsum(IN_SIZES)])
    cols = [p["w_in"][l][:, offs[k]:offs[k + 1]] for k in range(len(IN_SIZES))]
    (sb_q, sb_k, sb_v, cq, ckv, kr, nq, ck, cv, sk, sv, wk, wv, ngate, mgate) = cols
    sb_scale = 1.0 / math.sqrt(SB_HD)
    w["w_sb"] = jnp.concatenate([sb_q * sb_scale, sb_k, sb_v], axis=1).astype(BF16)
    kr_blk = jnp.pad(kr, ((0, 0), (MLA_NOPE, LANE - MLA_NOPE - MLA_ROPE)))
    ng = ngate.reshape(D_MODEL, N_BRANCH, NSA_GROUPS, NSA_REP)
    ng = jnp.concatenate([_pad_cols(ng[:, :, g, :].reshape(D_MODEL, N_BRANCH * NSA_REP), LANE)
                          for g in range(NSA_GROUPS)], axis=1)
    w["w_z2"] = jnp.concatenate([cq, ckv, kr_blk, nq, ck, cv, _dup_groups(sk), _dup_groups(sv),
                                 _dup_groups(wk), _dup_groups(wv), ng], axis=1).astype(BF16)
    w["w_gate"] = mgate.reshape(D_MODEL, N_BRANCH, D_MODEL).transpose(1, 0, 2).astype(BF16)
    w["mix_norm"] = row(p["mix_norm"])
    w["mla_q_norm"] = row(p["mla_q_norm"])
    w["mla_kv_norm"] = row(p["mla_kv_norm"])
    wuq = p["mla_w_uq"][l].reshape(MLA_Q_LORA, MLA_HEADS, MLA_QK)
    w["wuq"] = jnp.pad(wuq, ((0, 0), (0, 0), (0, LANE - MLA_QK))).reshape(MLA_Q_LORA, -1).astype(BF16)
    wukv = p["mla_w_ukv"][l].reshape(MLA_KV_LORA, MLA_HEADS, MLA_NOPE + MLA_V)
    w["wuk"] = jnp.pad(wukv[:, :, :MLA_NOPE], ((0, 0), (0, 0), (0, LANE - MLA_NOPE))).reshape(MLA_KV_LORA, -1).astype(BF16)
    w["wuv"] = jnp.pad(wukv[:, :, MLA_NOPE:], ((0, 0), (0, 0), (0, LANE - MLA_V))).reshape(MLA_KV_LORA, -1).astype(BF16)
    w["mla_gq"] = _pad_cols(row(p["mla_qk_gain_q"]), LANE)
    w["mla_gk"] = _pad_cols(row(p["mla_qk_gain_k"]), LANE)
    w["nsa_qg"] = jnp.tile(row(p["nsa_q_gain"]), (1, 2))
    w["nsa_kg"] = jnp.tile(row(p["nsa_k_gain"]), (1, 2))
    w["pe_k"] = jnp.tile(p["cmp_pos_k"][l], (1, 2)).astype(F32)
    w["pe_v"] = jnp.tile(p["cmp_pos_v"][l], (1, 2)).astype(F32)
    for nm, w1, w2 in (("k", p["cmp_wk1"][l], p["cmp_wk2"][l]), ("v", p["cmp_wv1"][l], p["cmp_wv2"][l])):
        w1 = w1.reshape(CMP_LEN, NSA_HD, CMP_HIDDEN)
        bd = jax.vmap(_blockdiag2)(w1).astype(BF16)
        w["w1%s_lo" % nm] = bd[:CMP_STRIDE]
        w["w1%s_hi" % nm] = bd[CMP_STRIDE:]
        z = jnp.zeros_like(w2)
        w["w2" + nm] = jnp.concatenate([jnp.concatenate([w2, w2, z, z], axis=1),
                                        jnp.concatenate([z, z, w2, w2], axis=1)], axis=0).astype(BF16)
    for nm in ("proj_sb", "proj_mla", "proj_nsa", "w_out"):
        w[nm] = p[nm][l].astype(BF16)
    for nm in ("ffn1", "ffn2"):
        w[nm + "_norm"] = row(p[nm + "_norm"])
        w[nm + "_wi"] = p[nm + "_wi"][l].astype(BF16)
        w[nm + "_wo"] = p[nm + "_wo"][l].astype(BF16)
    return w


def kernel(x, ffn1_norm, ffn1_wi, ffn1_wo, mix_norm, w_in, mla_q_norm, mla_w_uq, mla_kv_norm, mla_w_ukv, mla_qk_gain_q, mla_qk_gain_k, nsa_q_gain, nsa_k_gain, cmp_pos_k, cmp_pos_v, cmp_wk1, cmp_wk2, cmp_wv1, cmp_wv2, proj_sb, proj_mla, proj_nsa, w_out, ffn2_norm, ffn2_wi, ffn2_wo):
    p = dict(ffn1_norm=ffn1_norm, ffn1_wi=ffn1_wi, ffn1_wo=ffn1_wo, mix_norm=mix_norm, w_in=w_in,
             mla_q_norm=mla_q_norm, mla_w_uq=mla_w_uq, mla_kv_norm=mla_kv_norm, mla_w_ukv=mla_w_ukv,
             mla_qk_gain_q=mla_qk_gain_q, mla_qk_gain_k=mla_qk_gain_k, nsa_q_gain=nsa_q_gain,
             nsa_k_gain=nsa_k_gain, cmp_pos_k=cmp_pos_k, cmp_pos_v=cmp_pos_v, cmp_wk1=cmp_wk1,
             cmp_wk2=cmp_wk2, cmp_wv1=cmp_wv1, cmp_wv2=cmp_wv2, proj_sb=proj_sb, proj_mla=proj_mla,
             proj_nsa=proj_nsa, w_out=w_out, ffn2_norm=ffn2_norm, ffn2_wi=ffn2_wi, ffn2_wo=ffn2_wo)
    B, S, D = x.shape
    assert S // CMP_STRIDE == LANE and S // SLC_LEN <= LANE and D == D_MODEL
    tabs = _tables(S)
    xt = x.reshape(B * S, D)
    for l in range(ffn1_wi.shape[0]):
        w = _layer_weights(l, p)
        xt = _ffn(xt, w["ffn1_norm"], w["ffn1_wi"], w["ffn1_wo"])
        zsb, z2 = _inproj(xt, w["mix_norm"], w["w_sb"], w["w_z2"])
        zsb = zsb.reshape(B, S, -1)
        z2 = z2.reshape(B, S, -1)
        qm, km, vm, qn, sk, sv, wk, wv, gates = _prep(z2, w, tabs)
        kc, vc = _compress(z2, w, tabs)
        o_sb = _sb_attention(zsb)
        o_mla = _mla_attention(qm, km, vm)
        o_cmp, bias = _nsa_cmp(qn, kc, vc, tabs["overlap"])
        o_nsa = _nsa_attention(qn, sk, sv, wk, wv, bias, tabs["blk_onehot"], o_cmp, gates)
        xt = _merge(xt, w["mix_norm"], o_sb.reshape(B * S, -1), o_mla.reshape(B * S, -1),
                    o_nsa.reshape(B * S, -1), w)
        xt = _ffn(xt, w["ffn2_norm"], w["ffn2_wi"], w["ffn2_wo"])
    return xt.reshape(B, S, D)
```

```python
import functools
import math

import numpy as np
import jax
import jax.numpy as jnp
from jax import lax
from jax.experimental import pallas as pl
from jax.experimental.pallas import tpu as pltpu

D_MODEL = 1024
D_FF = 2816
EPS = 1e-6
ROPE_THETA = 10000.0
N_BRANCH = 3
SB_HEADS = 8
SB_HD = 64
MLA_HEADS = 8
MLA_NOPE = 64
MLA_ROPE = 32
MLA_V = 64
MLA_Q_LORA = 256
MLA_KV_LORA = 128
MLA_QK = MLA_NOPE + MLA_ROPE
NSA_HEADS = 8
NSA_GROUPS = 2
NSA_REP = NSA_HEADS // NSA_GROUPS
NSA_HD = 64
CMP_LEN = 32
CMP_STRIDE = 16
CMP_HIDDEN = 128
SLC_LEN = 64
SLC_TOPK = 16
WIN = 512
FORCE_SCORE = 1e3
NEG_INF = -1e30
SB_W = SB_HEADS * SB_HD
NSA_KV_W = NSA_GROUPS * NSA_HD
IN_SIZES = (SB_W, SB_W, SB_W, MLA_Q_LORA, MLA_KV_LORA, MLA_ROPE, NSA_HEADS * NSA_HD,
            NSA_KV_W, NSA_KV_W, NSA_KV_W, NSA_KV_W, NSA_KV_W, NSA_KV_W,
            N_BRANCH * NSA_HEADS, N_BRANCH * D_MODEL)

LANE = 128
V7X_VMEM_LIMIT = 56 * 1024 * 1024

FFN_TM = 1024
FFN_TF = 256
PROJ_TM = 512
PREP_TS = 512
ATT_T = 128
NSA_TK = 512
CMP_T = 256
SB_T = 256
SB_HEADS_PER_STEP = 4
MLA_TQ = 256
MLA_HEADS_PER_STEP = 4
MLA_TK = 512
MERGE_TM = 512

Z2_CQ = 0
Z2_CKV = 256
Z2_KR = 384
Z2_NQ = 512
Z2_CK = 1024
Z2_CV = 1152
Z2_SK = 1280
Z2_SV = 1536
Z2_WK = 1792
Z2_WV = 2048
Z2_GATE = 2304
Z2_W = 2560

BF16 = jnp.bfloat16
F32 = jnp.float32


def _cparams(sem, vmem=V7X_VMEM_LIMIT):
    return pltpu.CompilerParams(dimension_semantics=sem, vmem_limit_bytes=vmem)


def _dot(a, b):
    return jnp.dot(a, b, preferred_element_type=F32)


def _dot_nt(a, b):
    return lax.dot_general(a, b, (((1,), (1,)), ((), ())), preferred_element_type=F32)


def _rms(x, g):
    y = x * lax.rsqrt(jnp.mean(x * x, axis=-1, keepdims=True) + EPS)
    return y * g


def _ffn_kernel(x_ref, g_ref, wa_ref, wb_ref, wo_ref, o_ref, h_ref, acc_ref):
    j = pl.program_id(1)

    @pl.when(j == 0)
    def _():
        h_ref[...] = _rms(x_ref[...], g_ref[...]).astype(BF16)
        acc_ref[...] = jnp.zeros_like(acc_ref)

    h = h_ref[...]
    a = _dot(h, wa_ref[...])
    b = _dot(h, wb_ref[...])
    u = (jax.nn.silu(a) * b).astype(BF16)
    acc_ref[...] += _dot(u, wo_ref[...])

    @pl.when(j == pl.num_programs(1) - 1)
    def _():
        o_ref[...] = x_ref[...] + 0.5 * acc_ref[...]


def _ffn(x, g, wi, wo):
    T, D = x.shape
    F = wo.shape[0]
    tm, tf = FFN_TM, FFN_TF
    nf = F // tf
    return pl.pallas_call(
        _ffn_kernel,
        grid=(T // tm, nf),
        in_specs=[
            pl.BlockSpec((tm, D), lambda i, j: (i, 0)),
            pl.BlockSpec((1, D), lambda i, j: (0, 0)),
            pl.BlockSpec((D, tf), lambda i, j: (0, j)),
            pl.BlockSpec((D, tf), lambda i, j: (0, j + nf)),
            pl.BlockSpec((tf, D), lambda i, j: (j, 0)),
        ],
        out_specs=pl.BlockSpec((tm, D), lambda i, j: (i, 0)),
        out_shape=jax.ShapeDtypeStruct((T, D), F32),
        scratch_shapes=[pltpu.VMEM((tm, D), BF16), pltpu.VMEM((tm, D), F32)],
        compiler_params=_cparams(("parallel", "arbitrary")),
        name="ffn",
    )(x, g, wi, wi, wo)


def _inproj_kernel(x_ref, g_ref, wa_ref, wb_ref, oa_ref, ob_ref):
    h = _rms(x_ref[...], g_ref[...]).astype(BF16)
    oa_ref[...] = _dot(h, wa_ref[...]).astype(oa_ref.dtype)
    ob_ref[...] = _dot(h, wb_ref[...]).astype(ob_ref.dtype)


def _inproj(x, g, wa, wb):
    T, D = x.shape
    tm = PROJ_TM
    na, nb = wa.shape[1], wb.shape[1]
    return pl.pallas_call(
        _inproj_kernel,
        grid=(T // tm,),
        in_specs=[
            pl.BlockSpec((tm, D), lambda i: (i, 0)),
            pl.BlockSpec((1, D), lambda i: (0, 0)),
            pl.BlockSpec((D, na), lambda i: (0, 0)),
            pl.BlockSpec((D, nb), lambda i: (0, 0)),
        ],
        out_specs=[pl.BlockSpec((tm, na), lambda i: (i, 0)),
                   pl.BlockSpec((tm, nb), lambda i: (i, 0))],
        out_shape=[jax.ShapeDtypeStruct((T, na), BF16), jax.ShapeDtypeStruct((T, nb), F32)],
        compiler_params=_cparams(("parallel",)),
        name="inproj",
    )(x, g, wa, wb)


def _rope_mix(y, c, s1, s2, half):
    return y * c + pltpu.roll(y, LANE - half, 1) * s1 + pltpu.roll(y, half, 1) * s2


def _norm_rope_pair(x, gain, c, s1, s2, scale):
    lane = lax.broadcasted_iota(jnp.int32, x.shape, 1)
    lo = lane < NSA_HD
    sq = x * x
    ss_lo = jnp.sum(jnp.where(lo, sq, 0.0), axis=-1, keepdims=True)
    ss_hi = jnp.sum(jnp.where(lo, 0.0, sq), axis=-1, keepdims=True)
    r = jnp.where(lo, lax.rsqrt(ss_lo * (1.0 / NSA_HD) + EPS), lax.rsqrt(ss_hi * (1.0 / NSA_HD) + EPS))
    y = x * r * gain
    y = _rope_mix(y, c, s1, s2, NSA_HD // 2)
    return y * scale if scale != 1.0 else y


def _norm_rope_mla(x, gain, c, s1, s2, scale):
    ss = jnp.sum(x * x, axis=-1, keepdims=True)
    y = x * lax.rsqrt(ss * (1.0 / MLA_QK) + EPS) * gain
    y = _rope_mix(y, c, s1, s2, MLA_ROPE // 2)
    return y * scale if scale != 1.0 else y


def _prep_kernel(z_ref, qn_ref, kvn_ref, wuq_ref, wuk_ref, wuv_ref, gq_ref, gk_ref,
                 mc_ref, ms1_ref, ms2_ref, nqg_ref, nkg_ref, nc_ref, ns1_ref, ns2_ref,
                 qm_ref, km_ref, vm_ref, qn_out, sk_out, sv_out, wk_out, wv_out, gate_out):
    zs = lambda start, width: z_ref[0, :, start:start + width]
    mc, ms1, ms2 = mc_ref[...], ms1_ref[...], ms2_ref[...]
    nc, ns1, ns2 = nc_ref[...], ns1_ref[...], ns2_ref[...]
    cq = _rms(zs(Z2_CQ, MLA_Q_LORA), qn_ref[...]).astype(BF16)
    ckv = _rms(zs(Z2_CKV, MLA_KV_LORA), kvn_ref[...]).astype(BF16)
    kr = zs(Z2_KR, LANE)
    q = _dot(cq, wuq_ref[...])
    kn = _dot(ckv, wuk_ref[...])
    v = _dot(ckv, wuv_ref[...])
    vlane = lax.broadcasted_iota(jnp.int32, (v.shape[0], LANE), 1)
    q_scale = 1.0 / math.sqrt(MLA_QK)
    for h in range(MLA_HEADS):
        sl = slice(h * LANE, (h + 1) * LANE)
        vm_ref[0, :, sl] = jnp.where(vlane < MLA_V, v[:, sl], 1.0).astype(BF16)
        qm_ref[0, :, sl] = _norm_rope_mla(q[:, sl], gq_ref[...], mc, ms1, ms2, q_scale).astype(BF16)
        km_ref[0, :, sl] = _norm_rope_mla(kn[:, sl] + kr, gk_ref[...], mc, ms1, ms2, 1.0).astype(BF16)
    nq_scale = 1.0 / math.sqrt(NSA_HD)
    for b in range(NSA_HEADS * NSA_HD // LANE):
        sl = slice(b * LANE, (b + 1) * LANE)
        qn_out[0, :, sl] = _norm_rope_pair(zs(Z2_NQ + b * LANE, LANE),
                                           nqg_ref[...], nc, ns1, ns2, nq_scale).astype(BF16)
    for b in range(2):
        sl = slice(b * LANE, (b + 1) * LANE)
        sk_out[0, :, sl] = _norm_rope_pair(zs(Z2_SK + b * LANE, LANE),
                                           nkg_ref[...], nc, ns1, ns2, 1.0).astype(BF16)
        wk_out[0, :, sl] = _norm_rope_pair(zs(Z2_WK + b * LANE, LANE),
                                           nkg_ref[...], nc, ns1, ns2, 1.0).astype(BF16)
    glane = lax.broadcasted_iota(jnp.int32, (z_ref.shape[1], 2 * LANE), 1) & (LANE - 1)
    sv_out[0] = jnp.where(glane < NSA_HD, zs(Z2_SV, 2 * LANE), 1.0).astype(BF16)
    wv_out[0] = jnp.where(glane < NSA_HD, zs(Z2_WV, 2 * LANE), 1.0).astype(BF16)
    gate_out[0] = jax.nn.sigmoid(zs(Z2_GATE, 2 * LANE))


def _prep(z2, w, tabs):
    B, S, _ = z2.shape
    ts = PREP_TS
    full = lambda shape: pl.BlockSpec(shape, lambda b, i: (0,) * len(shape))
    tab = pl.BlockSpec((ts, LANE), lambda b, i: (i, 0))
    tok = lambda width: pl.BlockSpec((1, ts, width), lambda b, i: (b, i, 0))
    out_w = (MLA_HEADS * LANE, MLA_HEADS * LANE, MLA_HEADS * LANE, NSA_HEADS * NSA_HD,
             2 * LANE, 2 * LANE, 2 * LANE, 2 * LANE, 2 * LANE)
    out_dt = (BF16,) * 8 + (F32,)
    return pl.pallas_call(
        _prep_kernel,
        grid=(B, S // ts),
        in_specs=[
            tok(Z2_W),
            full((1, MLA_Q_LORA)), full((1, MLA_KV_LORA)),
            full((MLA_Q_LORA, MLA_HEADS * LANE)), full((MLA_KV_LORA, MLA_HEADS * LANE)),
            full((MLA_KV_LORA, MLA_HEADS * LANE)),
            full((1, LANE)), full((1, LANE)),
            tab, tab, tab,
            full((1, LANE)), full((1, LANE)),
            tab, tab, tab,
        ],
        out_specs=[tok(wd) for wd in out_w],
        out_shape=[jax.ShapeDtypeStruct((B, S, wd), dt) for wd, dt in zip(out_w, out_dt)],
        compiler_params=_cparams(("parallel", "parallel")),
        name="prep",
    )(z2, w["mla_q_norm"], w["mla_kv_norm"], w["wuq"], w["wuk"], w["wuv"], w["mla_gq"], w["mla_gk"],
      tabs["mla_c"], tabs["mla_s1"], tabs["mla_s2"], w["nsa_qg"], w["nsa_kg"],
      tabs["nsa_c"], tabs["nsa_s1"], tabs["nsa_s2"])


def _compress_kernel(ck_ref, cv_ref, pek_ref, pev_ref, w1kl_ref, w1kh_ref, w2k_ref,
                     w1vl_ref, w1vh_ref, w2v_ref, kg_ref, c_ref, s1_ref, s2_ref,
                     kc_ref, vc_ref, *, nchunk):
    def branch(t_ref, pe_ref, w1l_ref, w1h_ref, w2_ref):
        lo = jnp.zeros((nchunk, 2 * CMP_HIDDEN), F32)
        hi = jnp.zeros((nchunk, 2 * CMP_HIDDEN), F32)
        for l in range(CMP_STRIDE):
            x = t_ref[0, pl.ds(l, nchunk, stride=CMP_STRIDE), :]
            lo = lo + _dot((x + pe_ref[l:l + 1, :]).astype(BF16), w1l_ref[l])
            hi = hi + _dot((x + pe_ref[CMP_STRIDE + l:CMP_STRIDE + l + 1, :]).astype(BF16), w1h_ref[l])
        pre = lo + pltpu.roll(hi, nchunk - 1, 0)
        return _dot(jax.nn.gelu(pre).astype(BF16), w2_ref[...])

    kc = branch(ck_ref, pek_ref, w1kl_ref, w1kh_ref, w2k_ref)
    vc = branch(cv_ref, pev_ref, w1vl_ref, w1vh_ref, w2v_ref)
    c, s1, s2 = c_ref[...], s1_ref[...], s2_ref[...]
    for b in range(2):
        sl = slice(b * LANE, (b + 1) * LANE)
        kc_ref[0, :, sl] = _norm_rope_pair(kc[:, sl], kg_ref[...], c, s1, s2, 1.0).astype(BF16)
    vc_ref[0] = vc.astype(BF16)


def _compress(z2, w, tabs):
    B, S, _ = z2.shape
    nchunk = S // CMP_STRIDE
    full = lambda shape: pl.BlockSpec(shape, lambda b: (0,) * len(shape))
    tab = full((nchunk, LANE))
    return pl.pallas_call(
        functools.partial(_compress_kernel, nchunk=nchunk),
        grid=(B,),
        in_specs=[
            pl.BlockSpec((1, S, LANE), lambda b: (b, 0, Z2_CK // LANE)),
            pl.BlockSpec((1, S, LANE), lambda b: (b, 0, Z2_CV // LANE)),
            full((CMP_LEN, LANE)), full((CMP_LEN, LANE)),
            full((CMP_STRIDE, LANE, 2 * CMP_HIDDEN)), full((CMP_STRIDE, LANE, 2 * CMP_HIDDEN)),
            full((2 * CMP_HIDDEN, 2 * LANE)),
            full((CMP_STRIDE, LANE, 2 * CMP_HIDDEN)), full((CMP_STRIDE, LANE, 2 * CMP_HIDDEN)),
            full((2 * CMP_HIDDEN, 2 * LANE)),
            full((1, LANE)), tab, tab, tab,
        ],
        out_specs=[pl.BlockSpec((1, nchunk, 2 * LANE), lambda b: (b, 0, 0))] * 2,
        out_shape=[jax.ShapeDtypeStruct((B, nchunk, 2 * LANE), BF16)] * 2,
        compiler_params=_cparams(("parallel",)),
        name="compress",
    )(z2, z2, w["pe_k"], w["pe_v"], w["w1k_lo"], w["w1k_hi"], w["w2k"],
      w["w1v_lo"], w["w1v_hi"], w["w2v"], w["nsa_kg"], tabs["cmp_c"], tabs["cmp_s1"], tabs["cmp_s2"])


def _head_lanes(lane, half):
    return (lane >= NSA_HD * half) & (lane < NSA_HD * (half + 1))


def _sb_kernel(q_ref, k_ref, v_ref, o_ref, *, t, nh):
    i = pl.program_id(2)
    w = nh * SB_HD
    lane = lax.broadcasted_iota(jnp.int32, (t, w), 1)
    row = lax.broadcasted_iota(jnp.int32, (t, t), 0)
    col = lax.broadcasted_iota(jnp.int32, (t, t), 1)
    upper = jnp.where(row > col, 1.0, 0.0).astype(BF16)
    strict = (lax.broadcasted_iota(jnp.int32, (nh * t, t), 1)
              < (lax.broadcasted_iota(jnp.int32, (nh * t, t), 0) & (t - 1)))
    q = q_ref[0]
    own = [(lane >= SB_HD * h) & (lane < SB_HD * (h + 1)) for h in range(nh)]
    qs = jnp.concatenate([jnp.where(own[h], q, jnp.zeros_like(q)) for h in range(nh)], axis=0)

    def tile(j, carry, acc, masked):
        off = pl.multiple_of(j * t, t)
        z = _dot_nt(qs, k_ref[0, pl.ds(off, t), :])
        lg = jnp.log(1.0 + jnp.exp(-jnp.abs(z)))
        logsig = jnp.minimum(z, 0.0) - lg
        lom = logsig - z
        if masked:
            lom = jnp.where(strict, lom, 0.0)
        hi = lom.astype(BF16)
        lo = (lom - hi.astype(F32)).astype(BF16)
        suffix = _dot(hi, upper) + _dot(lo, upper)
        wgt = jnp.exp(logsig + suffix + carry)
        if masked:
            wgt = jnp.where(strict, wgt, 0.0)
        acc = acc + _dot(wgt.astype(BF16), v_ref[0, pl.ds(off, t), :])
        carry = carry + jnp.sum(lom, axis=-1, keepdims=True)
        return carry, acc

    carry, acc = tile(i, jnp.zeros((nh * t, 1), F32), jnp.zeros((nh * t, w), F32), True)
    carry, acc = lax.fori_loop(0, i, lambda s, c: tile(i - 1 - s, c[0], c[1], False), (carry, acc))
    out = acc[:t]
    for h in range(1, nh):
        out = jnp.where(own[h], acc[h * t:(h + 1) * t], out)
    o_ref[0] = out.astype(o_ref.dtype)


def _sb_attention(zsb):
    B, S, _ = zsb.shape
    t = SB_T
    nh = SB_HEADS_PER_STEP
    w = nh * SB_HD
    npair = SB_W // w
    return pl.pallas_call(
        functools.partial(_sb_kernel, t=t, nh=nh),
        grid=(B, npair, S // t),
        in_specs=[
            pl.BlockSpec((1, t, w), lambda b, p, i: (b, i, p)),
            pl.BlockSpec((1, S, w), lambda b, p, i: (b, 0, npair + p)),
            pl.BlockSpec((1, S, w), lambda b, p, i: (b, 0, 2 * npair + p)),
        ],
        out_specs=pl.BlockSpec((1, t, w), lambda b, p, i: (b, i, p)),
        out_shape=jax.ShapeDtypeStruct((B, S, SB_W), BF16),
        compiler_params=_cparams(("parallel", "parallel", "arbitrary")),
        name="sb_attn",
    )(zsb, zsb, zsb)


def _softmax_tile(s, vt, m, l, acc, mask):
    if mask is not None:
        s = jnp.where(mask, s, NEG_INF)
    m_new = jnp.maximum(m, jnp.max(s, axis=-1, keepdims=True))
    alpha = jnp.exp(m - m_new)
    p = jnp.exp(s - m_new)
    if mask is not None:
        p = jnp.where(mask, p, 0.0)
    l = alpha * l + jnp.sum(p, axis=-1, keepdims=True)
    acc = alpha * acc + _dot(p.astype(BF16), vt)
    return m_new, l, acc


def _flash_step(s, vt_ones, m, acc):
    m_new = jnp.maximum(m, jnp.max(s, axis=-1, keepdims=True))
    alpha = jnp.exp(m - m_new)
    p = jnp.exp(s - m_new).astype(BF16)
    return m_new, alpha * acc + _dot(p, vt_ones)


def _flash_finish(acc):
    return acc / jnp.maximum(pltpu.roll(acc, NSA_HD, 1), 1e-30)


def _mla_kernel(q_ref, k_ref, v_ref, o_ref, *, tq, tk, nh):
    i = pl.program_id(2)
    lane = lax.broadcasted_iota(jnp.int32, (tq, LANE), 1)
    col_minus_row = (lax.broadcasted_iota(jnp.int32, (tq, tk), 1)
                     - lax.broadcasted_iota(jnp.int32, (tq, tk), 0))
    tail = (i * tq) // tk
    thr = i * tq - tail * tk

    def tile(j, c, masked):
        off = pl.multiple_of(j * tk, tk)
        out = []
        for h in range(nh):
            sl = slice(h * LANE, (h + 1) * LANE)
            s = _dot_nt(q_ref[0, :, sl], k_ref[0, pl.ds(off, tk), sl])
            if masked:
                s = jnp.where(col_minus_row <= thr, s, NEG_INF)
            out += list(_flash_step(s, v_ref[0, pl.ds(off, tk), sl], c[2 * h], c[2 * h + 1]))
        return tuple(out)

    m0 = jnp.full((tq, 1), NEG_INF, F32)
    a0 = jnp.zeros((tq, LANE), F32)
    c = lax.fori_loop(0, tail, lambda j, c: tile(j, c, False), (m0, a0) * nh)
    c = tile(tail, c, True)
    for p in range(nh // 2):
        o = jnp.where(lane < MLA_V, _flash_finish(c[4 * p + 1]),
                      pltpu.roll(_flash_finish(c[4 * p + 3]), MLA_V, 1))
        o_ref[0, :, p * LANE:(p + 1) * LANE] = o.astype(o_ref.dtype)


def _mla_attention(qm, km, vm):
    B, S, _ = qm.shape
    t = MLA_TQ
    nh = MLA_HEADS_PER_STEP
    ngrp = MLA_HEADS // nh
    return pl.pallas_call(
        functools.partial(_mla_kernel, tq=t, tk=MLA_TK, nh=nh),
        grid=(B, ngrp, S // t),
        in_specs=[
            pl.BlockSpec((1, t, nh * LANE), lambda b, p, i: (b, i, p)),
            pl.BlockSpec((1, S, nh * LANE), lambda b, p, i: (b, 0, p)),
            pl.BlockSpec((1, S, nh * LANE), lambda b, p, i: (b, 0, p)),
        ],
        out_specs=pl.BlockSpec((1, t, nh * MLA_V), lambda b, p, i: (b, i, p)),
        out_shape=jax.ShapeDtypeStruct((B, S, MLA_HEADS * MLA_V), BF16),
        compiler_params=_cparams(("parallel", "parallel", "arbitrary")),
        name="mla_attn",
    )(qm, km, vm)


def _stack_heads(q, lane):
    parts = []
    for h in range(NSA_REP):
        blk = q[:, (h // 2) * LANE:(h // 2 + 1) * LANE]
        parts.append(jnp.where(_head_lanes(lane, h % 2), blk, jnp.zeros_like(blk)))
    return parts


def _pair_lanes(lane, a, b):
    return jnp.where(lane < NSA_HD, a, b)


def _tree_sum(terms):
    while len(terms) > 1:
        terms = [a + b for a, b in zip(terms[0::2], terms[1::2])] + (terms[-1:] if len(terms) % 2 else [])
    return terms[0]


def _nsa_cmp_kernel(q_ref, kc_ref, vc_ref, ovt_ref, ocmp_ref, bias_ref, *, t, n_slc):
    i = pl.program_id(1)
    gw = NSA_REP * NSA_HD
    lane = lax.broadcasted_iota(jnp.int32, (t, LANE), 1)
    row = lax.broadcasted_iota(jnp.int32, (t, LANE), 0)
    visible = lane * CMP_STRIDE + (CMP_LEN - 1) <= i * t + row
    blk = lax.broadcasted_iota(jnp.int32, (n_slc, t), 0)
    cur = (i * t + lax.broadcasted_iota(jnp.int32, (n_slc, t), 1)) // SLC_LEN
    forced = (blk == 0) | (blk == cur) | (blk == cur - 1)
    for g in range(NSA_GROUPS):
        kc = kc_ref[0, :, g * LANE:(g + 1) * LANE]
        vc = vc_ref[0, :, g * LANE:(g + 1) * LANE]
        psum = jnp.zeros((t, LANE), F32)
        outs = []
        for qh in _stack_heads(q_ref[0, :, g * gw:(g + 1) * gw], lane):
            s = jnp.where(visible, _dot_nt(qh, kc), NEG_INF)
            m = jnp.max(s, axis=-1, keepdims=True)
            e = jnp.where(visible, jnp.exp(s - m), 0.0)
            p = e / jnp.maximum(jnp.sum(e, axis=-1, keepdims=True), 1e-30)
            psum = psum + p
            outs.append(_dot(p.astype(BF16), vc))
        ocmp_ref[0, :, g * gw:g * gw + LANE] = _pair_lanes(lane, outs[0], outs[1])
        ocmp_ref[0, :, g * gw + LANE:(g + 1) * gw] = _pair_lanes(lane, outs[2], outs[3])
        hi = psum.astype(BF16)
        lo = (psum - hi.astype(F32)).astype(BF16)
        imp = (_dot_nt(ovt_ref[...], hi) + _dot_nt(ovt_ref[...], lo))[:n_slc]
        score = jnp.where(blk > cur, -1.0, jnp.where(forced, FORCE_SCORE, imp))
        terms = []
        for c in range(n_slc):
            sc = score[c:c + 1, :]
            before = (sc > score) | ((sc == score) & (blk > c))
            terms.append(jnp.where(before, 1.0, 0.0))
        bias_t = jnp.where(_tree_sum(terms) < float(min(SLC_TOPK, n_slc)), 0.0, NEG_INF)
        bias_t = jnp.concatenate([bias_t, jnp.full((LANE - n_slc, t), NEG_INF, F32)], axis=0)
        bias_ref[0, g] = bias_t.T.astype(BF16)


def _nsa_cmp(qn, kc, vc, overlap):
    B, S, _ = qn.shape
    t = CMP_T
    nkv = NSA_GROUPS * LANE
    return pl.pallas_call(
        functools.partial(_nsa_cmp_kernel, t=t, n_slc=S // SLC_LEN),
        grid=(B, S // t),
        in_specs=[
            pl.BlockSpec((1, t, NSA_HEADS * NSA_HD), lambda b, i: (b, i, 0)),
            pl.BlockSpec((1, kc.shape[1], nkv), lambda b, i: (b, 0, 0)),
            pl.BlockSpec((1, vc.shape[1], nkv), lambda b, i: (b, 0, 0)),
            pl.BlockSpec((LANE, LANE), lambda b, i: (0, 0)),
        ],
        out_specs=[pl.BlockSpec((1, t, NSA_HEADS * NSA_HD), lambda b, i: (b, i, 0)),
                   pl.BlockSpec((1, NSA_GROUPS, t, LANE), lambda b, i: (b, 0, i, 0))],
        out_shape=[jax.ShapeDtypeStruct((B, S, NSA_HEADS * NSA_HD), F32),
                   jax.ShapeDtypeStruct((B, NSA_GROUPS, S, LANE), BF16)],
        compiler_params=_cparams(("parallel", "parallel")),
        name="nsa_cmp",
    )(qn, kc, vc, overlap)


def _nsa_attn_kernel(q_ref, sk_ref, sv_ref, wk_ref, wv_ref, bias_ref, blk_ref, ocmp_ref, gate_ref,
                     o_ref, *, t, tk):
    i = pl.program_id(1)
    r4 = NSA_REP * t
    gw = NSA_REP * NSA_HD
    groups = range(NSA_GROUPS)
    lane = lax.broadcasted_iota(jnp.int32, (t, LANE), 1)
    gl = [slice(g * LANE, (g + 1) * LANE) for g in groups]
    q4 = [jnp.concatenate(_stack_heads(q_ref[0, :, g * gw:(g + 1) * gw], lane), axis=0) for g in groups]
    q4b = [jnp.concatenate([q4[g], jnp.concatenate([bias_ref[0, g]] * NSA_REP, axis=0)], axis=1)
           for g in groups]

    def row_minus_col(width):
        row = lax.broadcasted_iota(jnp.int32, (r4, width), 0) & (t - 1)
        return row - lax.broadcasted_iota(jnp.int32, (r4, width), 1)

    tail = (i * t) // tk
    thr = i * t - tail * tk

    def slc_tile(j, c, masked):
        off = pl.multiple_of(j * tk, tk)
        onehot = blk_ref[pl.ds(off, tk), :]
        out = []
        for g in groups:
            s = _dot_nt(q4b[g], jnp.concatenate([sk_ref[0, pl.ds(off, tk), gl[g]], onehot], axis=1))
            if masked:
                s = jnp.where(row_minus_col(tk) >= -thr, s, NEG_INF)
            out += list(_flash_step(s, sv_ref[0, pl.ds(off, tk), gl[g]], c[2 * g], c[2 * g + 1]))
        return tuple(out)

    init = (jnp.full((r4, 1), NEG_INF, F32), jnp.zeros((r4, LANE), F32)) * NSA_GROUPS
    c = lax.fori_loop(0, tail, lambda j, c: slc_tile(j, c, False), init)
    c = slc_tile(tail, c, True)

    wkeys = WIN + t
    start = pl.multiple_of(jnp.maximum(i * t - WIN, 0), t)
    dist = row_minus_col(wkeys) + (i * t - start)
    in_window = (dist >= 0) & (dist < WIN)
    for g in groups:
        s = jnp.where(in_window, _dot_nt(q4[g], wk_ref[0, pl.ds(start, wkeys), gl[g]]), NEG_INF)
        p = jnp.exp(s - jnp.max(s, axis=-1, keepdims=True)).astype(BF16)
        o_w = _flash_finish(_dot(p, wv_ref[0, pl.ds(start, wkeys), gl[g]]))
        o_s = _flash_finish(c[2 * g + 1])
        gate = gate_ref[0, :, gl[g]]
        for p2 in range(NSA_REP // 2):
            def gate_pair(br):
                c0 = br * NSA_REP + 2 * p2
                return _pair_lanes(lane, gate[:, c0:c0 + 1], gate[:, c0 + 1:c0 + 2])

            def pair(o):
                a, b = 2 * p2, 2 * p2 + 1
                return _pair_lanes(lane, o[a * t:(a + 1) * t], pltpu.roll(o[b * t:(b + 1) * t], NSA_HD, 1))

            cols = slice(g * gw + p2 * LANE, g * gw + (p2 + 1) * LANE)
            o = gate_pair(0) * ocmp_ref[0, :, cols] + gate_pair(1) * pair(o_s) + gate_pair(2) * pair(o_w)
            o_ref[0, :, cols] = o.astype(o_ref.dtype)


def _nsa_attention(qn, sk, sv, wk, wv, bias, blk_onehot, ocmp, gates):
    B, S, _ = qn.shape
    t = ATT_T
    hw = NSA_HEADS * NSA_HD
    nkv = NSA_GROUPS * LANE
    kv = pl.BlockSpec((1, S, nkv), lambda b, i: (b, 0, 0))
    return pl.pallas_call(
        functools.partial(_nsa_attn_kernel, t=t, tk=NSA_TK),
        grid=(B, S // t),
        in_specs=[
            pl.BlockSpec((1, t, hw), lambda b, i: (b, i, 0)),
            kv, kv, kv, kv,
            pl.BlockSpec((1, NSA_GROUPS, t, LANE), lambda b, i: (b, 0, i, 0)),
            pl.BlockSpec((S, LANE), lambda b, i: (0, 0)),
            pl.BlockSpec((1, t, hw), lambda b, i: (b, i, 0)),
            pl.BlockSpec((1, t, nkv), lambda b, i: (b, i, 0)),
        ],
        out_specs=pl.BlockSpec((1, t, hw), lambda b, i: (b, i, 0)),
        out_shape=jax.ShapeDtypeStruct((B, S, hw), BF16),
        compiler_params=_cparams(("parallel", "arbitrary")),
        name="nsa_attn",
    )(qn, sk, sv, wk, wv, bias, blk_onehot, ocmp, gates)


def _merge_kernel(x_ref, g_ref, osb_ref, omla_ref, onsa_ref, wg_ref, psb_ref, pmla_ref, pnsa_ref,
                  wout_ref, o_ref):
    x = x_ref[...]
    h = _rms(x, g_ref[...]).astype(BF16)
    y = jnp.zeros(x.shape, F32)
    for br, (o_r, p_r) in enumerate(((osb_ref, psb_ref), (omla_ref, pmla_ref), (onsa_ref, pnsa_ref))):
        gate = jax.nn.sigmoid(_dot(h, wg_ref[br]))
        y = y + gate * _dot(o_r[...], p_r[...])
    o_ref[...] = x + _dot(y.astype(BF16), wout_ref[...])


def _merge(x, g, osb, omla, onsa, w):
    T, D = x.shape
    tm = MERGE_TM
    tok = lambda width: pl.BlockSpec((tm, width), lambda i: (i, 0))
    full = lambda shape: pl.BlockSpec(shape, lambda i: (0,) * len(shape))
    return pl.pallas_call(
        _merge_kernel,
        grid=(T // tm,),
        in_specs=[tok(D), full((1, D)), tok(osb.shape[1]), tok(omla.shape[1]), tok(onsa.shape[1]),
                  full((N_BRANCH, D, D)), full(w["proj_sb"].shape), full(w["proj_mla"].shape),
                  full(w["proj_nsa"].shape), full((D, D))],
        out_specs=tok(D),
        out_shape=jax.ShapeDtypeStruct((T, D), F32),
        compiler_params=_cparams(("parallel",)),
        name="merge",
    )(x, g, osb, omla, onsa, w["w_gate"], w["proj_sb"], w["proj_mla"], w["proj_nsa"], w["w_out"])


def _rope_tables(pos, d, seg_starts):
    half = d // 2
    inv = jnp.exp(-math.log(ROPE_THETA) * jnp.arange(half, dtype=F32) * (2.0 / d))
    ang = pos.astype(F32)[:, None] * inv[None, :]
    cos, sin = jnp.cos(ang), jnp.sin(ang)
    n = pos.shape[0]
    zeros = jnp.zeros((n, half), F32)
    covered = sorted(seg_starts)
    c_parts, s1_parts, s2_parts = [], [], []
    lane = 0
    for st in covered:
        if st > lane:
            fill = jnp.ones((n, st - lane), F32)
            c_parts.append(fill)
            s1_parts.append(0.0 * fill)
            s2_parts.append(0.0 * fill)
        c_parts += [cos, cos]
        s1_parts += [-sin, zeros]
        s2_parts += [zeros, sin]
        lane = st + d
    if lane < LANE:
        fill = jnp.zeros((n, LANE - lane), F32)
        c_parts.append(fill)
        s1_parts.append(fill)
        s2_parts.append(fill)
    cat = lambda parts: jnp.concatenate(parts, axis=1)
    return cat(c_parts), cat(s1_parts), cat(s2_parts)


def _tables(S):
    pos = jnp.arange(S, dtype=jnp.int32)
    nchunk = S // CMP_STRIDE
    ends = jnp.arange(nchunk, dtype=jnp.int32) * CMP_STRIDE + (CMP_LEN - 1)
    t = {}
    t["nsa_c"], t["nsa_s1"], t["nsa_s2"] = _rope_tables(pos, NSA_HD, (0, NSA_HD))
    t["cmp_c"], t["cmp_s1"], t["cmp_s2"] = _rope_tables(ends, NSA_HD, (0, NSA_HD))
    t["mla_c"], t["mla_s1"], t["mla_s2"] = _rope_tables(pos, MLA_ROPE, (MLA_NOPE,))
    t["blk_onehot"] = (pos[:, None] // SLC_LEN == jnp.arange(LANE, dtype=jnp.int32)[None, :]).astype(BF16)
    n_cmp = (S - CMP_LEN) // CMP_STRIDE + 1
    n_slc = S // SLC_LEN
    c0 = np.arange(n_cmp)[:, None] * CMP_STRIDE
    s0 = np.arange(n_slc)[None, :] * SLC_LEN
    ov = np.clip(np.minimum(c0 + CMP_LEN, s0 + SLC_LEN) - np.maximum(c0, s0), 0, None) / CMP_LEN
    full = np.zeros((LANE, LANE), np.float32)
    full[:n_slc, :n_cmp] = ov.T
    t["overlap"] = jnp.asarray(full, BF16)
    return t


def _pad_cols(w, width):
    return jnp.pad(w, ((0, 0), (0, width - w.shape[1])))


def _dup_groups(w):
    parts = []
    for g in range(NSA_GROUPS):
        blk = w[:, g * NSA_HD:(g + 1) * NSA_HD]
        parts += [blk, blk]
    return jnp.concatenate(parts, axis=1)


def _blockdiag2(a):
    z = jnp.zeros_like(a)
    return jnp.concatenate([jnp.concatenate([a, z], axis=1), jnp.concatenate([z, a], axis=1)], axis=0)


def _layer_weights(l, p):
    w = {}
    row = lambda v: v[l][None, :].astype(F32)
    offs = np.concatenate([[0], np.cumsum(IN_SIZES)])
    cols = [p["w_in"][l][:, offs[k]:offs[k + 1]] for k in range(len(IN_SIZES))]
    (sb_q, sb_k, sb_v, cq, ckv, kr, nq, ck, cv, sk, sv, wk, wv, ngate, mgate) = cols
    sb_scale = 1.0 / math.sqrt(SB_HD)
    w["w_sb"] = jnp.concatenate([sb_q * sb_scale, sb_k, sb_v], axis=1).astype(BF16)
    kr_blk = jnp.pad(kr, ((0, 0), (MLA_NOPE, LANE - MLA_NOPE - MLA_ROPE)))
    ng = ngate.reshape(D_MODEL, N_BRANCH, NSA_GROUPS, NSA_REP)
    ng = jnp.concatenate([_pad_cols(ng[:, :, g, :].reshape(D_MODEL, N_BRANCH * NSA_REP), LANE)
                          for g in range(NSA_GROUPS)], axis=1)
    w["w_z2"] = jnp.concatenate([cq, ckv, kr_blk, nq, ck, cv, _dup_groups(sk), _dup_groups(sv),
                                 _dup_groups(wk), _dup_groups(wv), ng], axis=1).astype(BF16)
    w["w_gate"] = mgate.reshape(D_MODEL, N_BRANCH, D_MODEL).transpose(1, 0, 2).astype(BF16)
    w["mix_norm"] = row(p["mix_norm"])
    w["mla_q_norm"] = row(p["mla_q_norm"])
    w["mla_kv_norm"] = row(p["mla_kv_norm"])
    wuq = p["mla_w_uq"][l].reshape(MLA_Q_LORA, MLA_HEADS, MLA_QK)
    w["wuq"] = jnp.pad(wuq, ((0, 0), (0, 0), (0, LANE - MLA_QK))).reshape(MLA_Q_LORA, -1).astype(BF16)
    wukv = p["mla_w_ukv"][l].reshape(MLA_KV_LORA, MLA_HEADS, MLA_NOPE + MLA_V)
    w["wuk"] = jnp.pad(wukv[:, :, :MLA_NOPE], ((0, 0), (0, 0), (0, LANE - MLA_NOPE))).reshape(MLA_KV_LORA, -1).astype(BF16)
    w["wuv"] = jnp.pad(wukv[:, :, MLA_NOPE:], ((0, 0), (0, 0), (0, LANE - MLA_V))).reshape(MLA_KV_LORA, -1).astype(BF16)
    w["mla_gq"] = _pad_cols(row(p["mla_qk_gain_q"]), LANE)
    w["mla_gk"] = _pad_cols(row(p["mla_qk_gain_k"]), LANE)
    w["nsa_qg"] = jnp.tile(row(p["nsa_q_gain"]), (1, 2))
    w["nsa_kg"] = jnp.tile(row(p["nsa_k_gain"]), (1, 2))
    w["pe_k"] = jnp.tile(p["cmp_pos_k"][l], (1, 2)).astype(F32)
    w["pe_v"] = jnp.tile(p["cmp_pos_v"][l], (1, 2)).astype(F32)
    for nm, w1, w2 in (("k", p["cmp_wk1"][l], p["cmp_wk2"][l]), ("v", p["cmp_wv1"][l], p["cmp_wv2"][l])):
        w1 = w1.reshape(CMP_LEN, NSA_HD, CMP_HIDDEN)
        bd = jax.vmap(_blockdiag2)(w1).astype(BF16)
        w["w1%s_lo" % nm] = bd[:CMP_STRIDE]
        w["w1%s_hi" % nm] = bd[CMP_STRIDE:]
        z = jnp.zeros_like(w2)
        w["w2" + nm] = jnp.concatenate([jnp.concatenate([w2, w2, z, z], axis=1),
                                        jnp.concatenate([z, z, w2, w2], axis=1)], axis=0).astype(BF16)
    for nm in ("proj_sb", "proj_mla", "proj_nsa", "w_out"):
        w[nm] = p[nm][l].astype(BF16)
    for nm in ("ffn1", "ffn2"):
        w[nm + "_norm"] = row(p[nm + "_norm"])
        w[nm + "_wi"] = p[nm + "_wi"][l].astype(BF16)
        w[nm + "_wo"] = p[nm + "_wo"][l].astype(BF16)
    return w


def kernel(x, ffn1_norm, ffn1_wi, ffn1_wo, mix_norm, w_in, mla_q_norm, mla_w_uq, mla_kv_norm, mla_w_ukv, mla_qk_gain_q, mla_qk_gain_k, nsa_q_gain, nsa_k_gain, cmp_pos_k, cmp_pos_v, cmp_wk1, cmp_wk2, cmp_wv1, cmp_wv2, proj_sb, proj_mla, proj_nsa, w_out, ffn2_norm, ffn2_wi, ffn2_wo):
    p = dict(ffn1_norm=ffn1_norm, ffn1_wi=ffn1_wi, ffn1_wo=ffn1_wo, mix_norm=mix_norm, w_in=w_in,
             mla_q_norm=mla_q_norm, mla_w_uq=mla_w_uq, mla_kv_norm=mla_kv_norm, mla_w_ukv=mla_w_ukv,
             mla_qk_gain_q=mla_qk_gain_q, mla_qk_gain_k=mla_qk_gain_k, nsa_q_gain=nsa_q_gain,
             nsa_k_gain=nsa_k_gain, cmp_pos_k=cmp_pos_k, cmp_pos_v=cmp_pos_v, cmp_wk1=cmp_wk1,
             cmp_wk2=cmp_wk2, cmp_wv1=cmp_wv1, cmp_wv2=cmp_wv2, proj_sb=proj_sb, proj_mla=proj_mla,
             proj_nsa=proj_nsa, w_out=w_out, ffn2_norm=ffn2_norm, ffn2_wi=ffn2_wi, ffn2_wo=ffn2_wo)
    B, S, D = x.shape
    assert S // CMP_STRIDE == LANE and S // SLC_LEN <= LANE and D == D_MODEL
    tabs = _tables(S)
    xt = x.reshape(B * S, D)
    for l in range(ffn1_wi.shape[0]):
        w = _layer_weights(l, p)
        xt = _ffn(xt, w["ffn1_norm"], w["ffn1_wi"], w["ffn1_wo"])
        zsb, z2 = _inproj(xt, w["mix_norm"], w["w_sb"], w["w_z2"])
        zsb = zsb.reshape(B, S, -1)
        z2 = z2.reshape(B, S, -1)
        qm, km, vm, qn, sk, sv, wk, wv, gates = _prep(z2, w, tabs)
        kc, vc = _compress(z2, w, tabs)
        o_sb = _sb_attention(zsb)
        o_mla = _mla_attention(qm, km, vm)
        o_cmp, bias = _nsa_cmp(qn, kc, vc, tabs["overlap"])
        o_nsa = _nsa_attention(qn, sk, sv, wk, wv, bias, tabs["blk_onehot"], o_cmp, gates)
        xt = _merge(xt, w["mix_norm"], o_sb.reshape(B * S, -1), o_mla.reshape(B * S, -1),
                    o_nsa.reshape(B * S, -1), w)
        xt = _ffn(xt, w["ffn2_norm"], w["ffn2_wi"], w["ffn2_wo"])
    return xt.reshape(B, S, D)
```

```python
import functools
import math

import numpy as np
import jax
import jax.numpy as jnp
from jax import lax
from jax.experimental import pallas as pl
from jax.experimental.pallas import tpu as pltpu

D_MODEL = 1024
D_FF = 2816
EPS = 1e-6
ROPE_THETA = 10000.0
N_BRANCH = 3
SB_HEADS = 8
SB_HD = 64
MLA_HEADS = 8
MLA_NOPE = 64
MLA_ROPE = 32
MLA_V = 64
MLA_Q_LORA = 256
MLA_KV_LORA = 128
MLA_QK = MLA_NOPE + MLA_ROPE
NSA_HEADS = 8
NSA_GROUPS = 2
NSA_REP = NSA_HEADS // NSA_GROUPS
NSA_HD = 64
CMP_LEN = 32
CMP_STRIDE = 16
CMP_HIDDEN = 128
SLC_LEN = 64
SLC_TOPK = 16
WIN = 512
FORCE_SCORE = 1e3
NEG_INF = -1e30
SB_W = SB_HEADS * SB_HD
NSA_KV_W = NSA_GROUPS * NSA_HD
IN_SIZES = (SB_W, SB_W, SB_W, MLA_Q_LORA, MLA_KV_LORA, MLA_ROPE, NSA_HEADS * NSA_HD,
            NSA_KV_W, NSA_KV_W, NSA_KV_W, NSA_KV_W, NSA_KV_W, NSA_KV_W,
            N_BRANCH * NSA_HEADS, N_BRANCH * D_MODEL)

LANE = 128
V7X_VMEM_LIMIT = 56 * 1024 * 1024

FFN_TM = 1024
FFN_TF = 256
PROJ_TM = 512
PREP_TS = 512
ATT_T = 128
NSA_TK = 512
CMP_T = 256
SB_T = 256
SB_HEADS_PER_STEP = 4
MLA_TQ = 256
MLA_HEADS_PER_STEP = 4
MLA_TK = 512
MERGE_TM = 512

Z2_CQ = 0
Z2_CKV = 256
Z2_KR = 384
Z2_NQ = 512
Z2_CK = 1024
Z2_CV = 1152
Z2_SK = 1280
Z2_SV = 1536
Z2_WK = 1792
Z2_WV = 2048
Z2_GATE = 2304
Z2_W = 2560

BF16 = jnp.bfloat16
F32 = jnp.float32


def _cparams(sem, vmem=V7X_VMEM_LIMIT):
    return pltpu.CompilerParams(dimension_semantics=sem, vmem_limit_bytes=vmem)


def _dot(a, b):
    return jnp.dot(a, b, preferred_element_type=F32)


def _dot_nt(a, b):
    return lax.dot_general(a, b, (((1,), (1,)), ((), ())), preferred_element_type=F32)


def _split_rows(dot, a, b):
    h = a.shape[0] // 2
    return jnp.concatenate([dot(a[:h], b), dot(a[h:], b)], axis=0)


def _rms(x, g):
    y = x * lax.rsqrt(jnp.mean(x * x, axis=-1, keepdims=True) + EPS)
    return y * g


def _ffn_kernel(x_ref, g_ref, wa_ref, wb_ref, wo_ref, o_ref, h_ref, acc_ref):
    j = pl.program_id(1)

    @pl.when(j == 0)
    def _():
        h_ref[...] = _rms(x_ref[...], g_ref[...]).astype(BF16)
        acc_ref[...] = jnp.zeros_like(acc_ref)

    h = h_ref[...]
    a = _dot(h, wa_ref[...])
    b = _dot(h, wb_ref[...])
    u = (jax.nn.silu(a) * b).astype(BF16)
    acc_ref[...] += _dot(u, wo_ref[...])

    @pl.when(j == pl.num_programs(1) - 1)
    def _():
        o_ref[...] = x_ref[...] + 0.5 * acc_ref[...]


def _ffn(x, g, wi, wo):
    T, D = x.shape
    F = wo.shape[0]
    tm, tf = FFN_TM, FFN_TF
    nf = F // tf
    return pl.pallas_call(
        _ffn_kernel,
        grid=(T // tm, nf),
        in_specs=[
            pl.BlockSpec((tm, D), lambda i, j: (i, 0)),
            pl.BlockSpec((1, D), lambda i, j: (0, 0)),
            pl.BlockSpec((D, tf), lambda i, j: (0, j)),
            pl.BlockSpec((D, tf), lambda i, j: (0, j + nf)),
            pl.BlockSpec((tf, D), lambda i, j: (j, 0)),
        ],
        out_specs=pl.BlockSpec((tm, D), lambda i, j: (i, 0)),
        out_shape=jax.ShapeDtypeStruct((T, D), F32),
        scratch_shapes=[pltpu.VMEM((tm, D), BF16), pltpu.VMEM((tm, D), F32)],
        compiler_params=_cparams(("parallel", "arbitrary")),
        name="ffn",
    )(x, g, wi, wi, wo)


def _inproj_kernel(x_ref, g_ref, wa_ref, wb_ref, oa_ref, ob_ref):
    h = _rms(x_ref[...], g_ref[...]).astype(BF16)
    oa_ref[...] = _dot(h, wa_ref[...]).astype(oa_ref.dtype)
    ob_ref[...] = _dot(h, wb_ref[...]).astype(ob_ref.dtype)


def _inproj(x, g, wa, wb):
    T, D = x.shape
    tm = PROJ_TM
    na, nb = wa.shape[1], wb.shape[1]
    return pl.pallas_call(
        _inproj_kernel,
        grid=(T // tm,),
        in_specs=[
            pl.BlockSpec((tm, D), lambda i: (i, 0)),
            pl.BlockSpec((1, D), lambda i: (0, 0)),
            pl.BlockSpec((D, na), lambda i: (0, 0)),
            pl.BlockSpec((D, nb), lambda i: (0, 0)),
        ],
        out_specs=[pl.BlockSpec((tm, na), lambda i: (i, 0)),
                   pl.BlockSpec((tm, nb), lambda i: (i, 0))],
        out_shape=[jax.ShapeDtypeStruct((T, na), BF16), jax.ShapeDtypeStruct((T, nb), F32)],
        compiler_params=_cparams(("parallel",)),
        name="inproj",
    )(x, g, wa, wb)


def _rope_mix(y, c, s1, s2, half):
    return y * c + pltpu.roll(y, LANE - half, 1) * s1 + pltpu.roll(y, half, 1) * s2


def _norm_rope_pair(x, gain, c, s1, s2, scale):
    lane = lax.broadcasted_iota(jnp.int32, x.shape, 1)
    lo = lane < NSA_HD
    sq = x * x
    ss_lo = jnp.sum(jnp.where(lo, sq, 0.0), axis=-1, keepdims=True)
    ss_hi = jnp.sum(jnp.where(lo, 0.0, sq), axis=-1, keepdims=True)
    r = jnp.where(lo, lax.rsqrt(ss_lo * (1.0 / NSA_HD) + EPS), lax.rsqrt(ss_hi * (1.0 / NSA_HD) + EPS))
    y = x * r * gain
    y = _rope_mix(y, c, s1, s2, NSA_HD // 2)
    return y * scale if scale != 1.0 else y


def _norm_rope_mla(x, gain, c, s1, s2, scale):
    ss = jnp.sum(x * x, axis=-1, keepdims=True)
    y = x * lax.rsqrt(ss * (1.0 / MLA_QK) + EPS) * gain
    y = _rope_mix(y, c, s1, s2, MLA_ROPE // 2)
    return y * scale if scale != 1.0 else y


def _prep_kernel(z_ref, qn_ref, kvn_ref, wuq_ref, wuk_ref, wuv_ref, gq_ref, gk_ref,
                 mc_ref, ms1_ref, ms2_ref, nqg_ref, nkg_ref, nc_ref, ns1_ref, ns2_ref,
                 qm_ref, km_ref, vm_ref, qn_out, sk_out, sv_out, wk_out, wv_out, gate_out):
    zs = lambda start, width: z_ref[0, :, start:start + width]
    mc, ms1, ms2 = mc_ref[...], ms1_ref[...], ms2_ref[...]
    nc, ns1, ns2 = nc_ref[...], ns1_ref[...], ns2_ref[...]
    cq = _rms(zs(Z2_CQ, MLA_Q_LORA), qn_ref[...]).astype(BF16)
    ckv = _rms(zs(Z2_CKV, MLA_KV_LORA), kvn_ref[...]).astype(BF16)
    kr = zs(Z2_KR, LANE)
    q = _dot(cq, wuq_ref[...])
    kn = _dot(ckv, wuk_ref[...])
    v = _dot(ckv, wuv_ref[...])
    vlane = lax.broadcasted_iota(jnp.int32, (v.shape[0], LANE), 1)
    q_scale = 1.0 / math.sqrt(MLA_QK)
    for h in range(MLA_HEADS):
        sl = slice(h * LANE, (h + 1) * LANE)
        vm_ref[0, :, sl] = jnp.where(vlane < MLA_V, v[:, sl], 1.0).astype(BF16)
        qm_ref[0, :, sl] = _norm_rope_mla(q[:, sl], gq_ref[...], mc, ms1, ms2, q_scale).astype(BF16)
        km_ref[0, :, sl] = _norm_rope_mla(kn[:, sl] + kr, gk_ref[...], mc, ms1, ms2, 1.0).astype(BF16)
    nq_scale = 1.0 / math.sqrt(NSA_HD)
    for b in range(NSA_HEADS * NSA_HD // LANE):
        sl = slice(b * LANE, (b + 1) * LANE)
        qn_out[0, :, sl] = _norm_rope_pair(zs(Z2_NQ + b * LANE, LANE),
                                           nqg_ref[...], nc, ns1, ns2, nq_scale).astype(BF16)
    for b in range(2):
        sl = slice(b * LANE, (b + 1) * LANE)
        sk_out[0, :, sl] = _norm_rope_pair(zs(Z2_SK + b * LANE, LANE),
                                           nkg_ref[...], nc, ns1, ns2, 1.0).astype(BF16)
        wk_out[0, :, sl] = _norm_rope_pair(zs(Z2_WK + b * LANE, LANE),
                                           nkg_ref[...], nc, ns1, ns2, 1.0).astype(BF16)
    glane = lax.broadcasted_iota(jnp.int32, (z_ref.shape[1], 2 * LANE), 1) & (LANE - 1)
    sv_out[0] = jnp.where(glane < NSA_HD, zs(Z2_SV, 2 * LANE), 1.0).astype(BF16)
    wv_out[0] = jnp.where(glane < NSA_HD, zs(Z2_WV, 2 * LANE), 1.0).astype(BF16)
    gate_out[0] = jax.nn.sigmoid(zs(Z2_GATE, 2 * LANE))


def _prep(z2, w, tabs):
    B, S, _ = z2.shape
    ts = PREP_TS
    full = lambda shape: pl.BlockSpec(shape, lambda b, i: (0,) * len(shape))
    tab = pl.BlockSpec((ts, LANE), lambda b, i: (i, 0))
    tok = lambda width: pl.BlockSpec((1, ts, width), lambda b, i: (b, i, 0))
    out_w = (MLA_HEADS * LANE, MLA_HEADS * LANE, MLA_HEADS * LANE, NSA_HEADS * NSA_HD,
             2 * LANE, 2 * LANE, 2 * LANE, 2 * LANE, 2 * LANE)
    out_dt = (BF16,) * 8 + (F32,)
    return pl.pallas_call(
        _prep_kernel,
        grid=(B, S // ts),
        in_specs=[
            tok(Z2_W),
            full((1, MLA_Q_LORA)), full((1, MLA_KV_LORA)),
            full((MLA_Q_LORA, MLA_HEADS * LANE)), full((MLA_KV_LORA, MLA_HEADS * LANE)),
            full((MLA_KV_LORA, MLA_HEADS * LANE)),
            full((1, LANE)), full((1, LANE)),
            tab, tab, tab,
            full((1, LANE)), full((1, LANE)),
            tab, tab, tab,
        ],
        out_specs=[tok(wd) for wd in out_w],
        out_shape=[jax.ShapeDtypeStruct((B, S, wd), dt) for wd, dt in zip(out_w, out_dt)],
        compiler_params=_cparams(("parallel", "parallel")),
        name="prep",
    )(z2, w["mla_q_norm"], w["mla_kv_norm"], w["wuq"], w["wuk"], w["wuv"], w["mla_gq"], w["mla_gk"],
      tabs["mla_c"], tabs["mla_s1"], tabs["mla_s2"], w["nsa_qg"], w["nsa_kg"],
      tabs["nsa_c"], tabs["nsa_s1"], tabs["nsa_s2"])


def _compress_kernel(ck_ref, cv_ref, pek_ref, pev_ref, w1kl_ref, w1kh_ref, w2k_ref,
                     w1vl_ref, w1vh_ref, w2v_ref, kg_ref, c_ref, s1_ref, s2_ref,
                     kc_ref, vc_ref, *, nchunk):
    def branch(t_ref, pe_ref, w1l_ref, w1h_ref, w2_ref):
        lo = jnp.zeros((nchunk, 2 * CMP_HIDDEN), F32)
        hi = jnp.zeros((nchunk, 2 * CMP_HIDDEN), F32)
        for l in range(CMP_STRIDE):
            x = t_ref[0, pl.ds(l, nchunk, stride=CMP_STRIDE), :]
            lo = lo + _dot((x + pe_ref[l:l + 1, :]).astype(BF16), w1l_ref[l])
            hi = hi + _dot((x + pe_ref[CMP_STRIDE + l:CMP_STRIDE + l + 1, :]).astype(BF16), w1h_ref[l])
        pre = lo + pltpu.roll(hi, nchunk - 1, 0)
        return _dot(jax.nn.gelu(pre).astype(BF16), w2_ref[...])

    kc = branch(ck_ref, pek_ref, w1kl_ref, w1kh_ref, w2k_ref)
    vc = branch(cv_ref, pev_ref, w1vl_ref, w1vh_ref, w2v_ref)
    c, s1, s2 = c_ref[...], s1_ref[...], s2_ref[...]
    for b in range(2):
        sl = slice(b * LANE, (b + 1) * LANE)
        kc_ref[0, :, sl] = _norm_rope_pair(kc[:, sl], kg_ref[...], c, s1, s2, 1.0).astype(BF16)
    vc_ref[0] = vc.astype(BF16)


def _compress(z2, w, tabs):
    B, S, _ = z2.shape
    nchunk = S // CMP_STRIDE
    full = lambda shape: pl.BlockSpec(shape, lambda b: (0,) * len(shape))
    tab = full((nchunk, LANE))
    return pl.pallas_call(
        functools.partial(_compress_kernel, nchunk=nchunk),
        grid=(B,),
        in_specs=[
            pl.BlockSpec((1, S, LANE), lambda b: (b, 0, Z2_CK // LANE)),
            pl.BlockSpec((1, S, LANE), lambda b: (b, 0, Z2_CV // LANE)),
            full((CMP_LEN, LANE)), full((CMP_LEN, LANE)),
            full((CMP_STRIDE, LANE, 2 * CMP_HIDDEN)), full((CMP_STRIDE, LANE, 2 * CMP_HIDDEN)),
            full((2 * CMP_HIDDEN, 2 * LANE)),
            full((CMP_STRIDE, LANE, 2 * CMP_HIDDEN)), full((CMP_STRIDE, LANE, 2 * CMP_HIDDEN)),
            full((2 * CMP_HIDDEN, 2 * LANE)),
            full((1, LANE)), tab, tab, tab,
        ],
        out_specs=[pl.BlockSpec((1, nchunk, 2 * LANE), lambda b: (b, 0, 0))] * 2,
        out_shape=[jax.ShapeDtypeStruct((B, nchunk, 2 * LANE), BF16)] * 2,
        compiler_params=_cparams(("parallel",)),
        name="compress",
    )(z2, z2, w["pe_k"], w["pe_v"], w["w1k_lo"], w["w1k_hi"], w["w2k"],
      w["w1v_lo"], w["w1v_hi"], w["w2v"], w["nsa_kg"], tabs["cmp_c"], tabs["cmp_s1"], tabs["cmp_s2"])


def _head_lanes(lane, half):
    return (lane >= NSA_HD * half) & (lane < NSA_HD * (half + 1))


def _sb_kernel(q_ref, k_ref, v_ref, o_ref, acc_ref, *, t, nh):
    i = pl.program_id(2)
    w = nh * SB_HD
    lane = lax.broadcasted_iota(jnp.int32, (t, w), 1)
    row = lax.broadcasted_iota(jnp.int32, (t, t), 0)
    col = lax.broadcasted_iota(jnp.int32, (t, t), 1)
    upper = jnp.where(row > col, 1.0, 0.0).astype(BF16)
    strict = (lax.broadcasted_iota(jnp.int32, (nh * t, t), 1)
              < (lax.broadcasted_iota(jnp.int32, (nh * t, t), 0) & (t - 1)))
    q = q_ref[0]
    own = [(lane >= SB_HD * h) & (lane < SB_HD * (h + 1)) for h in range(nh)]
    qs = jnp.concatenate([jnp.where(own[h], q, jnp.zeros_like(q)) for h in range(nh)], axis=0)

    def tile(j, carry, masked):
        off = pl.multiple_of(j * t, t)
        z = _split_rows(_dot_nt, qs, k_ref[0, pl.ds(off, t), :])
        lg = jnp.log(1.0 + jnp.exp(-jnp.abs(z)))
        logsig = jnp.minimum(z, 0.0) - lg
        lom = logsig - z
        if masked:
            lom = jnp.where(strict, lom, 0.0)
        suffix = _split_rows(_dot, lom.astype(BF16), upper)
        wgt = jnp.exp(logsig + suffix + carry)
        if masked:
            wgt = jnp.where(strict, wgt, 0.0)
        pv = _split_rows(_dot, wgt.astype(BF16), v_ref[0, pl.ds(off, t), :])
        if masked:
            acc_ref[...] = pv
        else:
            acc_ref[...] += pv
        return carry + jnp.sum(lom, axis=-1, keepdims=True)

    carry = tile(i, jnp.zeros((nh * t, 1), F32), True)
    lax.fori_loop(0, i, lambda s, c: tile(i - 1 - s, c, False), carry)
    out = acc_ref[0:t, :]
    for h in range(1, nh):
        out = jnp.where(own[h], acc_ref[h * t:(h + 1) * t, :], out)
    o_ref[0] = out.astype(o_ref.dtype)


def _sb_attention(zsb):
    B, S, _ = zsb.shape
    t = SB_T
    nh = SB_HEADS_PER_STEP
    w = nh * SB_HD
    npair = SB_W // w
    return pl.pallas_call(
        functools.partial(_sb_kernel, t=t, nh=nh),
        grid=(B, npair, S // t),
        in_specs=[
            pl.BlockSpec((1, t, w), lambda b, p, i: (b, i, p)),
            pl.BlockSpec((1, S, w), lambda b, p, i: (b, 0, npair + p)),
            pl.BlockSpec((1, S, w), lambda b, p, i: (b, 0, 2 * npair + p)),
        ],
        out_specs=pl.BlockSpec((1, t, w), lambda b, p, i: (b, i, p)),
        out_shape=jax.ShapeDtypeStruct((B, S, SB_W), BF16),
        scratch_shapes=[pltpu.VMEM((nh * t, w), F32)],
        compiler_params=_cparams(("parallel", "parallel", "arbitrary")),
        name="sb_attn",
    )(zsb, zsb, zsb)


def _softmax_tile(s, vt, m, l, acc, mask):
    if mask is not None:
        s = jnp.where(mask, s, NEG_INF)
    m_new = jnp.maximum(m, jnp.max(s, axis=-1, keepdims=True))
    alpha = jnp.exp(m - m_new)
    p = jnp.exp(s - m_new)
    if mask is not None:
        p = jnp.where(mask, p, 0.0)
    l = alpha * l + jnp.sum(p, axis=-1, keepdims=True)
    acc = alpha * acc + _dot(p.astype(BF16), vt)
    return m_new, l, acc


def _flash_step(s, vt_ones, m, acc):
    m_new = jnp.maximum(m, jnp.max(s, axis=-1, keepdims=True))
    alpha = jnp.exp(m - m_new)
    p = jnp.exp(s - m_new).astype(BF16)
    return m_new, alpha * acc + _dot(p, vt_ones)


def _flash_finish(acc):
    return acc / jnp.maximum(pltpu.roll(acc, NSA_HD, 1), 1e-30)


def _lane_block_max(m, s):
    for c in range(s.shape[1] // LANE):
        m = jnp.maximum(m, s[:, c * LANE:(c + 1) * LANE])
    return m


def _mla_kernel(q_ref, k_ref, v_ref, o_ref, s_ref, *, tq, tk, nh):
    i = pl.program_id(2)
    lane = lax.broadcasted_iota(jnp.int32, (tq, LANE), 1)
    col_minus_row = (lax.broadcasted_iota(jnp.int32, (tq, tk), 1)
                     - lax.broadcasted_iota(jnp.int32, (tq, tk), 0))
    tail = (i * tq) // tk
    thr = i * tq - tail * tk
    heads = [slice(h * LANE, (h + 1) * LANE) for h in range(nh)]

    def score_tile(j, mx, masked):
        off = pl.multiple_of(j * tk, tk)
        out = []
        for h, sl in enumerate(heads):
            s = _dot_nt(q_ref[0, :, sl], k_ref[0, pl.ds(off, tk), sl])
            if masked:
                s = jnp.where(col_minus_row <= thr, s, NEG_INF)
            s_ref[h, j] = s
            out.append(_lane_block_max(mx[h], s))
        return tuple(out)

    mx = lax.fori_loop(0, tail, lambda j, mx: score_tile(j, mx, False),
                       (jnp.full((tq, LANE), NEG_INF, F32),) * nh)
    mx = score_tile(tail, mx, True)
    m = [jnp.max(mx[h], axis=-1, keepdims=True) for h in range(nh)]

    def absorb(j, acc):
        off = pl.multiple_of(j * tk, tk)
        return tuple(acc[h] + _dot(jnp.exp(s_ref[h, j] - m[h]).astype(BF16), v_ref[0, pl.ds(off, tk), sl])
                     for h, sl in enumerate(heads))

    acc = lax.fori_loop(0, tail + 1, absorb, (jnp.zeros((tq, LANE), F32),) * nh)
    for p in range(nh // 2):
        o = jnp.where(lane < MLA_V, _flash_finish(acc[2 * p]),
                      pltpu.roll(_flash_finish(acc[2 * p + 1]), MLA_V, 1))
        o_ref[0, :, p * LANE:(p + 1) * LANE] = o.astype(o_ref.dtype)


def _mla_attention(qm, km, vm):
    B, S, _ = qm.shape
    t = MLA_TQ
    nh = MLA_HEADS_PER_STEP
    ngrp = MLA_HEADS // nh
    return pl.pallas_call(
        functools.partial(_mla_kernel, tq=t, tk=MLA_TK, nh=nh),
        grid=(B, ngrp, S // t),
        in_specs=[
            pl.BlockSpec((1, t, nh * LANE), lambda b, p, i: (b, i, p)),
            pl.BlockSpec((1, S, nh * LANE), lambda b, p, i: (b, 0, p)),
            pl.BlockSpec((1, S, nh * LANE), lambda b, p, i: (b, 0, p)),
        ],
        out_specs=pl.BlockSpec((1, t, nh * MLA_V), lambda b, p, i: (b, i, p)),
        out_shape=jax.ShapeDtypeStruct((B, S, MLA_HEADS * MLA_V), BF16),
        scratch_shapes=[pltpu.VMEM((nh, S // MLA_TK, t, MLA_TK), F32)],
        compiler_params=_cparams(("parallel", "parallel", "arbitrary")),
        name="mla_attn",
    )(qm, km, vm)


def _stack_heads(q, lane):
    parts = []
    for h in range(NSA_REP):
        blk = q[:, (h // 2) * LANE:(h // 2 + 1) * LANE]
        parts.append(jnp.where(_head_lanes(lane, h % 2), blk, jnp.zeros_like(blk)))
    return parts


def _pair_lanes(lane, a, b):
    return jnp.where(lane < NSA_HD, a, b)


def _tree_sum(terms):
    while len(terms) > 1:
        terms = [a + b for a, b in zip(terms[0::2], terms[1::2])] + (terms[-1:] if len(terms) % 2 else [])
    return terms[0]


def _nsa_cmp_kernel(q_ref, kc_ref, vc_ref, ovt_ref, ocmp_ref, bias_ref, *, t, n_slc):
    i = pl.program_id(1)
    gw = NSA_REP * NSA_HD
    lane = lax.broadcasted_iota(jnp.int32, (t, LANE), 1)
    row = lax.broadcasted_iota(jnp.int32, (t, LANE), 0)
    visible = lane * CMP_STRIDE + (CMP_LEN - 1) <= i * t + row
    blk = lax.broadcasted_iota(jnp.int32, (n_slc, t), 0)
    cur = (i * t + lax.broadcasted_iota(jnp.int32, (n_slc, t), 1)) // SLC_LEN
    forced = (blk == 0) | (blk == cur) | (blk == cur - 1)
    for g in range(NSA_GROUPS):
        kc = kc_ref[0, :, g * LANE:(g + 1) * LANE]
        vc = vc_ref[0, :, g * LANE:(g + 1) * LANE]
        psum = jnp.zeros((t, LANE), F32)
        outs = []
        for qh in _stack_heads(q_ref[0, :, g * gw:(g + 1) * gw], lane):
            s = jnp.where(visible, _dot_nt(qh, kc), NEG_INF)
            m = jnp.max(s, axis=-1, keepdims=True)
            e = jnp.where(visible, jnp.exp(s - m), 0.0)
            p = e / jnp.maximum(jnp.sum(e, axis=-1, keepdims=True), 1e-30)
            psum = psum + p
            outs.append(_dot(p.astype(BF16), vc))
        ocmp_ref[0, :, g * gw:g * gw + LANE] = _pair_lanes(lane, outs[0], outs[1])
        ocmp_ref[0, :, g * gw + LANE:(g + 1) * gw] = _pair_lanes(lane, outs[2], outs[3])
        hi = psum.astype(BF16)
        lo = (psum - hi.astype(F32)).astype(BF16)
        imp = (_dot_nt(ovt_ref[...], hi) + _dot_nt(ovt_ref[...], lo))[:n_slc]
        score = jnp.where(blk > cur, -1.0, jnp.where(forced, FORCE_SCORE, imp))
        terms = []
        for c in range(n_slc):
            sc = score[c:c + 1, :]
            before = (sc > score) | ((sc == score) & (blk > c))
            terms.append(jnp.where(before, 1.0, 0.0))
        bias_t = jnp.where(_tree_sum(terms) < float(min(SLC_TOPK, n_slc)), 0.0, NEG_INF)
        bias_t = jnp.concatenate([bias_t, jnp.full((LANE - n_slc, t), NEG_INF, F32)], axis=0)
        bias_ref[0, g] = bias_t.T.astype(BF16)


def _nsa_cmp(qn, kc, vc, overlap):
    B, S, _ = qn.shape
    t = CMP_T
    nkv = NSA_GROUPS * LANE
    return pl.pallas_call(
        functools.partial(_nsa_cmp_kernel, t=t, n_slc=S // SLC_LEN),
        grid=(B, S // t),
        in_specs=[
            pl.BlockSpec((1, t, NSA_HEADS * NSA_HD), lambda b, i: (b, i, 0)),
            pl.BlockSpec((1, kc.shape[1], nkv), lambda b, i: (b, 0, 0)),
            pl.BlockSpec((1, vc.shape[1], nkv), lambda b, i: (b, 0, 0)),
            pl.BlockSpec((LANE, LANE), lambda b, i: (0, 0)),
        ],
        out_specs=[pl.BlockSpec((1, t, NSA_HEADS * NSA_HD), lambda b, i: (b, i, 0)),
                   pl.BlockSpec((1, NSA_GROUPS, t, LANE), lambda b, i: (b, 0, i, 0))],
        out_shape=[jax.ShapeDtypeStruct((B, S, NSA_HEADS * NSA_HD), F32),
                   jax.ShapeDtypeStruct((B, NSA_GROUPS, S, LANE), BF16)],
        compiler_params=_cparams(("parallel", "parallel")),
        name="nsa_cmp",
    )(qn, kc, vc, overlap)


def _nsa_attn_kernel(q_ref, sk_ref, sv_ref, wk_ref, wv_ref, bias_ref, blk_ref, ocmp_ref, gate_ref,
                     o_ref, s_ref, *, t, tk):
    i = pl.program_id(1)
    r4 = NSA_REP * t
    gw = NSA_REP * NSA_HD
    groups = range(NSA_GROUPS)
    lane = lax.broadcasted_iota(jnp.int32, (t, LANE), 1)
    gl = [slice(g * LANE, (g + 1) * LANE) for g in groups]
    q4 = [jnp.concatenate(_stack_heads(q_ref[0, :, g * gw:(g + 1) * gw], lane), axis=0) for g in groups]
    q4b = [jnp.concatenate([q4[g], jnp.concatenate([bias_ref[0, g]] * NSA_REP, axis=0)], axis=1)
           for g in groups]

    def row_minus_col(width):
        row = lax.broadcasted_iota(jnp.int32, (r4, width), 0) & (t - 1)
        return row - lax.broadcasted_iota(jnp.int32, (r4, width), 1)

    tail = (i * t) // tk
    thr = i * t - tail * tk

    def score_tile(j, mx, masked):
        off = pl.multiple_of(j * tk, tk)
        onehot = blk_ref[pl.ds(off, tk), :]
        out = []
        for g in groups:
            s = _dot_nt(q4b[g], jnp.concatenate([sk_ref[0, pl.ds(off, tk), gl[g]], onehot], axis=1))
            if masked:
                s = jnp.where(row_minus_col(tk) >= -thr, s, NEG_INF)
            s_ref[g, j] = s
            out.append(_lane_block_max(mx[g], s))
        return tuple(out)

    mx = lax.fori_loop(0, tail, lambda j, mx: score_tile(j, mx, False),
                       (jnp.full((r4, LANE), NEG_INF, F32),) * NSA_GROUPS)
    mx = score_tile(tail, mx, True)
    m_s = [jnp.max(mx[g], axis=-1, keepdims=True) for g in groups]

    def absorb(j, acc):
        off = pl.multiple_of(j * tk, tk)
        return tuple(acc[g] + _dot(jnp.exp(s_ref[g, j] - m_s[g]).astype(BF16), sv_ref[0, pl.ds(off, tk), gl[g]])
                     for g in groups)

    acc_s = lax.fori_loop(0, tail + 1, absorb, (jnp.zeros((r4, LANE), F32),) * NSA_GROUPS)

    wkeys = WIN + t
    start = pl.multiple_of(jnp.maximum(i * t - WIN, 0), t)
    dist = row_minus_col(wkeys) + (i * t - start)
    in_window = (dist >= 0) & (dist < WIN)
    for g in groups:
        s = jnp.where(in_window, _dot_nt(q4[g], wk_ref[0, pl.ds(start, wkeys), gl[g]]), NEG_INF)
        p = jnp.exp(s - jnp.max(s, axis=-1, keepdims=True)).astype(BF16)
        o_w = _flash_finish(_dot(p, wv_ref[0, pl.ds(start, wkeys), gl[g]]))
        o_s = _flash_finish(acc_s[g])
        gate = gate_ref[0, :, gl[g]]
        for p2 in range(NSA_REP // 2):
            def gate_pair(br):
                c0 = br * NSA_REP + 2 * p2
                return _pair_lanes(lane, gate[:, c0:c0 + 1], gate[:, c0 + 1:c0 + 2])

            def pair(o):
                a, b = 2 * p2, 2 * p2 + 1
                return _pair_lanes(lane, o[a * t:(a + 1) * t], pltpu.roll(o[b * t:(b + 1) * t], NSA_HD, 1))

            cols = slice(g * gw + p2 * LANE, g * gw + (p2 + 1) * LANE)
            o = gate_pair(0) * ocmp_ref[0, :, cols] + gate_pair(1) * pair(o_s) + gate_pair(2) * pair(o_w)
            o_ref[0, :, cols] = o.astype(o_ref.dtype)


def _nsa_attention(qn, sk, sv, wk, wv, bias, blk_onehot, ocmp, gates):
    B, S, _ = qn.shape
    t = ATT_T
    hw = NSA_HEADS * NSA_HD
    nkv = NSA_GROUPS * LANE
    kv = pl.BlockSpec((1, S, nkv), lambda b, i: (b, 0, 0))
    return pl.pallas_call(
        functools.partial(_nsa_attn_kernel, t=t, tk=NSA_TK),
        grid=(B, S // t),
        in_specs=[
            pl.BlockSpec((1, t, hw), lambda b, i: (b, i, 0)),
            kv, kv, kv, kv,
            pl.BlockSpec((1, NSA_GROUPS, t, LANE), lambda b, i: (b, 0, i, 0)),
            pl.BlockSpec((S, LANE), lambda b, i: (0, 0)),
            pl.BlockSpec((1, t, hw), lambda b, i: (b, i, 0)),
            pl.BlockSpec((1, t, nkv), lambda b, i: (b, i, 0)),
        ],
        out_specs=pl.BlockSpec((1, t, hw), lambda b, i: (b, i, 0)),
        out_shape=jax.ShapeDtypeStruct((B, S, hw), BF16),
        scratch_shapes=[pltpu.VMEM((NSA_GROUPS, S // NSA_TK, NSA_REP * t, NSA_TK), F32)],
        compiler_params=_cparams(("parallel", "arbitrary")),
        name="nsa_attn",
    )(qn, sk, sv, wk, wv, bias, blk_onehot, ocmp, gates)


def _merge_kernel(x_ref, g_ref, osb_ref, omla_ref, onsa_ref, wg_ref, psb_ref, pmla_ref, pnsa_ref,
                  wout_ref, o_ref):
    x = x_ref[...]
    h = _rms(x, g_ref[...]).astype(BF16)
    y = jnp.zeros(x.shape, F32)
    for br, (o_r, p_r) in enumerate(((osb_ref, psb_ref), (omla_ref, pmla_ref), (onsa_ref, pnsa_ref))):
        gate = jax.nn.sigmoid(_dot(h, wg_ref[br]))
        y = y + gate * _dot(o_r[...], p_r[...])
    o_ref[...] = x + _dot(y.astype(BF16), wout_ref[...])


def _merge(x, g, osb, omla, onsa, w):
    T, D = x.shape
    tm = MERGE_TM
    tok = lambda width: pl.BlockSpec((tm, width), lambda i: (i, 0))
    full = lambda shape: pl.BlockSpec(shape, lambda i: (0,) * len(shape))
    return pl.pallas_call(
        _merge_kernel,
        grid=(T // tm,),
        in_specs=[tok(D), full((1, D)), tok(osb.shape[1]), tok(omla.shape[1]), tok(onsa.shape[1]),
                  full((N_BRANCH, D, D)), full(w["proj_sb"].shape), full(w["proj_mla"].shape),
                  full(w["proj_nsa"].shape), full((D, D))],
        out_specs=tok(D),
        out_shape=jax.ShapeDtypeStruct((T, D), F32),
        compiler_params=_cparams(("parallel",)),
        name="merge",
    )(x, g, osb, omla, onsa, w["w_gate"], w["proj_sb"], w["proj_mla"], w["proj_nsa"], w["w_out"])


def _rope_tables(pos, d, seg_starts):
    half = d // 2
    inv = jnp.exp(-math.log(ROPE_THETA) * jnp.arange(half, dtype=F32) * (2.0 / d))
    ang = pos.astype(F32)[:, None] * inv[None, :]
    cos, sin = jnp.cos(ang), jnp.sin(ang)
    n = pos.shape[0]
    zeros = jnp.zeros((n, half), F32)
    covered = sorted(seg_starts)
    c_parts, s1_parts, s2_parts = [], [], []
    lane = 0
    for st in covered:
        if st > lane:
            fill = jnp.ones((n, st - lane), F32)
            c_parts.append(fill)
            s1_parts.append(0.0 * fill)
            s2_parts.append(0.0 * fill)
        c_parts += [cos, cos]
        s1_parts += [-sin, zeros]
        s2_parts += [zeros, sin]
        lane = st + d
    if lane < LANE:
        fill = jnp.zeros((n, LANE - lane), F32)
        c_parts.append(fill)
        s1_parts.append(fill)
        s2_parts.append(fill)
    cat = lambda parts: jnp.concatenate(parts, axis=1)
    return cat(c_parts), cat(s1_parts), cat(s2_parts)


def _tables(S):
    pos = jnp.arange(S, dtype=jnp.int32)
    nchunk = S // CMP_STRIDE
    ends = jnp.arange(nchunk, dtype=jnp.int32) * CMP_STRIDE + (CMP_LEN - 1)
    t = {}
    t["nsa_c"], t["nsa_s1"], t["nsa_s2"] = _rope_tables(pos, NSA_HD, (0, NSA_HD))
    t["cmp_c"], t["cmp_s1"], t["cmp_s2"] = _rope_tables(ends, NSA_HD, (0, NSA_HD))
    t["mla_c"], t["mla_s1"], t["mla_s2"] = _rope_tables(pos, MLA_ROPE, (MLA_NOPE,))
    t["blk_onehot"] = (pos[:, None] // SLC_LEN == jnp.arange(LANE, dtype=jnp.int32)[None, :]).astype(BF16)
    n_cmp = (S - CMP_LEN) // CMP_STRIDE + 1
    n_slc = S // SLC_LEN
    c0 = np.arange(n_cmp)[:, None] * CMP_STRIDE
    s0 = np.arange(n_slc)[None, :] * SLC_LEN
    ov = np.clip(np.minimum(c0 + CMP_LEN, s0 + SLC_LEN) - np.maximum(c0, s0), 0, None) / CMP_LEN
    full = np.zeros((LANE, LANE), np.float32)
    full[:n_slc, :n_cmp] = ov.T
    t["overlap"] = jnp.asarray(full, BF16)
    return t


def _pad_cols(w, width):
    return jnp.pad(w, ((0, 0), (0, width - w.shape[1])))


def _dup_groups(w):
    parts = []
    for g in range(NSA_GROUPS):
        blk = w[:, g * NSA_HD:(g + 1) * NSA_HD]
        parts += [blk, blk]
    return jnp.concatenate(parts, axis=1)


def _blockdiag2(a):
    z = jnp.zeros_like(a)
    return jnp.concatenate([jnp.concatenate([a, z], axis=1), jnp.concatenate([z, a], axis=1)], axis=0)


def _layer_weights(l, p):
    w = {}
    row = lambda v: v[l][None, :].astype(F32)
    offs = np.concatenate([[0], np.cumsum(IN_SIZES)])
    cols = [p["w_in"][l][:, offs[k]:offs[k + 1]] for k in range(len(IN_SIZES))]
    (sb_q, sb_k, sb_v, cq, ckv, kr, nq, ck, cv, sk, sv, wk, wv, ngate, mgate) = cols
    sb_scale = 1.0 / math.sqrt(SB_HD)
    w["w_sb"] = jnp.concatenate([sb_q * sb_scale, sb_k, sb_v], axis=1).astype(BF16)
    kr_blk = jnp.pad(kr, ((0, 0), (MLA_NOPE, LANE - MLA_NOPE - MLA_ROPE)))
    ng = ngate.reshape(D_MODEL, N_BRANCH, NSA_GROUPS, NSA_REP)
    ng = jnp.concatenate([_pad_cols(ng[:, :, g, :].reshape(D_MODEL, N_BRANCH * NSA_REP), LANE)
                          for g in range(NSA_GROUPS)], axis=1)
    w["w_z2"] = jnp.concatenate([cq, ckv, kr_blk, nq, ck, cv, _dup_groups(sk), _dup_groups(sv),
                                 _dup_groups(wk), _dup_groups(wv), ng], axis=1).astype(BF16)
    w["w_gate"] = mgate.reshape(D_MODEL, N_BRANCH, D_MODEL).transpose(1, 0, 2).astype(BF16)
    w["mix_norm"] = row(p["mix_norm"])
    w["mla_q_norm"] = row(p["mla_q_norm"])
    w["mla_kv_norm"] = row(p["mla_kv_norm"])
    wuq = p["mla_w_uq"][l].reshape(MLA_Q_LORA, MLA_HEADS, MLA_QK)
    w["wuq"] = jnp.pad(wuq, ((0, 0), (0, 0), (0, LANE - MLA_QK))).reshape(MLA_Q_LORA, -1).astype(BF16)
    wukv = p["mla_w_ukv"][l].reshape(MLA_KV_LORA, MLA_HEADS, MLA_NOPE + MLA_V)
    w["wuk"] = jnp.pad(wukv[:, :, :MLA_NOPE], ((0, 0), (0, 0), (0, LANE - MLA_NOPE))).reshape(MLA_KV_LORA, -1).astype(BF16)
    w["wuv"] = jnp.pad(wukv[:, :, MLA_NOPE:], ((0, 0), (0, 0), (0, LANE - MLA_V))).reshape(MLA_KV_LORA, -1).astype(BF16)
    w["mla_gq"] = _pad_cols(row(p["mla_qk_gain_q"]), LANE)
    w["mla_gk"] = _pad_cols(row(p["mla_qk_gain_k"]), LANE)
    w["nsa_qg"] = jnp.tile(row(p["nsa_q_gain"]), (1, 2))
    w["nsa_kg"] = jnp.tile(row(p["nsa_k_gain"]), (1, 2))
    w["pe_k"] = jnp.tile(p["cmp_pos_k"][l], (1, 2)).astype(F32)
    w["pe_v"] = jnp.tile(p["cmp_pos_v"][l], (1, 2)).astype(F32)
    for nm, w1, w2 in (("k", p["cmp_wk1"][l], p["cmp_wk2"][l]), ("v", p["cmp_wv1"][l], p["cmp_wv2"][l])):
        w1 = w1.reshape(CMP_LEN, NSA_HD, CMP_HIDDEN)
        bd = jax.vmap(_blockdiag2)(w1).astype(BF16)
        w["w1%s_lo" % nm] = bd[:CMP_STRIDE]
        w["w1%s_hi" % nm] = bd[CMP_STRIDE:]
        z = jnp.zeros_like(w2)
        w["w2" + nm] = jnp.concatenate([jnp.concatenate([w2, w2, z, z], axis=1),
                                        jnp.concatenate([z, z, w2, w2], axis=1)], axis=0).astype(BF16)
    for nm in ("proj_sb", "proj_mla", "proj_nsa", "w_out"):
        w[nm] = p[nm][l].astype(BF16)
    for nm in ("ffn1", "ffn2"):
        w[nm + "_norm"] = row(p[nm + "_norm"])
        w[nm + "_wi"] = p[nm + "_wi"][l].astype(BF16)
        w[nm + "_wo"] = p[nm + "_wo"][l].astype(BF16)
    return w


def kernel(x, ffn1_norm, ffn1_wi, ffn1_wo, mix_norm, w_in, mla_q_norm, mla_w_uq, mla_kv_norm, mla_w_ukv, mla_qk_gain_q, mla_qk_gain_k, nsa_q_gain, nsa_k_gain, cmp_pos_k, cmp_pos_v, cmp_wk1, cmp_wk2, cmp_wv1, cmp_wv2, proj_sb, proj_mla, proj_nsa, w_out, ffn2_norm, ffn2_wi, ffn2_wo):
    p = dict(ffn1_norm=ffn1_norm, ffn1_wi=ffn1_wi, ffn1_wo=ffn1_wo, mix_norm=mix_norm, w_in=w_in,
             mla_q_norm=mla_q_norm, mla_w_uq=mla_w_uq, mla_kv_norm=mla_kv_norm, mla_w_ukv=mla_w_ukv,
             mla_qk_gain_q=mla_qk_gain_q, mla_qk_gain_k=mla_qk_gain_k, nsa_q_gain=nsa_q_gain,
             nsa_k_gain=nsa_k_gain, cmp_pos_k=cmp_pos_k, cmp_pos_v=cmp_pos_v, cmp_wk1=cmp_wk1,
             cmp_wk2=cmp_wk2, cmp_wv1=cmp_wv1, cmp_wv2=cmp_wv2, proj_sb=proj_sb, proj_mla=proj_mla,
             proj_nsa=proj_nsa, w_out=w_out, ffn2_norm=ffn2_norm, ffn2_wi=ffn2_wi, ffn2_wo=ffn2_wo)
    B, S, D = x.shape
    assert S // CMP_STRIDE == LANE and S // SLC_LEN <= LANE and D == D_MODEL
    tabs = _tables(S)
    xt = x.reshape(B * S, D)
    for l in range(ffn1_wi.shape[0]):
        w = _layer_weights(l, p)
        xt = _ffn(xt, w["ffn1_norm"], w["ffn1_wi"], w["ffn1_wo"])
        zsb, z2 = _inproj(xt, w["mix_norm"], w["w_sb"], w["w_z2"])
        zsb = zsb.reshape(B, S, -1)
        z2 = z2.reshape(B, S, -1)
        qm, km, vm, qn, sk, sv, wk, wv, gates = _prep(z2, w, tabs)
        kc, vc = _compress(z2, w, tabs)
        o_sb = _sb_attention(zsb)
        o_mla = _mla_attention(qm, km, vm)
        o_cmp, bias = _nsa_cmp(qn, kc, vc, tabs["overlap"])
        o_nsa = _nsa_attention(qn, sk, sv, wk, wv, bias, tabs["blk_onehot"], o_cmp, gates)
        xt = _merge(xt, w["mix_norm"], o_sb.reshape(B * S, -1), o_mla.reshape(B * S, -1),
                    o_nsa.reshape(B * S, -1), w)
        xt = _ffn(xt, w["ffn2_norm"], w["ffn2_wi"], w["ffn2_wo"])
    return xt.reshape(B, S, D)
```

```python
import functools
import math

import numpy as np
import jax
import jax.numpy as jnp
from jax import lax
from jax.experimental import pallas as pl
from jax.experimental.pallas import tpu as pltpu

D_MODEL = 1024
D_FF = 2816
EPS = 1e-6
ROPE_THETA = 10000.0
N_BRANCH = 3
SB_HEADS = 8
SB_HD = 64
MLA_HEADS = 8
MLA_NOPE = 64
MLA_ROPE = 32
MLA_V = 64
MLA_Q_LORA = 256
MLA_KV_LORA = 128
MLA_QK = MLA_NOPE + MLA_ROPE
NSA_HEADS = 8
NSA_GROUPS = 2
NSA_REP = NSA_HEADS // NSA_GROUPS
NSA_HD = 64
CMP_LEN = 32
CMP_STRIDE = 16
CMP_HIDDEN = 128
SLC_LEN = 64
SLC_TOPK = 16
WIN = 512
FORCE_SCORE = 1e3
NEG_INF = -1e30
SB_W = SB_HEADS * SB_HD
NSA_KV_W = NSA_GROUPS * NSA_HD
IN_SIZES = (SB_W, SB_W, SB_W, MLA_Q_LORA, MLA_KV_LORA, MLA_ROPE, NSA_HEADS * NSA_HD,
            NSA_KV_W, NSA_KV_W, NSA_KV_W, NSA_KV_W, NSA_KV_W, NSA_KV_W,
            N_BRANCH * NSA_HEADS, N_BRANCH * D_MODEL)

LANE = 128
V7X_VMEM_LIMIT = 56 * 1024 * 1024

FFN_TM = 1024
FFN_TF = 256
PROJ_TM = 512
PREP_TS = 512
ATT_T = 128
NSA_TK = 512
CMP_T = 256
SB_T = 256
SB_HEADS_PER_STEP = 4
MLA_TQ = 256
MLA_HEADS_PER_STEP = 4
MLA_TK = 512
MERGE_TM = 512

Z2_CQ = 0
Z2_CKV = 256
Z2_KR = 384
Z2_NQ = 512
Z2_CK = 1024
Z2_CV = 1152
Z2_SK = 1280
Z2_SV = 1536
Z2_WK = 1792
Z2_WV = 2048
Z2_GATE = 2304
Z2_W = 2560

BF16 = jnp.bfloat16
F32 = jnp.float32


def _cparams(sem, vmem=V7X_VMEM_LIMIT):
    return pltpu.CompilerParams(dimension_semantics=sem, vmem_limit_bytes=vmem)


def _dot(a, b):
    return jnp.dot(a, b, preferred_element_type=F32)


def _dot_nt(a, b):
    return lax.dot_general(a, b, (((1,), (1,)), ((), ())), preferred_element_type=F32)


def _split_rows(dot, a, b):
    h = a.shape[0] // 2
    return jnp.concatenate([dot(a[:h], b), dot(a[h:], b)], axis=0)


def _rms(x, g):
    y = x * lax.rsqrt(jnp.mean(x * x, axis=-1, keepdims=True) + EPS)
    return y * g


def _ffn_kernel(x_ref, g_ref, wa_ref, wb_ref, wo_ref, o_ref, h_ref, acc_ref):
    j = pl.program_id(1)

    @pl.when(j == 0)
    def _():
        h_ref[...] = _rms(x_ref[...], g_ref[...]).astype(BF16)
        acc_ref[...] = jnp.zeros_like(acc_ref)

    h = h_ref[...]
    a = _dot(h, wa_ref[...])
    b = _dot(h, wb_ref[...])
    u = (jax.nn.silu(a) * b).astype(BF16)
    acc_ref[...] += _dot(u, wo_ref[...])

    @pl.when(j == pl.num_programs(1) - 1)
    def _():
        o_ref[...] = x_ref[...] + 0.5 * acc_ref[...]


def _ffn(x, g, wi, wo):
    T, D = x.shape
    F = wo.shape[0]
    tm, tf = FFN_TM, FFN_TF
    nf = F // tf
    return pl.pallas_call(
        _ffn_kernel,
        grid=(T // tm, nf),
        in_specs=[
            pl.BlockSpec((tm, D), lambda i, j: (i, 0)),
            pl.BlockSpec((1, D), lambda i, j: (0, 0)),
            pl.BlockSpec((D, tf), lambda i, j: (0, j)),
            pl.BlockSpec((D, tf), lambda i, j: (0, j + nf)),
            pl.BlockSpec((tf, D), lambda i, j: (j, 0)),
        ],
        out_specs=pl.BlockSpec((tm, D), lambda i, j: (i, 0)),
        out_shape=jax.ShapeDtypeStruct((T, D), F32),
        scratch_shapes=[pltpu.VMEM((tm, D), BF16), pltpu.VMEM((tm, D), F32)],
        compiler_params=_cparams(("parallel", "arbitrary")),
        name="ffn",
    )(x, g, wi, wi, wo)


def _rope_mix(y, c, s):
    return y * c + pltpu.roll(y, LANE // 2, 1) * s


def _first_head(lane):
    return (lane & (NSA_HD // 2)) == 0


def _norm_rope_pair(x, gain, c, s, scale):
    first = _first_head(lax.broadcasted_iota(jnp.int32, x.shape, 1))
    sq = x * x
    ss_a = jnp.sum(jnp.where(first, sq, 0.0), axis=-1, keepdims=True)
    ss_b = jnp.sum(jnp.where(first, 0.0, sq), axis=-1, keepdims=True)
    r = jnp.where(first, lax.rsqrt(ss_a * (1.0 / NSA_HD) + EPS), lax.rsqrt(ss_b * (1.0 / NSA_HD) + EPS))
    y = _rope_mix(x * r * gain, c, s)
    return y * scale if scale != 1.0 else y


def _norm_rope_mla(x, gain, c, s, scale):
    ss = jnp.sum(x * x, axis=-1, keepdims=True)
    y = _rope_mix(x * lax.rsqrt(ss * (1.0 / MLA_QK) + EPS) * gain, c, s)
    return y * scale if scale != 1.0 else y


def _inproj_kernel(x_ref, g_ref, wa_ref, wb_ref, oa_ref, ob_ref):
    h = _rms(x_ref[...], g_ref[...]).astype(BF16)
    oa_ref[...] = _dot(h, wa_ref[...]).astype(oa_ref.dtype)
    ob_ref[...] = _dot(h, wb_ref[...]).astype(ob_ref.dtype)


def _inproj(x, g, wa, wb):
    T, D = x.shape
    tm = PROJ_TM
    na, nb = wa.shape[1], wb.shape[1]
    return pl.pallas_call(
        _inproj_kernel,
        grid=(T // tm,),
        in_specs=[
            pl.BlockSpec((tm, D), lambda i: (i, 0)),
            pl.BlockSpec((1, D), lambda i: (0, 0)),
            pl.BlockSpec((D, na), lambda i: (0, 0)),
            pl.BlockSpec((D, nb), lambda i: (0, 0)),
        ],
        out_specs=[pl.BlockSpec((tm, na), lambda i: (i, 0)),
                   pl.BlockSpec((tm, nb), lambda i: (i, 0))],
        out_shape=[jax.ShapeDtypeStruct((T, na), BF16), jax.ShapeDtypeStruct((T, nb), F32)],
        compiler_params=_cparams(("parallel",)),
        name="inproj",
    )(x, g, wa, wb)


def _prep_kernel(z_ref, qn_ref, kvn_ref, wuq_ref, wuk_ref, wuv_ref, gq_ref, gk_ref,
                 mc_ref, ms_ref, nqg_ref, nkg_ref, nc_ref, ns_ref,
                 qm_ref, km_ref, vm_ref, qn_out, sk_out, sv_out, wk_out, wv_out, gate_out):
    zs = lambda start, width: z_ref[0, :, start:start + width]
    mc, ms = mc_ref[...], ms_ref[...]
    nc, ns = nc_ref[...], ns_ref[...]
    cq = _rms(zs(Z2_CQ, MLA_Q_LORA), qn_ref[...]).astype(BF16)
    ckv = _rms(zs(Z2_CKV, MLA_KV_LORA), kvn_ref[...]).astype(BF16)
    kr = zs(Z2_KR, LANE)
    q = _dot(cq, wuq_ref[...])
    kn = _dot(ckv, wuk_ref[...])
    v = _dot(ckv, wuv_ref[...])
    vlane = lax.broadcasted_iota(jnp.int32, (v.shape[0], LANE), 1)
    q_scale = 1.0 / math.sqrt(MLA_QK)
    for h in range(MLA_HEADS):
        sl = slice(h * LANE, (h + 1) * LANE)
        vm_ref[0, :, sl] = jnp.where(vlane < MLA_V, v[:, sl], 1.0).astype(BF16)
        qm_ref[0, :, sl] = _norm_rope_mla(q[:, sl], gq_ref[...], mc, ms, q_scale).astype(BF16)
        km_ref[0, :, sl] = _norm_rope_mla(kn[:, sl] + kr, gk_ref[...], mc, ms, 1.0).astype(BF16)
    nq_scale = 1.0 / math.sqrt(NSA_HD)
    for b in range(NSA_HEADS * NSA_HD // LANE):
        sl = slice(b * LANE, (b + 1) * LANE)
        qn_out[0, :, sl] = _norm_rope_pair(zs(Z2_NQ + b * LANE, LANE), nqg_ref[...], nc, ns,
                                           nq_scale).astype(BF16)
    for b in range(2):
        sl = slice(b * LANE, (b + 1) * LANE)
        sk_out[0, :, sl] = _norm_rope_pair(zs(Z2_SK + b * LANE, LANE), nkg_ref[...], nc, ns, 1.0).astype(BF16)
        wk_out[0, :, sl] = _norm_rope_pair(zs(Z2_WK + b * LANE, LANE), nkg_ref[...], nc, ns, 1.0).astype(BF16)
    glane = lax.broadcasted_iota(jnp.int32, (z_ref.shape[1], 2 * LANE), 1) & (LANE - 1)
    sv_out[0] = jnp.where(glane < NSA_HD, zs(Z2_SV, 2 * LANE), 1.0).astype(BF16)
    wv_out[0] = jnp.where(glane < NSA_HD, zs(Z2_WV, 2 * LANE), 1.0).astype(BF16)
    gate_out[0] = jax.nn.sigmoid(zs(Z2_GATE, 2 * LANE))


def _prep(z2, w, tabs):
    B, S, _ = z2.shape
    ts = PREP_TS
    full = lambda shape: pl.BlockSpec(shape, lambda b, i: (0,) * len(shape))
    tab = pl.BlockSpec((ts, LANE), lambda b, i: (i, 0))
    tok = lambda width: pl.BlockSpec((1, ts, width), lambda b, i: (b, i, 0))
    out_w = (MLA_HEADS * LANE, MLA_HEADS * LANE, MLA_HEADS * LANE, NSA_HEADS * NSA_HD,
             2 * LANE, 2 * LANE, 2 * LANE, 2 * LANE, 2 * LANE)
    out_dt = (BF16,) * 8 + (F32,)
    return pl.pallas_call(
        _prep_kernel,
        grid=(B, S // ts),
        in_specs=[
            tok(Z2_W),
            full((1, MLA_Q_LORA)), full((1, MLA_KV_LORA)),
            full((MLA_Q_LORA, MLA_HEADS * LANE)), full((MLA_KV_LORA, MLA_HEADS * LANE)),
            full((MLA_KV_LORA, MLA_HEADS * LANE)),
            full((1, LANE)), full((1, LANE)),
            tab, tab,
            full((1, LANE)), full((1, LANE)),
            tab, tab,
        ],
        out_specs=[tok(wd) for wd in out_w],
        out_shape=[jax.ShapeDtypeStruct((B, S, wd), dt) for wd, dt in zip(out_w, out_dt)],
        compiler_params=_cparams(("parallel", "parallel")),
        name="prep",
    )(z2, w["mla_q_norm"], w["mla_kv_norm"], w["wuq"], w["wuk"], w["wuv"], w["mla_gq"], w["mla_gk"],
      tabs["mla_c"], tabs["mla_s"], w["nsa_qg"], w["nsa_kg"], tabs["nsa_c"], tabs["nsa_s"])


def _compress_kernel(ck_ref, cv_ref, pek_ref, pev_ref, w1kl_ref, w1kh_ref, w2k_ref,
                     w1vl_ref, w1vh_ref, w2v_ref, kg_ref, c_ref, s_ref,
                     kc_ref, vc_ref, *, nchunk):
    def branch(t_ref, pe_ref, w1l_ref, w1h_ref, w2_ref):
        lo = jnp.zeros((nchunk, 2 * CMP_HIDDEN), F32)
        hi = jnp.zeros((nchunk, 2 * CMP_HIDDEN), F32)
        for l in range(CMP_STRIDE):
            x = t_ref[0, pl.ds(l, nchunk, stride=CMP_STRIDE), :]
            lo = lo + _dot((x + pe_ref[l:l + 1, :]).astype(BF16), w1l_ref[l])
            hi = hi + _dot((x + pe_ref[CMP_STRIDE + l:CMP_STRIDE + l + 1, :]).astype(BF16), w1h_ref[l])
        pre = lo + pltpu.roll(hi, nchunk - 1, 0)
        return _dot(jax.nn.gelu(pre).astype(BF16), w2_ref[...])

    kc = branch(ck_ref, pek_ref, w1kl_ref, w1kh_ref, w2k_ref)
    vc = branch(cv_ref, pev_ref, w1vl_ref, w1vh_ref, w2v_ref)
    for b in range(2):
        sl = slice(b * LANE, (b + 1) * LANE)
        kc_ref[0, :, sl] = _norm_rope_pair(kc[:, sl], kg_ref[...], c_ref[...], s_ref[...], 1.0).astype(BF16)
    vc_ref[0] = vc.astype(BF16)


def _compress(z2, w, tabs):
    B, S, _ = z2.shape
    nchunk = S // CMP_STRIDE
    full = lambda shape: pl.BlockSpec(shape, lambda b: (0,) * len(shape))
    tab = full((nchunk, LANE))
    return pl.pallas_call(
        functools.partial(_compress_kernel, nchunk=nchunk),
        grid=(B,),
        in_specs=[
            pl.BlockSpec((1, S, LANE), lambda b: (b, 0, Z2_CK // LANE)),
            pl.BlockSpec((1, S, LANE), lambda b: (b, 0, Z2_CV // LANE)),
            full((CMP_LEN, LANE)), full((CMP_LEN, LANE)),
            full((CMP_STRIDE, LANE, 2 * CMP_HIDDEN)), full((CMP_STRIDE, LANE, 2 * CMP_HIDDEN)),
            full((2 * CMP_HIDDEN, 2 * LANE)),
            full((CMP_STRIDE, LANE, 2 * CMP_HIDDEN)), full((CMP_STRIDE, LANE, 2 * CMP_HIDDEN)),
            full((2 * CMP_HIDDEN, 2 * LANE)),
            full((1, LANE)), tab, tab,
        ],
        out_specs=[pl.BlockSpec((1, nchunk, 2 * LANE), lambda b: (b, 0, 0))] * 2,
        out_shape=[jax.ShapeDtypeStruct((B, nchunk, 2 * LANE), BF16)] * 2,
        compiler_params=_cparams(("parallel",)),
        name="compress",
    )(z2, z2, w["pe_k"], w["pe_v"], w["w1k_lo"], w["w1k_hi"], w["w2k"],
      w["w1v_lo"], w["w1v_hi"], w["w2v"], w["nsa_kg"], tabs["cmp_c"], tabs["cmp_s"])


def _sb_kernel(q_ref, k_ref, v_ref, o_ref, acc_ref, *, t, nh):
    i = pl.program_id(2)
    w = nh * SB_HD
    lane = lax.broadcasted_iota(jnp.int32, (t, w), 1)
    row = lax.broadcasted_iota(jnp.int32, (t, t), 0)
    col = lax.broadcasted_iota(jnp.int32, (t, t), 1)
    upper = jnp.where(row > col, 1.0, 0.0).astype(BF16)
    strict = (lax.broadcasted_iota(jnp.int32, (nh * t, t), 1)
              < (lax.broadcasted_iota(jnp.int32, (nh * t, t), 0) & (t - 1)))
    q = q_ref[0]
    own = [(lane >= SB_HD * h) & (lane < SB_HD * (h + 1)) for h in range(nh)]
    qs = jnp.concatenate([jnp.where(own[h], q, jnp.zeros_like(q)) for h in range(nh)], axis=0)

    def tile(j, carry, masked):
        off = pl.multiple_of(j * t, t)
        z = _split_rows(_dot_nt, qs, k_ref[0, pl.ds(off, t), :])
        lg = jnp.log(1.0 + jnp.exp(-jnp.abs(z)))
        logsig = jnp.minimum(z, 0.0) - lg
        lom = logsig - z
        if masked:
            lom = jnp.where(strict, lom, 0.0)
        suffix = _split_rows(_dot, lom.astype(BF16), upper)
        wgt = jnp.exp(logsig + suffix + carry)
        if masked:
            wgt = jnp.where(strict, wgt, 0.0)
        pv = _split_rows(_dot, wgt.astype(BF16), v_ref[0, pl.ds(off, t), :])
        if masked:
            acc_ref[...] = pv
        else:
            acc_ref[...] += pv
        return carry + jnp.sum(lom, axis=-1, keepdims=True)

    carry = tile(i, jnp.zeros((nh * t, 1), F32), True)
    lax.fori_loop(0, i, lambda s, c: tile(i - 1 - s, c, False), carry)
    out = acc_ref[0:t, :]
    for h in range(1, nh):
        out = jnp.where(own[h], acc_ref[h * t:(h + 1) * t, :], out)
    o_ref[0] = out.astype(o_ref.dtype)


def _sb_attention(zsb):
    B, S, _ = zsb.shape
    t = SB_T
    nh = SB_HEADS_PER_STEP
    w = nh * SB_HD
    npair = SB_W // w
    return pl.pallas_call(
        functools.partial(_sb_kernel, t=t, nh=nh),
        grid=(B, npair, S // t),
        in_specs=[
            pl.BlockSpec((1, t, w), lambda b, p, i: (b, i, p)),
            pl.BlockSpec((1, S, w), lambda b, p, i: (b, 0, npair + p)),
            pl.BlockSpec((1, S, w), lambda b, p, i: (b, 0, 2 * npair + p)),
        ],
        out_specs=pl.BlockSpec((1, t, w), lambda b, p, i: (b, i, p)),
        out_shape=jax.ShapeDtypeStruct((B, S, SB_W), BF16),
        scratch_shapes=[pltpu.VMEM((nh * t, w), F32)],
        compiler_params=_cparams(("parallel", "parallel", "arbitrary")),
        name="sb_attn",
    )(zsb, zsb, zsb)


def _softmax_tile(s, vt, m, l, acc, mask):
    if mask is not None:
        s = jnp.where(mask, s, NEG_INF)
    m_new = jnp.maximum(m, jnp.max(s, axis=-1, keepdims=True))
    alpha = jnp.exp(m - m_new)
    p = jnp.exp(s - m_new)
    if mask is not None:
        p = jnp.where(mask, p, 0.0)
    l = alpha * l + jnp.sum(p, axis=-1, keepdims=True)
    acc = alpha * acc + _dot(p.astype(BF16), vt)
    return m_new, l, acc


def _flash_step(s, vt_ones, m, acc):
    m_new = jnp.maximum(m, jnp.max(s, axis=-1, keepdims=True))
    alpha = jnp.exp(m - m_new)
    p = jnp.exp(s - m_new).astype(BF16)
    return m_new, alpha * acc + _dot(p, vt_ones)


def _flash_finish(acc):
    return acc / jnp.maximum(pltpu.roll(acc, NSA_HD, 1), 1e-30)


def _lane_block_max(m, s):
    for c in range(s.shape[1] // LANE):
        m = jnp.maximum(m, s[:, c * LANE:(c + 1) * LANE])
    return m


def _mla_kernel(q_ref, k_ref, v_ref, o_ref, s_ref, *, tq, tk, nh):
    i = pl.program_id(2)
    lane = lax.broadcasted_iota(jnp.int32, (tq, LANE), 1)
    col_minus_row = (lax.broadcasted_iota(jnp.int32, (tq, tk), 1)
                     - lax.broadcasted_iota(jnp.int32, (tq, tk), 0))
    tail = (i * tq) // tk
    thr = i * tq - tail * tk
    heads = [slice(h * LANE, (h + 1) * LANE) for h in range(nh)]

    def score_tile(j, mx, masked):
        off = pl.multiple_of(j * tk, tk)
        out = []
        for h, sl in enumerate(heads):
            s = _dot_nt(q_ref[0, :, sl], k_ref[0, pl.ds(off, tk), sl])
            if masked:
                s = jnp.where(col_minus_row <= thr, s, NEG_INF)
            s_ref[h, j] = s
            out.append(_lane_block_max(mx[h], s))
        return tuple(out)

    mx = lax.fori_loop(0, tail, lambda j, mx: score_tile(j, mx, False),
                       (jnp.full((tq, LANE), NEG_INF, F32),) * nh)
    mx = score_tile(tail, mx, True)
    m = [jnp.max(mx[h], axis=-1, keepdims=True) for h in range(nh)]

    def absorb(j, acc):
        off = pl.multiple_of(j * tk, tk)
        return tuple(acc[h] + _dot(jnp.exp(s_ref[h, j] - m[h]).astype(BF16), v_ref[0, pl.ds(off, tk), sl])
                     for h, sl in enumerate(heads))

    acc = lax.fori_loop(0, tail + 1, absorb, (jnp.zeros((tq, LANE), F32),) * nh)
    for p in range(nh // 2):
        o = jnp.where(lane < MLA_V, _flash_finish(acc[2 * p]),
                      pltpu.roll(_flash_finish(acc[2 * p + 1]), MLA_V, 1))
        o_ref[0, :, p * LANE:(p + 1) * LANE] = o.astype(o_ref.dtype)


def _mla_attention(qm, km, vm):
    B, S, _ = qm.shape
    t = MLA_TQ
    nh = MLA_HEADS_PER_STEP
    ngrp = MLA_HEADS // nh
    return pl.pallas_call(
        functools.partial(_mla_kernel, tq=t, tk=MLA_TK, nh=nh),
        grid=(B, ngrp, S // t),
        in_specs=[
            pl.BlockSpec((1, t, nh * LANE), lambda b, p, i: (b, i, p)),
            pl.BlockSpec((1, S, nh * LANE), lambda b, p, i: (b, 0, p)),
            pl.BlockSpec((1, S, nh * LANE), lambda b, p, i: (b, 0, p)),
        ],
        out_specs=pl.BlockSpec((1, t, nh * MLA_V), lambda b, p, i: (b, i, p)),
        out_shape=jax.ShapeDtypeStruct((B, S, MLA_HEADS * MLA_V), BF16),
        scratch_shapes=[pltpu.VMEM((nh, S // MLA_TK, t, MLA_TK), F32)],
        compiler_params=_cparams(("parallel", "parallel", "arbitrary")),
        name="mla_attn",
    )(qm, km, vm)


def _stack_heads(q, lane):
    first = _first_head(lane)
    parts = []
    for h in range(NSA_REP):
        blk = q[:, (h // 2) * LANE:(h // 2 + 1) * LANE]
        parts.append(jnp.where(first if h % 2 == 0 else ~first, blk, jnp.zeros_like(blk)))
    return parts


def _pair_lanes(lane, a, b):
    return jnp.where(lane < NSA_HD, a, b)


def _tree_sum(terms):
    while len(terms) > 1:
        terms = [a + b for a, b in zip(terms[0::2], terms[1::2])] + (terms[-1:] if len(terms) % 2 else [])
    return terms[0]


def _nsa_cmp_kernel(q_ref, kc_ref, vc_ref, ovt_ref, ocmp_ref, bias_ref, *, t, n_slc):
    i = pl.program_id(1)
    gw = NSA_REP * NSA_HD
    lane = lax.broadcasted_iota(jnp.int32, (t, LANE), 1)
    row = lax.broadcasted_iota(jnp.int32, (t, LANE), 0)
    visible = lane * CMP_STRIDE + (CMP_LEN - 1) <= i * t + row
    blk = lax.broadcasted_iota(jnp.int32, (n_slc, t), 0)
    cur = (i * t + lax.broadcasted_iota(jnp.int32, (n_slc, t), 1)) // SLC_LEN
    forced = (blk == 0) | (blk == cur) | (blk == cur - 1)
    for g in range(NSA_GROUPS):
        kc = kc_ref[0, :, g * LANE:(g + 1) * LANE]
        vc = vc_ref[0, :, g * LANE:(g + 1) * LANE]
        psum = jnp.zeros((t, LANE), F32)
        outs = []
        for qh in _stack_heads(q_ref[0, :, g * gw:(g + 1) * gw], lane):
            s = jnp.where(visible, _dot_nt(qh, kc), NEG_INF)
            m = jnp.max(s, axis=-1, keepdims=True)
            e = jnp.where(visible, jnp.exp(s - m), 0.0)
            p = e / jnp.maximum(jnp.sum(e, axis=-1, keepdims=True), 1e-30)
            psum = psum + p
            outs.append(_dot(p.astype(BF16), vc))
        ocmp_ref[0, :, g * gw:g * gw + LANE] = _pair_lanes(lane, outs[0], outs[1])
        ocmp_ref[0, :, g * gw + LANE:(g + 1) * gw] = _pair_lanes(lane, outs[2], outs[3])
        hi = psum.astype(BF16)
        lo = (psum - hi.astype(F32)).astype(BF16)
        imp = (_dot_nt(ovt_ref[...], hi) + _dot_nt(ovt_ref[...], lo))[:n_slc]
        score = jnp.where(blk > cur, -1.0, jnp.where(forced, FORCE_SCORE, imp))
        terms = []
        for c in range(n_slc):
            sc = score[c:c + 1, :]
            before = (sc > score) | ((sc == score) & (blk > c))
            terms.append(jnp.where(before, 1.0, 0.0))
        bias_t = jnp.where(_tree_sum(terms) < float(min(SLC_TOPK, n_slc)), 0.0, NEG_INF)
        bias_t = jnp.concatenate([bias_t, jnp.full((LANE - n_slc, t), NEG_INF, F32)], axis=0)
        bias_ref[0, g] = bias_t.T.astype(BF16)


def _nsa_cmp(qn, kc, vc, overlap):
    B, S, _ = qn.shape
    t = CMP_T
    nkv = NSA_GROUPS * LANE
    return pl.pallas_call(
        functools.partial(_nsa_cmp_kernel, t=t, n_slc=S // SLC_LEN),
        grid=(B, S // t),
        in_specs=[
            pl.BlockSpec((1, t, NSA_HEADS * NSA_HD), lambda b, i: (b, i, 0)),
            pl.BlockSpec((1, kc.shape[1], nkv), lambda b, i: (b, 0, 0)),
            pl.BlockSpec((1, vc.shape[1], nkv), lambda b, i: (b, 0, 0)),
            pl.BlockSpec((LANE, LANE), lambda b, i: (0, 0)),
        ],
        out_specs=[pl.BlockSpec((1, t, NSA_HEADS * NSA_HD), lambda b, i: (b, i, 0)),
                   pl.BlockSpec((1, NSA_GROUPS, t, LANE), lambda b, i: (b, 0, i, 0))],
        out_shape=[jax.ShapeDtypeStruct((B, S, NSA_HEADS * NSA_HD), F32),
                   jax.ShapeDtypeStruct((B, NSA_GROUPS, S, LANE), BF16)],
        compiler_params=_cparams(("parallel", "parallel")),
        name="nsa_cmp",
    )(qn, kc, vc, overlap)


def _nsa_attn_kernel(q_ref, sk_ref, sv_ref, wk_ref, wv_ref, bias_ref, blk_ref, ocmp_ref, gate_ref,
                     o_ref, s_ref, mix_ref, *, t, tk):
    i = pl.program_id(1)
    r4 = NSA_REP * t
    gw = NSA_REP * NSA_HD
    groups = range(NSA_GROUPS)
    lane = lax.broadcasted_iota(jnp.int32, (t, LANE), 1)
    gl = [slice(g * LANE, (g + 1) * LANE) for g in groups]
    q4 = [jnp.concatenate(_stack_heads(q_ref[0, :, g * gw:(g + 1) * gw], lane), axis=0) for g in groups]
    q4b = [jnp.concatenate([q4[g], jnp.concatenate([bias_ref[0, g]] * NSA_REP, axis=0)], axis=1)
           for g in groups]

    def row_minus_col(width):
        row = lax.broadcasted_iota(jnp.int32, (r4, width), 0) & (t - 1)
        return row - lax.broadcasted_iota(jnp.int32, (r4, width), 1)

    tail = (i * t) // tk
    thr = i * t - tail * tk

    def score_tile(j, mx, masked):
        off = pl.multiple_of(j * tk, tk)
        onehot = blk_ref[pl.ds(off, tk), :]
        out = []
        for g in groups:
            s = _dot_nt(q4b[g], jnp.concatenate([sk_ref[0, pl.ds(off, tk), gl[g]], onehot], axis=1))
            if masked:
                s = jnp.where(row_minus_col(tk) >= -thr, s, NEG_INF)
            s_ref[g, j] = s
            out.append(_lane_block_max(mx[g], s))
        return tuple(out)

    mx = lax.fori_loop(0, tail, lambda j, mx: score_tile(j, mx, False),
                       (jnp.full((r4, LANE), NEG_INF, F32),) * NSA_GROUPS)
    mx = score_tile(tail, mx, True)
    m_s = [jnp.max(mx[g], axis=-1, keepdims=True) for g in groups]

    def gate_pair(g, br, p2):
        c0 = g * LANE + br * NSA_REP + 2 * p2
        return _pair_lanes(lane, gate_ref[0, :, c0:c0 + 1], gate_ref[0, :, c0 + 1:c0 + 2])

    def head_pair(o, p2):
        a, b = 2 * p2, 2 * p2 + 1
        return _pair_lanes(lane, o[a * t:(a + 1) * t], pltpu.roll(o[b * t:(b + 1) * t], NSA_HD, 1))

    wkeys = WIN + t
    start = pl.multiple_of(jnp.maximum(i * t - WIN, 0), t)
    dist = row_minus_col(wkeys) + (i * t - start)
    in_window = pltpu.bitcast(dist, jnp.uint32) < jnp.uint32(WIN)
    for g in groups:
        s = jnp.where(in_window, _dot_nt(q4[g], wk_ref[0, pl.ds(start, wkeys), gl[g]]), NEG_INF)
        p = jnp.exp(s - jnp.max(s, axis=-1, keepdims=True)).astype(BF16)
        o_w = _flash_finish(_dot(p, wv_ref[0, pl.ds(start, wkeys), gl[g]]))
        for p2 in range(NSA_REP // 2):
            cols = slice(g * gw + p2 * LANE, g * gw + (p2 + 1) * LANE)
            mix_ref[:, cols] = gate_pair(g, 0, p2) * ocmp_ref[0, :, cols] + gate_pair(g, 2, p2) * head_pair(o_w, p2)

    def absorb(j, acc):
        off = pl.multiple_of(j * tk, tk)
        return tuple(acc[g] + _dot(jnp.exp(s_ref[g, j] - m_s[g]).astype(BF16), sv_ref[0, pl.ds(off, tk), gl[g]])
                     for g in groups)

    acc_s = lax.fori_loop(0, tail + 1, absorb, (jnp.zeros((r4, LANE), F32),) * NSA_GROUPS)
    for g in groups:
        o_s = _flash_finish(acc_s[g])
        for p2 in range(NSA_REP // 2):
            cols = slice(g * gw + p2 * LANE, g * gw + (p2 + 1) * LANE)
            o_ref[0, :, cols] = (mix_ref[:, cols] + gate_pair(g, 1, p2) * head_pair(o_s, p2)).astype(o_ref.dtype)


def _nsa_attention(qn, sk, sv, wk, wv, bias, blk_onehot, ocmp, gates):
    B, S, _ = qn.shape
    t = ATT_T
    hw = NSA_HEADS * NSA_HD
    nkv = NSA_GROUPS * LANE
    kv = pl.BlockSpec((1, S, nkv), lambda b, i: (b, 0, 0))
    return pl.pallas_call(
        functools.partial(_nsa_attn_kernel, t=t, tk=NSA_TK),
        grid=(B, S // t),
        in_specs=[
            pl.BlockSpec((1, t, hw), lambda b, i: (b, i, 0)),
            kv, kv, kv, kv,
            pl.BlockSpec((1, NSA_GROUPS, t, LANE), lambda b, i: (b, 0, i, 0)),
            pl.BlockSpec((S, LANE), lambda b, i: (0, 0)),
            pl.BlockSpec((1, t, hw), lambda b, i: (b, i, 0)),
            pl.BlockSpec((1, t, nkv), lambda b, i: (b, i, 0)),
        ],
        out_specs=pl.BlockSpec((1, t, hw), lambda b, i: (b, i, 0)),
        out_shape=jax.ShapeDtypeStruct((B, S, hw), BF16),
        scratch_shapes=[pltpu.VMEM((NSA_GROUPS, S // NSA_TK, NSA_REP * t, NSA_TK), F32),
                        pltpu.VMEM((t, hw), F32)],
        compiler_params=_cparams(("parallel", "arbitrary")),
        name="nsa_attn",
    )(qn, sk, sv, wk, wv, bias, blk_onehot, ocmp, gates)


def _merge_kernel(x_ref, g_ref, osb_ref, omla_ref, onsa_ref, wg_ref, psb_ref, pmla_ref, pnsa_ref,
                  wout_ref, o_ref):
    x = x_ref[...]
    h = _rms(x, g_ref[...]).astype(BF16)
    y = jnp.zeros(x.shape, F32)
    for br, (o_r, p_r) in enumerate(((osb_ref, psb_ref), (omla_ref, pmla_ref), (onsa_ref, pnsa_ref))):
        gate = jax.nn.sigmoid(_dot(h, wg_ref[:, br * D_MODEL:(br + 1) * D_MODEL]))
        y = y + gate * _dot(o_r[...], p_r[...])
    o_ref[...] = x + _dot(y.astype(BF16), wout_ref[...])


def _merge(x, g, osb, omla, onsa, w):
    T, D = x.shape
    tm = MERGE_TM
    tok = lambda width: pl.BlockSpec((tm, width), lambda i: (i, 0))
    full = lambda shape: pl.BlockSpec(shape, lambda i: (0,) * len(shape))
    return pl.pallas_call(
        _merge_kernel,
        grid=(T // tm,),
        in_specs=[tok(D), full((1, D)), tok(osb.shape[1]), tok(omla.shape[1]), tok(onsa.shape[1]),
                  full((D, N_BRANCH * D)), full(w["proj_sb"].shape), full(w["proj_mla"].shape),
                  full(w["proj_nsa"].shape), full((D, D))],
        out_specs=tok(D),
        out_shape=jax.ShapeDtypeStruct((T, D), F32),
        compiler_params=_cparams(("parallel",)),
        name="merge",
    )(x, g, osb, omla, onsa, w["w_gate"], w["proj_sb"], w["proj_mla"], w["proj_nsa"], w["w_out"])


def _cos_sin(pos, d):
    half = d // 2
    inv = jnp.exp(-math.log(ROPE_THETA) * jnp.arange(half, dtype=F32) * (2.0 / d))
    ang = pos.astype(F32)[:, None] * inv[None, :]
    return jnp.cos(ang), jnp.sin(ang)


def _pair_rope_tables(pos):
    cos, sin = _cos_sin(pos, NSA_HD)
    return jnp.concatenate([cos] * 4, axis=1), jnp.concatenate([-sin, -sin, sin, sin], axis=1)


def _mla_rope_tables(pos):
    cos, sin = _cos_sin(pos, MLA_ROPE)
    n, half = cos.shape
    one = lambda k: jnp.ones((n, k), F32)
    zero = lambda k: jnp.zeros((n, k), F32)
    rest = LANE // 2 - half
    c = jnp.concatenate([cos, one(rest), cos, one(MLA_NOPE - rest), zero(LANE - MLA_QK)], axis=1)
    s = jnp.concatenate([-sin, zero(rest), sin, zero(LANE // 2 - half)], axis=1)
    return c, s


def _perm_pair(a, b):
    h = NSA_HD // 2
    return jnp.concatenate([a[..., :h], b[..., :h], a[..., h:], b[..., h:]], axis=-1)


def _perm_mla(nope, rope):
    h = MLA_ROPE // 2
    rest = LANE // 2 - h
    pad = jnp.zeros(nope.shape[:-1] + (LANE - MLA_QK,), nope.dtype)
    return jnp.concatenate([rope[..., :h], nope[..., :rest], rope[..., h:], nope[..., rest:], pad], axis=-1)


def _tables(S):
    pos = jnp.arange(S, dtype=jnp.int32)
    nchunk = S // CMP_STRIDE
    ends = jnp.arange(nchunk, dtype=jnp.int32) * CMP_STRIDE + (CMP_LEN - 1)
    t = {}
    t["nsa_c"], t["nsa_s"] = _pair_rope_tables(pos)
    t["cmp_c"], t["cmp_s"] = _pair_rope_tables(ends)
    t["mla_c"], t["mla_s"] = _mla_rope_tables(pos)
    t["blk_onehot"] = (pos[:, None] // SLC_LEN == jnp.arange(LANE, dtype=jnp.int32)[None, :]).astype(BF16)
    n_cmp = (S - CMP_LEN) // CMP_STRIDE + 1
    n_slc = S // SLC_LEN
    c0 = np.arange(n_cmp)[:, None] * CMP_STRIDE
    s0 = np.arange(n_slc)[None, :] * SLC_LEN
    ov = np.clip(np.minimum(c0 + CMP_LEN, s0 + SLC_LEN) - np.maximum(c0, s0), 0, None) / CMP_LEN
    full = np.zeros((LANE, LANE), np.float32)
    full[:n_slc, :n_cmp] = ov.T
    t["overlap"] = jnp.asarray(full, BF16)
    return t


def _pad_cols(w, width):
    return jnp.pad(w, ((0, 0), (0, width - w.shape[1])))


def _dup_groups(w, permuted):
    parts = []
    for g in range(NSA_GROUPS):
        blk = w[:, g * NSA_HD:(g + 1) * NSA_HD]
        parts.append(_perm_pair(blk, blk) if permuted else jnp.concatenate([blk, blk], axis=1))
    return jnp.concatenate(parts, axis=1)


def _blockdiag2(a):
    z = jnp.zeros_like(a)
    return jnp.concatenate([jnp.concatenate([a, z], axis=1), jnp.concatenate([z, a], axis=1)], axis=0)


def _layer_weights(l, p):
    w = {}
    row = lambda v: v[l][None, :].astype(F32)
    offs = np.concatenate([[0], np.cumsum(IN_SIZES)])
    cols = [p["w_in"][l][:, offs[k]:offs[k + 1]] for k in range(len(IN_SIZES))]
    (sb_q, sb_k, sb_v, cq, ckv, kr, nq, ck, cv, sk, sv, wk, wv, ngate, mgate) = cols
    sb_scale = 1.0 / math.sqrt(SB_HD)
    w["w_sb"] = jnp.concatenate([sb_q * sb_scale, sb_k, sb_v], axis=1).astype(BF16)
    kr_blk = _perm_mla(jnp.zeros((D_MODEL, MLA_NOPE), kr.dtype), kr)
    nq = jnp.concatenate([_perm_pair(nq[:, b * LANE:b * LANE + NSA_HD], nq[:, b * LANE + NSA_HD:(b + 1) * LANE])
                          for b in range(NSA_HEADS * NSA_HD // LANE)], axis=1)
    ng = ngate.reshape(D_MODEL, N_BRANCH, NSA_GROUPS, NSA_REP)
    ng = jnp.concatenate([_pad_cols(ng[:, :, g, :].reshape(D_MODEL, N_BRANCH * NSA_REP), LANE)
                          for g in range(NSA_GROUPS)], axis=1)
    w["w_z2"] = jnp.concatenate([cq, ckv, kr_blk, nq, ck, cv, _dup_groups(sk, True), _dup_groups(sv, False),
                                 _dup_groups(wk, True), _dup_groups(wv, False), ng], axis=1).astype(BF16)
    w["w_gate"] = mgate.astype(BF16)
    w["mix_norm"] = row(p["mix_norm"])
    w["mla_q_norm"] = row(p["mla_q_norm"])
    w["mla_kv_norm"] = row(p["mla_kv_norm"])
    wuq = p["mla_w_uq"][l].reshape(MLA_Q_LORA, MLA_HEADS, MLA_QK)
    w["wuq"] = _perm_mla(wuq[:, :, :MLA_NOPE], wuq[:, :, MLA_NOPE:]).reshape(MLA_Q_LORA, -1).astype(BF16)
    wukv = p["mla_w_ukv"][l].reshape(MLA_KV_LORA, MLA_HEADS, MLA_NOPE + MLA_V)
    no_rope = jnp.zeros((MLA_KV_LORA, MLA_HEADS, MLA_ROPE), wukv.dtype)
    w["wuk"] = _perm_mla(wukv[:, :, :MLA_NOPE], no_rope).reshape(MLA_KV_LORA, -1).astype(BF16)
    w["wuv"] = jnp.pad(wukv[:, :, MLA_NOPE:], ((0, 0), (0, 0), (0, LANE - MLA_V))).reshape(MLA_KV_LORA, -1).astype(BF16)
    gq, gk = row(p["mla_qk_gain_q"]), row(p["mla_qk_gain_k"])
    w["mla_gq"] = _perm_mla(gq[:, :MLA_NOPE], gq[:, MLA_NOPE:])
    w["mla_gk"] = _perm_mla(gk[:, :MLA_NOPE], gk[:, MLA_NOPE:])
    w["nsa_qg"] = _perm_pair(row(p["nsa_q_gain"]), row(p["nsa_q_gain"]))
    w["nsa_kg"] = _perm_pair(row(p["nsa_k_gain"]), row(p["nsa_k_gain"]))
    w["pe_k"] = jnp.tile(p["cmp_pos_k"][l], (1, 2)).astype(F32)
    w["pe_v"] = jnp.tile(p["cmp_pos_v"][l], (1, 2)).astype(F32)
    for nm, w1, w2 in (("k", p["cmp_wk1"][l], p["cmp_wk2"][l]), ("v", p["cmp_wv1"][l], p["cmp_wv2"][l])):
        w1 = w1.reshape(CMP_LEN, NSA_HD, CMP_HIDDEN)
        bd = jax.vmap(_blockdiag2)(w1).astype(BF16)
        w["w1%s_lo" % nm] = bd[:CMP_STRIDE]
        w["w1%s_hi" % nm] = bd[CMP_STRIDE:]
        w["w2" + nm] = _blockdiag2(_perm_pair(w2, w2) if nm == "k" else jnp.concatenate([w2, w2], axis=1)).astype(BF16)
    for nm in ("proj_sb", "proj_mla", "proj_nsa", "w_out"):
        w[nm] = p[nm][l].astype(BF16)
    for nm in ("ffn1", "ffn2"):
        w[nm + "_norm"] = row(p[nm + "_norm"])
        w[nm + "_wi"] = p[nm + "_wi"][l].astype(BF16)
        w[nm + "_wo"] = p[nm + "_wo"][l].astype(BF16)
    return w


def kernel(x, ffn1_norm, ffn1_wi, ffn1_wo, mix_norm, w_in, mla_q_norm, mla_w_uq, mla_kv_norm, mla_w_ukv, mla_qk_gain_q, mla_qk_gain_k, nsa_q_gain, nsa_k_gain, cmp_pos_k, cmp_pos_v, cmp_wk1, cmp_wk2, cmp_wv1, cmp_wv2, proj_sb, proj_mla, proj_nsa, w_out, ffn2_norm, ffn2_wi, ffn2_wo):
    p = dict(ffn1_norm=ffn1_norm, ffn1_wi=ffn1_wi, ffn1_wo=ffn1_wo, mix_norm=mix_norm, w_in=w_in,
             mla_q_norm=mla_q_norm, mla_w_uq=mla_w_uq, mla_kv_norm=mla_kv_norm, mla_w_ukv=mla_w_ukv,
             mla_qk_gain_q=mla_qk_gain_q, mla_qk_gain_k=mla_qk_gain_k, nsa_q_gain=nsa_q_gain,
             nsa_k_gain=nsa_k_gain, cmp_pos_k=cmp_pos_k, cmp_pos_v=cmp_pos_v, cmp_wk1=cmp_wk1,
             cmp_wk2=cmp_wk2, cmp_wv1=cmp_wv1, cmp_wv2=cmp_wv2, proj_sb=proj_sb, proj_mla=proj_mla,
             proj_nsa=proj_nsa, w_out=w_out, ffn2_norm=ffn2_norm, ffn2_wi=ffn2_wi, ffn2_wo=ffn2_wo)
    B, S, D = x.shape
    assert S // CMP_STRIDE == LANE and S // SLC_LEN <= LANE and D == D_MODEL
    tabs = _tables(S)
    xt = x.reshape(B * S, D)
    for l in range(ffn1_wi.shape[0]):
        w = _layer_weights(l, p)
        xt = _ffn(xt, w["ffn1_norm"], w["ffn1_wi"], w["ffn1_wo"])
        zsb, z2 = _inproj(xt, w["mix_norm"], w["w_sb"], w["w_z2"])
        zsb = zsb.reshape(B, S, -1)
        z2 = z2.reshape(B, S, -1)
        qm, km, vm, qn, sk, sv, wk, wv, gates = _prep(z2, w, tabs)
        kc, vc = _compress(z2, w, tabs)
        o_sb = _sb_attention(zsb)
        o_mla = _mla_attention(qm, km, vm)
        o_cmp, bias = _nsa_cmp(qn, kc, vc, tabs["overlap"])
        o_nsa = _nsa_attention(qn, sk, sv, wk, wv, bias, tabs["blk_onehot"], o_cmp, gates)
        xt = _merge(xt, w["mix_norm"], o_sb.reshape(B * S, -1), o_mla.reshape(B * S, -1),
                    o_nsa.reshape(B * S, -1), w)
        xt = _ffn(xt, w["ffn2_norm"], w["ffn2_wi"], w["ffn2_wo"])
    return xt.reshape(B, S, D)
```

```python
import functools
import math

import numpy as np
import jax
import jax.numpy as jnp
from jax import lax
from jax.experimental import pallas as pl
from jax.experimental.pallas import tpu as pltpu

D_MODEL = 1024
D_FF = 2816
EPS = 1e-6
ROPE_THETA = 10000.0
N_BRANCH = 3
SB_HEADS = 8
SB_HD = 64
MLA_HEADS = 8
MLA_NOPE = 64
MLA_ROPE = 32
MLA_V = 64
MLA_Q_LORA = 256
MLA_KV_LORA = 128
MLA_QK = MLA_NOPE + MLA_ROPE
NSA_HEADS = 8
NSA_GROUPS = 2
NSA_REP = NSA_HEADS // NSA_GROUPS
NSA_HD = 64
CMP_LEN = 32
CMP_STRIDE = 16
CMP_HIDDEN = 128
SLC_LEN = 64
SLC_TOPK = 16
WIN = 512
FORCE_SCORE = 1e3
NEG_INF = -1e30
SB_W = SB_HEADS * SB_HD
NSA_KV_W = NSA_GROUPS * NSA_HD
IN_SIZES = (SB_W, SB_W, SB_W, MLA_Q_LORA, MLA_KV_LORA, MLA_ROPE, NSA_HEADS * NSA_HD,
            NSA_KV_W, NSA_KV_W, NSA_KV_W, NSA_KV_W, NSA_KV_W, NSA_KV_W,
            N_BRANCH * NSA_HEADS, N_BRANCH * D_MODEL)

LANE = 128
V7X_VMEM_LIMIT = 56 * 1024 * 1024

FFN_TM = 1024
FFN_TF = 256
PROJ_TM = 512
PREP_TS = 512
ATT_T = 128
NSA_TK = 512
CMP_T = 256
SB_T = 256
SB_HEADS_PER_STEP = 4
MLA_TQ = 512
MLA_HEADS_PER_STEP = 4
MLA_TK = 512
MERGE_TM = 512

Z2_CQ = 0
Z2_CKV = 256
Z2_KR = 384
Z2_NQ = 512
Z2_CK = 1024
Z2_CV = 1152
Z2_SK = 1280
Z2_SV = 1536
Z2_WK = 1792
Z2_WV = 2048
Z2_GATE = 2304
Z2_W = 2560

BF16 = jnp.bfloat16
F32 = jnp.float32


def _cparams(sem, vmem=V7X_VMEM_LIMIT):
    return pltpu.CompilerParams(dimension_semantics=sem, vmem_limit_bytes=vmem)


def _dot(a, b):
    return jnp.dot(a, b, preferred_element_type=F32)


def _dot_nt(a, b):
    return lax.dot_general(a, b, (((1,), (1,)), ((), ())), preferred_element_type=F32)


def _split_rows(dot, a, b):
    h = a.shape[0] // 2
    return jnp.concatenate([dot(a[:h], b), dot(a[h:], b)], axis=0)


def _rms(x, g):
    y = x * lax.rsqrt(jnp.mean(x * x, axis=-1, keepdims=True) + EPS)
    return y * g


def _ffn_kernel(x_ref, g_ref, wa_ref, wb_ref, wo_ref, o_ref, h_ref, acc_ref):
    j = pl.program_id(1)

    @pl.when(j == 0)
    def _():
        h_ref[...] = _rms(x_ref[...], g_ref[...]).astype(BF16)
        acc_ref[...] = jnp.zeros_like(acc_ref)

    h = h_ref[...]
    a = _dot(h, wa_ref[...].astype(BF16))
    b = _dot(h, wb_ref[...].astype(BF16))
    u = (jax.nn.silu(a) * b).astype(BF16)
    acc_ref[...] += _dot(u, wo_ref[...].astype(BF16))

    @pl.when(j == pl.num_programs(1) - 1)
    def _():
        o_ref[...] = x_ref[...] + 0.5 * acc_ref[...]


def _ffn(x, g, wi, wo, l):
    T, D = x.shape
    F = wo.shape[1]
    tm, tf = FFN_TM, FFN_TF
    nf = F // tf
    return pl.pallas_call(
        _ffn_kernel,
        grid=(T // tm, nf),
        in_specs=[
            pl.BlockSpec((tm, D), lambda i, j: (i, 0)),
            pl.BlockSpec((1, D), lambda i, j: (0, 0)),
            pl.BlockSpec((None, D, tf), lambda i, j: (l, 0, j)),
            pl.BlockSpec((None, D, tf), lambda i, j: (l, 0, j + nf)),
            pl.BlockSpec((None, tf, D), lambda i, j: (l, j, 0)),
        ],
        out_specs=pl.BlockSpec((tm, D), lambda i, j: (i, 0)),
        out_shape=jax.ShapeDtypeStruct((T, D), F32),
        scratch_shapes=[pltpu.VMEM((tm, D), BF16), pltpu.VMEM((tm, D), F32)],
        compiler_params=_cparams(("parallel", "arbitrary")),
        name="ffn",
    )(x, g, wi, wi, wo)


def _rope_mix(y, c, s):
    return y * c + pltpu.roll(y, LANE // 2, 1) * s


def _first_head(lane):
    return (lane & (NSA_HD // 2)) == 0


def _norm_rope_pair(x, gain, c, s, scale):
    first = _first_head(lax.broadcasted_iota(jnp.int32, x.shape, 1))
    sq = x * x
    ss_a = jnp.sum(jnp.where(first, sq, 0.0), axis=-1, keepdims=True)
    ss_b = jnp.sum(jnp.where(first, 0.0, sq), axis=-1, keepdims=True)
    r = jnp.where(first, lax.rsqrt(ss_a * (1.0 / NSA_HD) + EPS), lax.rsqrt(ss_b * (1.0 / NSA_HD) + EPS))
    y = _rope_mix(x * r * gain, c, s)
    return y * scale if scale != 1.0 else y


def _norm_rope_mla(x, gain, c, s, scale):
    ss = jnp.sum(x * x, axis=-1, keepdims=True)
    y = _rope_mix(x * lax.rsqrt(ss * (1.0 / MLA_QK) + EPS) * gain, c, s)
    return y * scale if scale != 1.0 else y


def _inproj_kernel(x_ref, g_ref, wa_ref, wb_ref, oa_ref, ob_ref):
    h = _rms(x_ref[...], g_ref[...]).astype(BF16)
    oa_ref[...] = _dot(h, wa_ref[...]).astype(oa_ref.dtype)
    ob_ref[...] = _dot(h, wb_ref[...]).astype(ob_ref.dtype)


def _inproj(x, g, wa, wb):
    T, D = x.shape
    tm = PROJ_TM
    na, nb = wa.shape[1], wb.shape[1]
    return pl.pallas_call(
        _inproj_kernel,
        grid=(T // tm,),
        in_specs=[
            pl.BlockSpec((tm, D), lambda i: (i, 0)),
            pl.BlockSpec((1, D), lambda i: (0, 0)),
            pl.BlockSpec((D, na), lambda i: (0, 0)),
            pl.BlockSpec((D, nb), lambda i: (0, 0)),
        ],
        out_specs=[pl.BlockSpec((tm, na), lambda i: (i, 0)),
                   pl.BlockSpec((tm, nb), lambda i: (i, 0))],
        out_shape=[jax.ShapeDtypeStruct((T, na), BF16), jax.ShapeDtypeStruct((T, nb), F32)],
        compiler_params=_cparams(("parallel",)),
        name="inproj",
    )(x, g, wa, wb)


def _prep_kernel(z_ref, qn_ref, kvn_ref, wuq_ref, wuk_ref, wuv_ref, gq_ref, gk_ref,
                 mc_ref, ms_ref, nqg_ref, nkg_ref, nc_ref, ns_ref,
                 qm_ref, km_ref, vm_ref, qn_out, sk_out, sv_out, wk_out, wv_out, gate_out):
    zs = lambda start, width: z_ref[0, :, start:start + width]
    mc, ms = mc_ref[...], ms_ref[...]
    nc, ns = nc_ref[...], ns_ref[...]
    cq = _rms(zs(Z2_CQ, MLA_Q_LORA), qn_ref[...]).astype(BF16)
    ckv = _rms(zs(Z2_CKV, MLA_KV_LORA), kvn_ref[...]).astype(BF16)
    kr = zs(Z2_KR, LANE)
    q = _dot(cq, wuq_ref[...])
    kn = _dot(ckv, wuk_ref[...])
    v = _dot(ckv, wuv_ref[...])
    vlane = lax.broadcasted_iota(jnp.int32, (v.shape[0], LANE), 1)
    q_scale = 1.0 / math.sqrt(MLA_QK)
    for h in range(MLA_HEADS):
        sl = slice(h * LANE, (h + 1) * LANE)
        vm_ref[0, :, sl] = jnp.where(vlane < MLA_V, v[:, sl], 1.0).astype(BF16)
        qm_ref[0, :, sl] = _norm_rope_mla(q[:, sl], gq_ref[...], mc, ms, q_scale).astype(BF16)
        km_ref[0, :, sl] = _norm_rope_mla(kn[:, sl] + kr, gk_ref[...], mc, ms, 1.0).astype(BF16)
    nq_scale = 1.0 / math.sqrt(NSA_HD)
    for b in range(NSA_HEADS * NSA_HD // LANE):
        sl = slice(b * LANE, (b + 1) * LANE)
        qn_out[0, :, sl] = _norm_rope_pair(zs(Z2_NQ + b * LANE, LANE), nqg_ref[...], nc, ns,
                                           nq_scale).astype(BF16)
    for b in range(2):
        sl = slice(b * LANE, (b + 1) * LANE)
        sk_out[0, :, sl] = _norm_rope_pair(zs(Z2_SK + b * LANE, LANE), nkg_ref[...], nc, ns, 1.0).astype(BF16)
        wk_out[0, :, sl] = _norm_rope_pair(zs(Z2_WK + b * LANE, LANE), nkg_ref[...], nc, ns, 1.0).astype(BF16)
    glane = lax.broadcasted_iota(jnp.int32, (z_ref.shape[1], 2 * LANE), 1) & (LANE - 1)
    sv_out[0] = jnp.where(glane < NSA_HD, zs(Z2_SV, 2 * LANE), 1.0).astype(BF16)
    wv_out[0] = jnp.where(glane < NSA_HD, zs(Z2_WV, 2 * LANE), 1.0).astype(BF16)
    gate_out[0] = jax.nn.sigmoid(zs(Z2_GATE, 2 * LANE))


def _prep(z2, w, tabs):
    B, S, _ = z2.shape
    ts = PREP_TS
    full = lambda shape: pl.BlockSpec(shape, lambda b, i: (0,) * len(shape))
    tab = pl.BlockSpec((ts, LANE), lambda b, i: (i, 0))
    tok = lambda width: pl.BlockSpec((1, ts, width), lambda b, i: (b, i, 0))
    out_w = (MLA_HEADS * LANE, MLA_HEADS * LANE, MLA_HEADS * LANE, NSA_HEADS * NSA_HD,
             2 * LANE, 2 * LANE, 2 * LANE, 2 * LANE, 2 * LANE)
    out_dt = (BF16,) * 8 + (F32,)
    return pl.pallas_call(
        _prep_kernel,
        grid=(B, S // ts),
        in_specs=[
            tok(Z2_W),
            full((1, MLA_Q_LORA)), full((1, MLA_KV_LORA)),
            full((MLA_Q_LORA, MLA_HEADS * LANE)), full((MLA_KV_LORA, MLA_HEADS * LANE)),
            full((MLA_KV_LORA, MLA_HEADS * LANE)),
            full((1, LANE)), full((1, LANE)),
            tab, tab,
            full((1, LANE)), full((1, LANE)),
            tab, tab,
        ],
        out_specs=[tok(wd) for wd in out_w],
        out_shape=[jax.ShapeDtypeStruct((B, S, wd), dt) for wd, dt in zip(out_w, out_dt)],
        compiler_params=_cparams(("parallel", "parallel")),
        name="prep",
    )(z2, w["mla_q_norm"], w["mla_kv_norm"], w["wuq"], w["wuk"], w["wuv"], w["mla_gq"], w["mla_gk"],
      tabs["mla_c"], tabs["mla_s"], w["nsa_qg"], w["nsa_kg"], tabs["nsa_c"], tabs["nsa_s"])


def _compress_kernel(ck_ref, cv_ref, pek_ref, pev_ref, w1kl_ref, w1kh_ref, w2k_ref,
                     w1vl_ref, w1vh_ref, w2v_ref, kg_ref, c_ref, s_ref,
                     kc_ref, vc_ref, *, nchunk):
    def branch(t_ref, pe_ref, w1l_ref, w1h_ref, w2_ref):
        lo = jnp.zeros((nchunk, 2 * CMP_HIDDEN), F32)
        hi = jnp.zeros((nchunk, 2 * CMP_HIDDEN), F32)
        for l in range(CMP_STRIDE):
            x = t_ref[0, pl.ds(l, nchunk, stride=CMP_STRIDE), :]
            lo = lo + _dot((x + pe_ref[l:l + 1, :]).astype(BF16), w1l_ref[l])
            hi = hi + _dot((x + pe_ref[CMP_STRIDE + l:CMP_STRIDE + l + 1, :]).astype(BF16), w1h_ref[l])
        pre = lo + pltpu.roll(hi, nchunk - 1, 0)
        return _dot(jax.nn.gelu(pre).astype(BF16), w2_ref[...])

    kc = branch(ck_ref, pek_ref, w1kl_ref, w1kh_ref, w2k_ref)
    vc = branch(cv_ref, pev_ref, w1vl_ref, w1vh_ref, w2v_ref)
    for b in range(2):
        sl = slice(b * LANE, (b + 1) * LANE)
        kc_ref[0, :, sl] = _norm_rope_pair(kc[:, sl], kg_ref[...], c_ref[...], s_ref[...], 1.0).astype(BF16)
    vc_ref[0] = vc.astype(BF16)


def _compress(z2, w, tabs):
    B, S, _ = z2.shape
    nchunk = S // CMP_STRIDE
    full = lambda shape: pl.BlockSpec(shape, lambda b: (0,) * len(shape))
    tab = full((nchunk, LANE))
    return pl.pallas_call(
        functools.partial(_compress_kernel, nchunk=nchunk),
        grid=(B,),
        in_specs=[
            pl.BlockSpec((1, S, LANE), lambda b: (b, 0, Z2_CK // LANE)),
            pl.BlockSpec((1, S, LANE), lambda b: (b, 0, Z2_CV // LANE)),
            full((CMP_LEN, LANE)), full((CMP_LEN, LANE)),
            full((CMP_STRIDE, LANE, 2 * CMP_HIDDEN)), full((CMP_STRIDE, LANE, 2 * CMP_HIDDEN)),
            full((2 * CMP_HIDDEN, 2 * LANE)),
            full((CMP_STRIDE, LANE, 2 * CMP_HIDDEN)), full((CMP_STRIDE, LANE, 2 * CMP_HIDDEN)),
            full((2 * CMP_HIDDEN, 2 * LANE)),
            full((1, LANE)), tab, tab,
        ],
        out_specs=[pl.BlockSpec((1, nchunk, 2 * LANE), lambda b: (b, 0, 0))] * 2,
        out_shape=[jax.ShapeDtypeStruct((B, nchunk, 2 * LANE), BF16)] * 2,
        compiler_params=_cparams(("parallel",)),
        name="compress",
    )(z2, z2, w["pe_k"], w["pe_v"], w["w1k_lo"], w["w1k_hi"], w["w2k"],
      w["w1v_lo"], w["w1v_hi"], w["w2v"], w["nsa_kg"], tabs["cmp_c"], tabs["cmp_s"])


def _sb_kernel(q_ref, k_ref, v_ref, o_ref, acc_ref, *, t, nh):
    i = pl.program_id(2)
    w = nh * SB_HD
    lane = lax.broadcasted_iota(jnp.int32, (t, w), 1)
    row = lax.broadcasted_iota(jnp.int32, (t, t), 0)
    col = lax.broadcasted_iota(jnp.int32, (t, t), 1)
    upper = jnp.where(row > col, 1.0, 0.0).astype(BF16)
    strict = (lax.broadcasted_iota(jnp.int32, (nh * t, t), 1)
              < (lax.broadcasted_iota(jnp.int32, (nh * t, t), 0) & (t - 1)))
    q = q_ref[0]
    own = [(lane >= SB_HD * h) & (lane < SB_HD * (h + 1)) for h in range(nh)]
    qs = jnp.concatenate([jnp.where(own[h], q, jnp.zeros_like(q)) for h in range(nh)], axis=0)

    def tile(j, carry, masked):
        off = pl.multiple_of(j * t, t)
        z = _split_rows(_dot_nt, qs, k_ref[0, pl.ds(off, t), :])
        lg = jnp.log(1.0 + jnp.exp(-jnp.abs(z)))
        logsig = jnp.minimum(z, 0.0) - lg
        lom = logsig - z
        if masked:
            lom = jnp.where(strict, lom, 0.0)
        suffix = _split_rows(_dot, lom.astype(BF16), upper)
        wgt = jnp.exp(logsig + suffix + carry)
        if masked:
            wgt = jnp.where(strict, wgt, 0.0)
        pv = _split_rows(_dot, wgt.astype(BF16), v_ref[0, pl.ds(off, t), :])
        if masked:
            acc_ref[...] = pv
        else:
            acc_ref[...] += pv
        return carry + jnp.sum(lom, axis=-1, keepdims=True)

    carry = tile(i, jnp.zeros((nh * t, 1), F32), True)
    lax.fori_loop(0, i, lambda s, c: tile(i - 1 - s, c, False), carry)
    out = acc_ref[0:t, :]
    for h in range(1, nh):
        out = jnp.where(own[h], acc_ref[h * t:(h + 1) * t, :], out)
    o_ref[0] = out.astype(o_ref.dtype)


def _sb_attention(zsb):
    B, S, _ = zsb.shape
    t = SB_T
    nh = SB_HEADS_PER_STEP
    w = nh * SB_HD
    npair = SB_W // w
    return pl.pallas_call(
        functools.partial(_sb_kernel, t=t, nh=nh),
        grid=(B, npair, S // t),
        in_specs=[
            pl.BlockSpec((1, t, w), lambda b, p, i: (b, i, p)),
            pl.BlockSpec((1, S, w), lambda b, p, i: (b, 0, npair + p)),
            pl.BlockSpec((1, S, w), lambda b, p, i: (b, 0, 2 * npair + p)),
        ],
        out_specs=pl.BlockSpec((1, t, w), lambda b, p, i: (b, i, p)),
        out_shape=jax.ShapeDtypeStruct((B, S, SB_W), BF16),
        scratch_shapes=[pltpu.VMEM((nh * t, w), F32)],
        compiler_params=_cparams(("parallel", "parallel", "arbitrary")),
        name="sb_attn",
    )(zsb, zsb, zsb)


def _softmax_tile(s, vt, m, l, acc, mask):
    if mask is not None:
        s = jnp.where(mask, s, NEG_INF)
    m_new = jnp.maximum(m, jnp.max(s, axis=-1, keepdims=True))
    alpha = jnp.exp(m - m_new)
    p = jnp.exp(s - m_new)
    if mask is not None:
        p = jnp.where(mask, p, 0.0)
    l = alpha * l + jnp.sum(p, axis=-1, keepdims=True)
    acc = alpha * acc + _dot(p.astype(BF16), vt)
    return m_new, l, acc


def _flash_step(s, vt_ones, m, acc):
    m_new = jnp.maximum(m, jnp.max(s, axis=-1, keepdims=True))
    alpha = jnp.exp(m - m_new)
    p = jnp.exp(s - m_new).astype(BF16)
    return m_new, alpha * acc + _dot(p, vt_ones)


def _flash_finish(acc):
    return acc / jnp.maximum(pltpu.roll(acc, NSA_HD, 1), 1e-30)


def _lane_block_max(m, s):
    for c in range(s.shape[1] // LANE):
        m = jnp.maximum(m, s[:, c * LANE:(c + 1) * LANE])
    return m


def _mla_kernel(q_ref, k_ref, v_ref, o_ref, s_ref, *, tq, tk, nh):
    i = pl.program_id(2)
    lane = lax.broadcasted_iota(jnp.int32, (tq, LANE), 1)
    col_minus_row = (lax.broadcasted_iota(jnp.int32, (tq, tk), 1)
                     - lax.broadcasted_iota(jnp.int32, (tq, tk), 0))
    tail = (i * tq) // tk
    thr = i * tq - tail * tk
    heads = [slice(h * LANE, (h + 1) * LANE) for h in range(nh)]

    def score_tile(j, mx, masked):
        off = pl.multiple_of(j * tk, tk)
        out = []
        for h, sl in enumerate(heads):
            s = _dot_nt(q_ref[0, :, sl], k_ref[0, pl.ds(off, tk), sl])
            if masked:
                s = jnp.where(col_minus_row <= thr, s, NEG_INF)
            s_ref[h, j] = s
            out.append(_lane_block_max(mx[h], s))
        return tuple(out)

    mx = lax.fori_loop(0, tail, lambda j, mx: score_tile(j, mx, False),
                       (jnp.full((tq, LANE), NEG_INF, F32),) * nh)
    mx = score_tile(tail, mx, True)
    m = [jnp.max(mx[h], axis=-1, keepdims=True) for h in range(nh)]

    def absorb(j, acc):
        off = pl.multiple_of(j * tk, tk)
        return tuple(acc[h] + _dot(jnp.exp(s_ref[h, j] - m[h]).astype(BF16), v_ref[0, pl.ds(off, tk), sl])
                     for h, sl in enumerate(heads))

    acc = lax.fori_loop(0, tail + 1, absorb, (jnp.zeros((tq, LANE), F32),) * nh)
    for p in range(nh // 2):
        o = jnp.where(lane < MLA_V, _flash_finish(acc[2 * p]),
                      pltpu.roll(_flash_finish(acc[2 * p + 1]), MLA_V, 1))
        o_ref[0, :, p * LANE:(p + 1) * LANE] = o.astype(o_ref.dtype)


def _mla_attention(qm, km, vm):
    B, S, _ = qm.shape
    t = MLA_TQ
    nh = MLA_HEADS_PER_STEP
    ngrp = MLA_HEADS // nh
    return pl.pallas_call(
        functools.partial(_mla_kernel, tq=t, tk=MLA_TK, nh=nh),
        grid=(B, ngrp, S // t),
        in_specs=[
            pl.BlockSpec((1, t, nh * LANE), lambda b, p, i: (b, i, p)),
            pl.BlockSpec((1, S, nh * LANE), lambda b, p, i: (b, 0, p)),
            pl.BlockSpec((1, S, nh * LANE), lambda b, p, i: (b, 0, p)),
        ],
        out_specs=pl.BlockSpec((1, t, nh * MLA_V), lambda b, p, i: (b, i, p)),
        out_shape=jax.ShapeDtypeStruct((B, S, MLA_HEADS * MLA_V), BF16),
        scratch_shapes=[pltpu.VMEM((nh, S // MLA_TK, t, MLA_TK), F32)],
        compiler_params=_cparams(("parallel", "parallel", "arbitrary")),
        name="mla_attn",
    )(qm, km, vm)


def _stack_heads(q, lane):
    first = _first_head(lane)
    parts = []
    for h in range(NSA_REP):
        blk = q[:, (h // 2) * LANE:(h // 2 + 1) * LANE]
        parts.append(jnp.where(first if h % 2 == 0 else ~first, blk, jnp.zeros_like(blk)))
    return parts


def _pair_lanes(lane, a, b):
    return jnp.where(lane < NSA_HD, a, b)


def _tree_sum(terms):
    while len(terms) > 1:
        terms = [a + b for a, b in zip(terms[0::2], terms[1::2])] + (terms[-1:] if len(terms) % 2 else [])
    return terms[0]


def _nsa_cmp_kernel(q_ref, kc_ref, vc_ref, ovt_ref, ocmp_ref, bias_ref, *, t, n_slc):
    i = pl.program_id(1)
    gw = NSA_REP * NSA_HD
    lane = lax.broadcasted_iota(jnp.int32, (t, LANE), 1)
    row = lax.broadcasted_iota(jnp.int32, (t, LANE), 0)
    visible = lane * CMP_STRIDE + (CMP_LEN - 1) <= i * t + row
    blk = lax.broadcasted_iota(jnp.int32, (n_slc, t), 0)
    cur = (i * t + lax.broadcasted_iota(jnp.int32, (n_slc, t), 1)) // SLC_LEN
    forced = (blk == 0) | (blk == cur) | (blk == cur - 1)
    for g in range(NSA_GROUPS):
        kc = kc_ref[0, :, g * LANE:(g + 1) * LANE]
        vc = vc_ref[0, :, g * LANE:(g + 1) * LANE]
        psum = jnp.zeros((t, LANE), F32)
        outs = []
        for qh in _stack_heads(q_ref[0, :, g * gw:(g + 1) * gw], lane):
            s = jnp.where(visible, _dot_nt(qh, kc), NEG_INF)
            m = jnp.max(s, axis=-1, keepdims=True)
            e = jnp.where(visible, jnp.exp(s - m), 0.0)
            p = e / jnp.maximum(jnp.sum(e, axis=-1, keepdims=True), 1e-30)
            psum = psum + p
            outs.append(_dot(p.astype(BF16), vc))
        ocmp_ref[0, :, g * gw:g * gw + LANE] = _pair_lanes(lane, outs[0], outs[1])
        ocmp_ref[0, :, g * gw + LANE:(g + 1) * gw] = _pair_lanes(lane, outs[2], outs[3])
        hi = psum.astype(BF16)
        lo = (psum - hi.astype(F32)).astype(BF16)
        imp = (_dot_nt(ovt_ref[...], hi) + _dot_nt(ovt_ref[...], lo))[:n_slc]
        score = jnp.where(blk > cur, -1.0, jnp.where(forced, FORCE_SCORE, imp))
        terms = []
        for c in range(n_slc):
            sc = score[c:c + 1, :]
            before = (sc > score) | ((sc == score) & (blk > c))
            terms.append(jnp.where(before, 1.0, 0.0))
        bias_t = jnp.where(_tree_sum(terms) < float(min(SLC_TOPK, n_slc)), 0.0, NEG_INF)
        bias_t = jnp.concatenate([bias_t, jnp.full((LANE - n_slc, t), NEG_INF, F32)], axis=0)
        bias_ref[0, g] = bias_t.T.astype(BF16)


def _nsa_cmp(qn, kc, vc, overlap):
    B, S, _ = qn.shape
    t = CMP_T
    nkv = NSA_GROUPS * LANE
    return pl.pallas_call(
        functools.partial(_nsa_cmp_kernel, t=t, n_slc=S // SLC_LEN),
        grid=(B, S // t),
        in_specs=[
            pl.BlockSpec((1, t, NSA_HEADS * NSA_HD), lambda b, i: (b, i, 0)),
            pl.BlockSpec((1, kc.shape[1], nkv), lambda b, i: (b, 0, 0)),
            pl.BlockSpec((1, vc.shape[1], nkv), lambda b, i: (b, 0, 0)),
            pl.BlockSpec((LANE, LANE), lambda b, i: (0, 0)),
        ],
        out_specs=[pl.BlockSpec((1, t, NSA_HEADS * NSA_HD), lambda b, i: (b, i, 0)),
                   pl.BlockSpec((1, NSA_GROUPS, t, LANE), lambda b, i: (b, 0, i, 0))],
        out_shape=[jax.ShapeDtypeStruct((B, S, NSA_HEADS * NSA_HD), F32),
                   jax.ShapeDtypeStruct((B, NSA_GROUPS, S, LANE), BF16)],
        compiler_params=_cparams(("parallel", "parallel")),
        name="nsa_cmp",
    )(qn, kc, vc, overlap)


def _nsa_attn_kernel(q_ref, sk_ref, sv_ref, wk_ref, wv_ref, bias_ref, blk_ref, ocmp_ref, gate_ref,
                     o_ref, s_ref, mix_ref, *, t, tk):
    i = pl.program_id(1)
    r4 = NSA_REP * t
    gw = NSA_REP * NSA_HD
    groups = range(NSA_GROUPS)
    lane = lax.broadcasted_iota(jnp.int32, (t, LANE), 1)
    gl = [slice(g * LANE, (g + 1) * LANE) for g in groups]
    q4 = [jnp.concatenate(_stack_heads(q_ref[0, :, g * gw:(g + 1) * gw], lane), axis=0) for g in groups]
    q4b = [jnp.concatenate([q4[g], jnp.concatenate([bias_ref[0, g]] * NSA_REP, axis=0)], axis=1)
           for g in groups]

    rmc = ((lax.broadcasted_iota(jnp.int32, (r4, t), 0) & (t - 1))
           - lax.broadcasted_iota(jnp.int32, (r4, t), 1))

    def mask_blocks(s, visible):
        return jnp.concatenate([jnp.where(visible(c), s[:, c * t:(c + 1) * t], NEG_INF)
                                for c in range(s.shape[1] // t)], axis=1)

    tail = (i * t) // tk
    thr = i * t - tail * tk

    def score_tile(j, mx, masked):
        off = pl.multiple_of(j * tk, tk)
        onehot = blk_ref[pl.ds(off, tk), :]
        out = []
        for g in groups:
            s = _dot_nt(q4b[g], jnp.concatenate([sk_ref[0, pl.ds(off, tk), gl[g]], onehot], axis=1))
            if masked:
                s = mask_blocks(s, lambda c: rmc >= c * t - thr)
            s_ref[g, j] = s
            out.append(_lane_block_max(mx[g], s))
        return tuple(out)

    mx = lax.fori_loop(0, tail, lambda j, mx: score_tile(j, mx, False),
                       (jnp.full((r4, LANE), NEG_INF, F32),) * NSA_GROUPS)
    mx = score_tile(tail, mx, True)
    m_s = [jnp.max(mx[g], axis=-1, keepdims=True) for g in groups]

    def gate_pair(g, br, p2):
        c0 = g * LANE + br * NSA_REP + 2 * p2
        return _pair_lanes(lane, gate_ref[0, :, c0:c0 + 1], gate_ref[0, :, c0 + 1:c0 + 2])

    def head_pair(o, p2):
        a, b = 2 * p2, 2 * p2 + 1
        return _pair_lanes(lane, o[a * t:(a + 1) * t], pltpu.roll(o[b * t:(b + 1) * t], NSA_HD, 1))

    wkeys = WIN + t
    start = pl.multiple_of(jnp.maximum(i * t - WIN, 0), t)
    lead = i * t - start

    def in_window(c):
        return pltpu.bitcast(rmc + (lead - c * t), jnp.uint32) < jnp.uint32(WIN)

    for g in groups:
        s = mask_blocks(_dot_nt(q4[g], wk_ref[0, pl.ds(start, wkeys), gl[g]]), in_window)
        p = jnp.exp(s - jnp.max(s, axis=-1, keepdims=True)).astype(BF16)
        o_w = _flash_finish(_dot(p, wv_ref[0, pl.ds(start, wkeys), gl[g]]))
        for p2 in range(NSA_REP // 2):
            cols = slice(g * gw + p2 * LANE, g * gw + (p2 + 1) * LANE)
            mix_ref[:, cols] = gate_pair(g, 0, p2) * ocmp_ref[0, :, cols] + gate_pair(g, 2, p2) * head_pair(o_w, p2)

    def absorb(j, acc):
        off = pl.multiple_of(j * tk, tk)
        return tuple(acc[g] + _dot(jnp.exp(s_ref[g, j] - m_s[g]).astype(BF16), sv_ref[0, pl.ds(off, tk), gl[g]])
                     for g in groups)

    acc_s = lax.fori_loop(0, tail + 1, absorb, (jnp.zeros((r4, LANE), F32),) * NSA_GROUPS)
    for g in groups:
        o_s = _flash_finish(acc_s[g])
        for p2 in range(NSA_REP // 2):
            cols = slice(g * gw + p2 * LANE, g * gw + (p2 + 1) * LANE)
            o_ref[0, :, cols] = (mix_ref[:, cols] + gate_pair(g, 1, p2) * head_pair(o_s, p2)).astype(o_ref.dtype)


def _nsa_attention(qn, sk, sv, wk, wv, bias, blk_onehot, ocmp, gates):
    B, S, _ = qn.shape
    t = ATT_T
    hw = NSA_HEADS * NSA_HD
    nkv = NSA_GROUPS * LANE
    kv = pl.BlockSpec((1, S, nkv), lambda b, i: (b, 0, 0))
    return pl.pallas_call(
        functools.partial(_nsa_attn_kernel, t=t, tk=NSA_TK),
        grid=(B, S // t),
        in_specs=[
            pl.BlockSpec((1, t, hw), lambda b, i: (b, i, 0)),
            kv, kv, kv, kv,
            pl.BlockSpec((1, NSA_GROUPS, t, LANE), lambda b, i: (b, 0, i, 0)),
            pl.BlockSpec((S, LANE), lambda b, i: (0, 0)),
            pl.BlockSpec((1, t, hw), lambda b, i: (b, i, 0)),
            pl.BlockSpec((1, t, nkv), lambda b, i: (b, i, 0)),
        ],
        out_specs=pl.BlockSpec((1, t, hw), lambda b, i: (b, i, 0)),
        out_shape=jax.ShapeDtypeStruct((B, S, hw), BF16),
        scratch_shapes=[pltpu.VMEM((NSA_GROUPS, S // NSA_TK, NSA_REP * t, NSA_TK), F32),
                        pltpu.VMEM((t, hw), F32)],
        compiler_params=_cparams(("parallel", "arbitrary")),
        name="nsa_attn",
    )(qn, sk, sv, wk, wv, bias, blk_onehot, ocmp, gates)


def _merge_kernel(x_ref, g_ref, osb_ref, omla_ref, onsa_ref, wg_ref, psb_ref, pmla_ref, pnsa_ref,
                  wout_ref, o_ref):
    x = x_ref[...]
    h = _rms(x, g_ref[...]).astype(BF16)
    y = jnp.zeros(x.shape, F32)
    for br, (o_r, p_r) in enumerate(((osb_ref, psb_ref), (omla_ref, pmla_ref), (onsa_ref, pnsa_ref))):
        gate = jax.nn.sigmoid(_dot(h, wg_ref[:, br * D_MODEL:(br + 1) * D_MODEL]))
        y = y + gate * _dot(o_r[...], p_r[...])
    o_ref[...] = x + _dot(y.astype(BF16), wout_ref[...])


def _merge(x, g, osb, omla, onsa, w):
    T, D = x.shape
    tm = MERGE_TM
    tok = lambda width: pl.BlockSpec((tm, width), lambda i: (i, 0))
    full = lambda shape: pl.BlockSpec(shape, lambda i: (0,) * len(shape))
    return pl.pallas_call(
        _merge_kernel,
        grid=(T // tm,),
        in_specs=[tok(D), full((1, D)), tok(osb.shape[1]), tok(omla.shape[1]), tok(onsa.shape[1]),
                  full((D, N_BRANCH * D)), full(w["proj_sb"].shape), full(w["proj_mla"].shape),
                  full(w["proj_nsa"].shape), full((D, D))],
        out_specs=tok(D),
        out_shape=jax.ShapeDtypeStruct((T, D), F32),
        compiler_params=_cparams(("parallel",)),
        name="merge",
    )(x, g, osb, omla, onsa, w["w_gate"], w["proj_sb"], w["proj_mla"], w["proj_nsa"], w["w_out"])


def _cos_sin(pos, d):
    half = d // 2
    inv = jnp.exp(-math.log(ROPE_THETA) * jnp.arange(half, dtype=F32) * (2.0 / d))
    ang = pos.astype(F32)[:, None] * inv[None, :]
    return jnp.cos(ang), jnp.sin(ang)


def _pair_rope_tables(pos):
    cos, sin = _cos_sin(pos, NSA_HD)
    return jnp.concatenate([cos] * 4, axis=1), jnp.concatenate([-sin, -sin, sin, sin], axis=1)


def _mla_rope_tables(pos):
    cos, sin = _cos_sin(pos, MLA_ROPE)
    n, half = cos.shape
    one = lambda k: jnp.ones((n, k), F32)
    zero = lambda k: jnp.zeros((n, k), F32)
    rest = LANE // 2 - half
    c = jnp.concatenate([cos, one(rest), cos, one(MLA_NOPE - rest), zero(LANE - MLA_QK)], axis=1)
    s = jnp.concatenate([-sin, zero(rest), sin, zero(LANE // 2 - half)], axis=1)
    return c, s


def _perm_pair(a, b):
    h = NSA_HD // 2
    return jnp.concatenate([a[..., :h], b[..., :h], a[..., h:], b[..., h:]], axis=-1)


def _perm_mla(nope, rope):
    h = MLA_ROPE // 2
    rest = LANE // 2 - h
    pad = jnp.zeros(nope.shape[:-1] + (LANE - MLA_QK,), nope.dtype)
    return jnp.concatenate([rope[..., :h], nope[..., :rest], rope[..., h:], nope[..., rest:], pad], axis=-1)


def _tables(S):
    pos = jnp.arange(S, dtype=jnp.int32)
    nchunk = S // CMP_STRIDE
    ends = jnp.arange(nchunk, dtype=jnp.int32) * CMP_STRIDE + (CMP_LEN - 1)
    t = {}
    t["nsa_c"], t["nsa_s"] = _pair_rope_tables(pos)
    t["cmp_c"], t["cmp_s"] = _pair_rope_tables(ends)
    t["mla_c"], t["mla_s"] = _mla_rope_tables(pos)
    t["blk_onehot"] = (pos[:, None] // SLC_LEN == jnp.arange(LANE, dtype=jnp.int32)[None, :]).astype(BF16)
    n_cmp = (S - CMP_LEN) // CMP_STRIDE + 1
    n_slc = S // SLC_LEN
    c0 = np.arange(n_cmp)[:, None] * CMP_STRIDE
    s0 = np.arange(n_slc)[None, :] * SLC_LEN
    ov = np.clip(np.minimum(c0 + CMP_LEN, s0 + SLC_LEN) - np.maximum(c0, s0), 0, None) / CMP_LEN
    full = np.zeros((LANE, LANE), np.float32)
    full[:n_slc, :n_cmp] = ov.T
    t["overlap"] = jnp.asarray(full, BF16)
    return t


def _pad_cols(w, width):
    return jnp.pad(w, ((0, 0), (0, width - w.shape[1])))


def _dup_groups(w, permuted):
    parts = []
    for g in range(NSA_GROUPS):
        blk = w[:, g * NSA_HD:(g + 1) * NSA_HD]
        parts.append(_perm_pair(blk, blk) if permuted else jnp.concatenate([blk, blk], axis=1))
    return jnp.concatenate(parts, axis=1)


def _blockdiag2(a):
    z = jnp.zeros_like(a)
    return jnp.concatenate([jnp.concatenate([a, z], axis=1), jnp.concatenate([z, a], axis=1)], axis=0)


def _layer_weights(l, p):
    w = {}
    row = lambda v: v[l][None, :].astype(F32)
    offs = np.concatenate([[0], np.cumsum(IN_SIZES)])
    cols = [p["w_in"][l][:, offs[k]:offs[k + 1]] for k in range(len(IN_SIZES))]
    (sb_q, sb_k, sb_v, cq, ckv, kr, nq, ck, cv, sk, sv, wk, wv, ngate, mgate) = cols
    sb_scale = 1.0 / math.sqrt(SB_HD)
    w["w_sb"] = jnp.concatenate([sb_q * sb_scale, sb_k, sb_v], axis=1).astype(BF16)
    kr_blk = _perm_mla(jnp.zeros((D_MODEL, MLA_NOPE), kr.dtype), kr)
    nq = jnp.concatenate([_perm_pair(nq[:, b * LANE:b * LANE + NSA_HD], nq[:, b * LANE + NSA_HD:(b + 1) * LANE])
                          for b in range(NSA_HEADS * NSA_HD // LANE)], axis=1)
    ng = ngate.reshape(D_MODEL, N_BRANCH, NSA_GROUPS, NSA_REP)
    ng = jnp.concatenate([_pad_cols(ng[:, :, g, :].reshape(D_MODEL, N_BRANCH * NSA_REP), LANE)
                          for g in range(NSA_GROUPS)], axis=1)
    w["w_z2"] = jnp.concatenate([cq, ckv, kr_blk, nq, ck, cv, _dup_groups(sk, True), _dup_groups(sv, False),
                                 _dup_groups(wk, True), _dup_groups(wv, False), ng], axis=1).astype(BF16)
    w["w_gate"] = mgate.astype(BF16)
    w["mix_norm"] = row(p["mix_norm"])
    w["mla_q_norm"] = row(p["mla_q_norm"])
    w["mla_kv_norm"] = row(p["mla_kv_norm"])
    wuq = p["mla_w_uq"][l].reshape(MLA_Q_LORA, MLA_HEADS, MLA_QK)
    w["wuq"] = _perm_mla(wuq[:, :, :MLA_NOPE], wuq[:, :, MLA_NOPE:]).reshape(MLA_Q_LORA, -1).astype(BF16)
    wukv = p["mla_w_ukv"][l].reshape(MLA_KV_LORA, MLA_HEADS, MLA_NOPE + MLA_V)
    no_rope = jnp.zeros((MLA_KV_LORA, MLA_HEADS, MLA_ROPE), wukv.dtype)
    w["wuk"] = _perm_mla(wukv[:, :, :MLA_NOPE], no_rope).reshape(MLA_KV_LORA, -1).astype(BF16)
    w["wuv"] = jnp.pad(wukv[:, :, MLA_NOPE:], ((0, 0), (0, 0), (0, LANE - MLA_V))).reshape(MLA_KV_LORA, -1).astype(BF16)
    gq, gk = row(p["mla_qk_gain_q"]), row(p["mla_qk_gain_k"])
    w["mla_gq"] = _perm_mla(gq[:, :MLA_NOPE], gq[:, MLA_NOPE:])
    w["mla_gk"] = _perm_mla(gk[:, :MLA_NOPE], gk[:, MLA_NOPE:])
    w["nsa_qg"] = _perm_pair(row(p["nsa_q_gain"]), row(p["nsa_q_gain"]))
    w["nsa_kg"] = _perm_pair(row(p["nsa_k_gain"]), row(p["nsa_k_gain"]))
    w["pe_k"] = jnp.tile(p["cmp_pos_k"][l], (1, 2)).astype(F32)
    w["pe_v"] = jnp.tile(p["cmp_pos_v"][l], (1, 2)).astype(F32)
    for nm, w1, w2 in (("k", p["cmp_wk1"][l], p["cmp_wk2"][l]), ("v", p["cmp_wv1"][l], p["cmp_wv2"][l])):
        w1 = w1.reshape(CMP_LEN, NSA_HD, CMP_HIDDEN)
        bd = jax.vmap(_blockdiag2)(w1).astype(BF16)
        w["w1%s_lo" % nm] = bd[:CMP_STRIDE]
        w["w1%s_hi" % nm] = bd[CMP_STRIDE:]
        w["w2" + nm] = _blockdiag2(_perm_pair(w2, w2) if nm == "k" else jnp.concatenate([w2, w2], axis=1)).astype(BF16)
    for nm in ("proj_sb", "proj_mla", "proj_nsa", "w_out"):
        w[nm] = p[nm][l].astype(BF16)
    for nm in ("ffn1", "ffn2"):
        w[nm + "_norm"] = row(p[nm + "_norm"])
    return w


def kernel(x, ffn1_norm, ffn1_wi, ffn1_wo, mix_norm, w_in, mla_q_norm, mla_w_uq, mla_kv_norm, mla_w_ukv, mla_qk_gain_q, mla_qk_gain_k, nsa_q_gain, nsa_k_gain, cmp_pos_k, cmp_pos_v, cmp_wk1, cmp_wk2, cmp_wv1, cmp_wv2, proj_sb, proj_mla, proj_nsa, w_out, ffn2_norm, ffn2_wi, ffn2_wo):
    p = dict(ffn1_norm=ffn1_norm, ffn1_wi=ffn1_wi, ffn1_wo=ffn1_wo, mix_norm=mix_norm, w_in=w_in,
             mla_q_norm=mla_q_norm, mla_w_uq=mla_w_uq, mla_kv_norm=mla_kv_norm, mla_w_ukv=mla_w_ukv,
             mla_qk_gain_q=mla_qk_gain_q, mla_qk_gain_k=mla_qk_gain_k, nsa_q_gain=nsa_q_gain,
             nsa_k_gain=nsa_k_gain, cmp_pos_k=cmp_pos_k, cmp_pos_v=cmp_pos_v, cmp_wk1=cmp_wk1,
             cmp_wk2=cmp_wk2, cmp_wv1=cmp_wv1, cmp_wv2=cmp_wv2, proj_sb=proj_sb, proj_mla=proj_mla,
             proj_nsa=proj_nsa, w_out=w_out, ffn2_norm=ffn2_norm, ffn2_wi=ffn2_wi, ffn2_wo=ffn2_wo)
    B, S, D = x.shape
    assert S // CMP_STRIDE == LANE and S // SLC_LEN <= LANE and D == D_MODEL
    tabs = _tables(S)
    xt = x.reshape(B * S, D)
    for l in range(ffn1_wi.shape[0]):
        w = _layer_weights(l, p)
        xt = _ffn(xt, w["ffn1_norm"], ffn1_wi, ffn1_wo, l)
        zsb, z2 = _inproj(xt, w["mix_norm"], w["w_sb"], w["w_z2"])
        zsb = zsb.reshape(B, S, -1)
        z2 = z2.reshape(B, S, -1)
        qm, km, vm, qn, sk, sv, wk, wv, gates = _prep(z2, w, tabs)
        kc, vc = _compress(z2, w, tabs)
        o_sb = _sb_attention(zsb)
        o_mla = _mla_attention(qm, km, vm)
        o_cmp, bias = _nsa_cmp(qn, kc, vc, tabs["overlap"])
        o_nsa = _nsa_attention(qn, sk, sv, wk, wv, bias, tabs["blk_onehot"], o_cmp, gates)
        xt = _merge(xt, w["mix_norm"], o_sb.reshape(B * S, -1), o_mla.reshape(B * S, -1),
                    o_nsa.reshape(B * S, -1), w)
        xt = _ffn(xt, w["ffn2_norm"], ffn2_wi, ffn2_wo, l)
    return xt.reshape(B, S, D)
```

```python
import functools
import math

import numpy as np
import jax
import jax.numpy as jnp
from jax import lax
from jax.experimental import pallas as pl
from jax.experimental.pallas import tpu as pltpu

D_MODEL = 1024
D_FF = 2816
EPS = 1e-6
ROPE_THETA = 10000.0
N_BRANCH = 3
SB_HEADS = 8
SB_HD = 64
MLA_HEADS = 8
MLA_NOPE = 64
MLA_ROPE = 32
MLA_V = 64
MLA_Q_LORA = 256
MLA_KV_LORA = 128
MLA_QK = MLA_NOPE + MLA_ROPE
NSA_HEADS = 8
NSA_GROUPS = 2
NSA_REP = NSA_HEADS // NSA_GROUPS
NSA_HD = 64
CMP_LEN = 32
CMP_STRIDE = 16
CMP_HIDDEN = 128
SLC_LEN = 64
SLC_TOPK = 16
WIN = 512
FORCE_SCORE = 1e3
NEG_INF = -1e30
SB_W = SB_HEADS * SB_HD
NSA_KV_W = NSA_GROUPS * NSA_HD
IN_SIZES = (SB_W, SB_W, SB_W, MLA_Q_LORA, MLA_KV_LORA, MLA_ROPE, NSA_HEADS * NSA_HD,
            NSA_KV_W, NSA_KV_W, NSA_KV_W, NSA_KV_W, NSA_KV_W, NSA_KV_W,
            N_BRANCH * NSA_HEADS, N_BRANCH * D_MODEL)

LANE = 128
V7X_VMEM_LIMIT = 56 * 1024 * 1024

FFN_TM = 1024
FFN_TF = 256
PROJ_TM = 512
PREP_TS = 512
ATT_T = 128
NSA_TK = 512
CMP_T = 256
SB_T = 256
SB_HEADS_PER_STEP = 4
MLA_TQ = 512
MLA_HEADS_PER_STEP = 4
MLA_TK = 512
MERGE_TM = 512

Z2_CQ = 0
Z2_CKV = 256
Z2_KR = 384
Z2_NQ = 512
Z2_CK = 1024
Z2_CV = 1152
Z2_SK = 1280
Z2_SV = 1536
Z2_WK = 1792
Z2_WV = 2048
Z2_GATE = 2304
Z2_W = 2560

BF16 = jnp.bfloat16
F32 = jnp.float32


def _cparams(sem, vmem=V7X_VMEM_LIMIT):
    return pltpu.CompilerParams(dimension_semantics=sem, vmem_limit_bytes=vmem)


def _dot(a, b):
    return jnp.dot(a, b, preferred_element_type=F32)


def _dot_nt(a, b):
    return lax.dot_general(a, b, (((1,), (1,)), ((), ())), preferred_element_type=F32)


def _split_rows(dot, a, b):
    h = a.shape[0] // 2
    return jnp.concatenate([dot(a[:h], b), dot(a[h:], b)], axis=0)


def _rms(x, g):
    y = x * lax.rsqrt(jnp.mean(x * x, axis=-1, keepdims=True) + EPS)
    return y * g


def _ffn_kernel(x_ref, g_ref, wa_ref, wb_ref, wo_ref, o_ref, h_ref, acc_ref):
    j = pl.program_id(1)

    @pl.when(j == 0)
    def _():
        h_ref[...] = _rms(x_ref[...], g_ref[...]).astype(BF16)
        acc_ref[...] = jnp.zeros_like(acc_ref)

    h = h_ref[...]
    a = _dot(h, wa_ref[...].astype(BF16))
    b = _dot(h, wb_ref[...].astype(BF16))
    u = (jax.nn.silu(a) * b).astype(BF16)
    acc_ref[...] += _dot(u, wo_ref[...].astype(BF16))

    @pl.when(j == pl.num_programs(1) - 1)
    def _():
        o_ref[...] = x_ref[...] + 0.5 * acc_ref[...]


def _ffn(x, g, wi, wo, l):
    T, D = x.shape
    F = wo.shape[1]
    tm, tf = FFN_TM, FFN_TF
    nf = F // tf
    return pl.pallas_call(
        _ffn_kernel,
        grid=(T // tm, nf),
        in_specs=[
            pl.BlockSpec((tm, D), lambda i, j: (i, 0)),
            pl.BlockSpec((1, D), lambda i, j: (0, 0)),
            pl.BlockSpec((None, D, tf), lambda i, j: (l, 0, j)),
            pl.BlockSpec((None, D, tf), lambda i, j: (l, 0, j + nf)),
            pl.BlockSpec((None, tf, D), lambda i, j: (l, j, 0)),
        ],
        out_specs=pl.BlockSpec((tm, D), lambda i, j: (i, 0)),
        out_shape=jax.ShapeDtypeStruct((T, D), F32),
        scratch_shapes=[pltpu.VMEM((tm, D), BF16), pltpu.VMEM((tm, D), F32)],
        compiler_params=_cparams(("parallel", "arbitrary")),
        name="ffn",
    )(x, g, wi, wi, wo)


def _rope_mix(y, c, s):
    return y * c + pltpu.roll(y, LANE // 2, 1) * s


def _first_head(lane):
    return (lane & (NSA_HD // 2)) == 0


def _norm_rope_pair(x, gain, c, s, scale):
    first = _first_head(lax.broadcasted_iota(jnp.int32, x.shape, 1))
    sq = x * x
    ss_a = jnp.sum(jnp.where(first, sq, 0.0), axis=-1, keepdims=True)
    ss_b = jnp.sum(jnp.where(first, 0.0, sq), axis=-1, keepdims=True)
    r = jnp.where(first, lax.rsqrt(ss_a * (1.0 / NSA_HD) + EPS), lax.rsqrt(ss_b * (1.0 / NSA_HD) + EPS))
    y = _rope_mix(x * r * gain, c, s)
    return y * scale if scale != 1.0 else y


def _norm_rope_mla(x, gain, c, s, scale):
    ss = jnp.sum(x * x, axis=-1, keepdims=True)
    y = _rope_mix(x * lax.rsqrt(ss * (1.0 / MLA_QK) + EPS) * gain, c, s)
    return y * scale if scale != 1.0 else y


def _inproj_kernel(x_ref, g_ref, wa_ref, wb_ref, oa_ref, ob_ref):
    h = _rms(x_ref[...], g_ref[...]).astype(BF16)
    oa_ref[...] = _dot(h, wa_ref[...]).astype(oa_ref.dtype)
    ob_ref[...] = _dot(h, wb_ref[...]).astype(ob_ref.dtype)


def _inproj(x, g, wa, wb):
    T, D = x.shape
    tm = PROJ_TM
    na, nb = wa.shape[1], wb.shape[1]
    return pl.pallas_call(
        _inproj_kernel,
        grid=(T // tm,),
        in_specs=[
            pl.BlockSpec((tm, D), lambda i: (i, 0)),
            pl.BlockSpec((1, D), lambda i: (0, 0)),
            pl.BlockSpec((D, na), lambda i: (0, 0)),
            pl.BlockSpec((D, nb), lambda i: (0, 0)),
        ],
        out_specs=[pl.BlockSpec((tm, na), lambda i: (i, 0)),
                   pl.BlockSpec((tm, nb), lambda i: (i, 0))],
        out_shape=[jax.ShapeDtypeStruct((T, na), BF16), jax.ShapeDtypeStruct((T, nb), F32)],
        compiler_params=_cparams(("parallel",)),
        name="inproj",
    )(x, g, wa, wb)


def _prep_kernel(z_ref, qn_ref, kvn_ref, wuq_ref, wuk_ref, wuv_ref, gq_ref, gk_ref,
                 mc_ref, ms_ref, nqg_ref, nkg_ref, nc_ref, ns_ref,
                 qm_ref, km_ref, vm_ref, qn_out, sk_out, sv_out, wk_out, wv_out, gate_out):
    zs = lambda start, width: z_ref[0, :, start:start + width]
    mc, ms = mc_ref[...], ms_ref[...]
    nc, ns = nc_ref[...], ns_ref[...]
    cq = _rms(zs(Z2_CQ, MLA_Q_LORA), qn_ref[...]).astype(BF16)
    ckv = _rms(zs(Z2_CKV, MLA_KV_LORA), kvn_ref[...]).astype(BF16)
    kr = zs(Z2_KR, LANE)
    q = _dot(cq, wuq_ref[...])
    kn = _dot(ckv, wuk_ref[...])
    v = _dot(ckv, wuv_ref[...])
    vlane = lax.broadcasted_iota(jnp.int32, (v.shape[0], LANE), 1)
    q_scale = 1.0 / math.sqrt(MLA_QK)
    for h in range(MLA_HEADS):
        sl = slice(h * LANE, (h + 1) * LANE)
        vm_ref[0, :, sl] = jnp.where(vlane < MLA_V, v[:, sl], 1.0).astype(BF16)
        qm_ref[0, :, sl] = _norm_rope_mla(q[:, sl], gq_ref[...], mc, ms, q_scale).astype(BF16)
        km_ref[0, :, sl] = _norm_rope_mla(kn[:, sl] + kr, gk_ref[...], mc, ms, 1.0).astype(BF16)
    nq_scale = 1.0 / math.sqrt(NSA_HD)
    for b in range(NSA_HEADS * NSA_HD // LANE):
        sl = slice(b * LANE, (b + 1) * LANE)
        qn_out[0, :, sl] = _norm_rope_pair(zs(Z2_NQ + b * LANE, LANE), nqg_ref[...], nc, ns,
                                           nq_scale).astype(BF16)
    for b in range(2):
        sl = slice(b * LANE, (b + 1) * LANE)
        sk_out[0, :, sl] = _norm_rope_pair(zs(Z2_SK + b * LANE, LANE), nkg_ref[...], nc, ns, 1.0).astype(BF16)
        wk_out[0, :, sl] = _norm_rope_pair(zs(Z2_WK + b * LANE, LANE), nkg_ref[...], nc, ns, 1.0).astype(BF16)
    glane = lax.broadcasted_iota(jnp.int32, (z_ref.shape[1], 2 * LANE), 1) & (LANE - 1)
    sv_out[0] = jnp.where(glane < NSA_HD, zs(Z2_SV, 2 * LANE), 1.0).astype(BF16)
    wv_out[0] = jnp.where(glane < NSA_HD, zs(Z2_WV, 2 * LANE), 1.0).astype(BF16)
    gate_out[0] = jax.nn.sigmoid(zs(Z2_GATE, 2 * LANE))


def _prep(z2, w, tabs):
    B, S, _ = z2.shape
    ts = PREP_TS
    full = lambda shape: pl.BlockSpec(shape, lambda b, i: (0,) * len(shape))
    tab = pl.BlockSpec((ts, LANE), lambda b, i: (i, 0))
    tok = lambda width: pl.BlockSpec((1, ts, width), lambda b, i: (b, i, 0))
    out_w = (MLA_HEADS * LANE, MLA_HEADS * LANE, MLA_HEADS * LANE, NSA_HEADS * NSA_HD,
             2 * LANE, 2 * LANE, 2 * LANE, 2 * LANE, 2 * LANE)
    out_dt = (BF16,) * 8 + (F32,)
    return pl.pallas_call(
        _prep_kernel,
        grid=(B, S // ts),
        in_specs=[
            tok(Z2_W),
            full((1, MLA_Q_LORA)), full((1, MLA_KV_LORA)),
            full((MLA_Q_LORA, MLA_HEADS * LANE)), full((MLA_KV_LORA, MLA_HEADS * LANE)),
            full((MLA_KV_LORA, MLA_HEADS * LANE)),
            full((1, LANE)), full((1, LANE)),
            tab, tab,
            full((1, LANE)), full((1, LANE)),
            tab, tab,
        ],
        out_specs=[tok(wd) for wd in out_w],
        out_shape=[jax.ShapeDtypeStruct((B, S, wd), dt) for wd, dt in zip(out_w, out_dt)],
        compiler_params=_cparams(("parallel", "parallel")),
        name="prep",
    )(z2, w["mla_q_norm"], w["mla_kv_norm"], w["wuq"], w["wuk"], w["wuv"], w["mla_gq"], w["mla_gk"],
      tabs["mla_c"], tabs["mla_s"], w["nsa_qg"], w["nsa_kg"], tabs["nsa_c"], tabs["nsa_s"])


def _compress_kernel(ck_ref, cv_ref, pek_ref, pev_ref, w1kl_ref, w1kh_ref, w2k_ref,
                     w1vl_ref, w1vh_ref, w2v_ref, kg_ref, c_ref, s_ref,
                     kc_ref, vc_ref, *, nchunk):
    def branch(t_ref, pe_ref, w1l_ref, w1h_ref, w2_ref):
        lo = jnp.zeros((nchunk, 2 * CMP_HIDDEN), F32)
        hi = jnp.zeros((nchunk, 2 * CMP_HIDDEN), F32)
        for l in range(CMP_STRIDE):
            x = t_ref[0, pl.ds(l, nchunk, stride=CMP_STRIDE), :]
            lo = lo + _dot((x + pe_ref[l:l + 1, :]).astype(BF16), w1l_ref[l])
            hi = hi + _dot((x + pe_ref[CMP_STRIDE + l:CMP_STRIDE + l + 1, :]).astype(BF16), w1h_ref[l])
        pre = lo + pltpu.roll(hi, nchunk - 1, 0)
        return _dot(jax.nn.gelu(pre).astype(BF16), w2_ref[...])

    kc = branch(ck_ref, pek_ref, w1kl_ref, w1kh_ref, w2k_ref)
    vc = branch(cv_ref, pev_ref, w1vl_ref, w1vh_ref, w2v_ref)
    for b in range(2):
        sl = slice(b * LANE, (b + 1) * LANE)
        kc_ref[0, :, sl] = _norm_rope_pair(kc[:, sl], kg_ref[...], c_ref[...], s_ref[...], 1.0).astype(BF16)
    vc_ref[0] = vc.astype(BF16)


def _compress(z2, w, tabs):
    B, S, _ = z2.shape
    nchunk = S // CMP_STRIDE
    full = lambda shape: pl.BlockSpec(shape, lambda b: (0,) * len(shape))
    tab = full((nchunk, LANE))
    return pl.pallas_call(
        functools.partial(_compress_kernel, nchunk=nchunk),
        grid=(B,),
        in_specs=[
            pl.BlockSpec((1, S, LANE), lambda b: (b, 0, Z2_CK // LANE)),
            pl.BlockSpec((1, S, LANE), lambda b: (b, 0, Z2_CV // LANE)),
            full((CMP_LEN, LANE)), full((CMP_LEN, LANE)),
            full((CMP_STRIDE, LANE, 2 * CMP_HIDDEN)), full((CMP_STRIDE, LANE, 2 * CMP_HIDDEN)),
            full((2 * CMP_HIDDEN, 2 * LANE)),
            full((CMP_STRIDE, LANE, 2 * CMP_HIDDEN)), full((CMP_STRIDE, LANE, 2 * CMP_HIDDEN)),
            full((2 * CMP_HIDDEN, 2 * LANE)),
            full((1, LANE)), tab, tab,
        ],
        out_specs=[pl.BlockSpec((1, nchunk, 2 * LANE), lambda b: (b, 0, 0))] * 2,
        out_shape=[jax.ShapeDtypeStruct((B, nchunk, 2 * LANE), BF16)] * 2,
        compiler_params=_cparams(("parallel",)),
        name="compress",
    )(z2, z2, w["pe_k"], w["pe_v"], w["w1k_lo"], w["w1k_hi"], w["w2k"],
      w["w1v_lo"], w["w1v_hi"], w["w2v"], w["nsa_kg"], tabs["cmp_c"], tabs["cmp_s"])


def _sb_kernel(q_ref, k_ref, v_ref, o_ref, acc_ref, carry_ref, *, t, nh):
    i = pl.program_id(2)
    w = nh * SB_HD
    lane = lax.broadcasted_iota(jnp.int32, (t, w), 1)
    row = lax.broadcasted_iota(jnp.int32, (t, t), 0)
    col = lax.broadcasted_iota(jnp.int32, (t, t), 1)
    upper = jnp.where(row > col, 1.0, 0.0).astype(BF16)
    strict = (lax.broadcasted_iota(jnp.int32, (nh * t, t), 1)
              < (lax.broadcasted_iota(jnp.int32, (nh * t, t), 0) & (t - 1)))
    q = q_ref[0]
    own = [(lane >= SB_HD * h) & (lane < SB_HD * (h + 1)) for h in range(nh)]
    qs = jnp.concatenate([jnp.where(own[h], q, jnp.zeros_like(q)) for h in range(nh)], axis=0)

    def tile(j, carry, masked):
        off = pl.multiple_of(j * t, t)
        z = _split_rows(_dot_nt, qs, k_ref[0, pl.ds(off, t), :])
        lg = jnp.log(1.0 + jnp.exp(-jnp.abs(z)))
        logsig = jnp.minimum(z, 0.0) - lg
        lom = logsig - z
        if masked:
            lom = jnp.where(strict, lom, 0.0)
        suffix = _split_rows(_dot, lom.astype(BF16), upper)
        wgt = jnp.exp(logsig + suffix + carry)
        if masked:
            wgt = jnp.where(strict, wgt, 0.0)
        pv = _split_rows(_dot, wgt.astype(BF16), v_ref[0, pl.ds(off, t), :])
        return carry + jnp.sum(lom, axis=-1, keepdims=True), pv

    carry, pv = tile(i, jnp.zeros((nh * t, 1), F32), True)
    acc_ref[...] = pv
    odd = i % 2

    @pl.when(odd == 1)
    def _():
        c1, pv1 = tile(i - 1, carry, False)
        acc_ref[...] += pv1
        carry_ref[...] = c1

    @pl.when(odd == 0)
    def _():
        carry_ref[...] = carry

    def pair(s, c):
        j = i - 1 - odd - 2 * s
        c, pv_a = tile(j, c, False)
        c, pv_b = tile(j - 1, c, False)
        acc_ref[...] += pv_a + pv_b
        return c

    lax.fori_loop(0, i // 2, pair, carry_ref[...])
    out = acc_ref[0:t, :]
    for h in range(1, nh):
        out = jnp.where(own[h], acc_ref[h * t:(h + 1) * t, :], out)
    o_ref[0] = out.astype(o_ref.dtype)


def _sb_attention(zsb):
    B, S, _ = zsb.shape
    t = SB_T
    nh = SB_HEADS_PER_STEP
    w = nh * SB_HD
    npair = SB_W // w
    return pl.pallas_call(
        functools.partial(_sb_kernel, t=t, nh=nh),
        grid=(B, npair, S // t),
        in_specs=[
            pl.BlockSpec((1, t, w), lambda b, p, i: (b, i, p)),
            pl.BlockSpec((1, S, w), lambda b, p, i: (b, 0, npair + p)),
            pl.BlockSpec((1, S, w), lambda b, p, i: (b, 0, 2 * npair + p)),
        ],
        out_specs=pl.BlockSpec((1, t, w), lambda b, p, i: (b, i, p)),
        out_shape=jax.ShapeDtypeStruct((B, S, SB_W), BF16),
        scratch_shapes=[pltpu.VMEM((nh * t, w), F32), pltpu.VMEM((nh * t, 1), F32)],
        compiler_params=_cparams(("parallel", "parallel", "arbitrary")),
        name="sb_attn",
    )(zsb, zsb, zsb)


def _softmax_tile(s, vt, m, l, acc, mask):
    if mask is not None:
        s = jnp.where(mask, s, NEG_INF)
    m_new = jnp.maximum(m, jnp.max(s, axis=-1, keepdims=True))
    alpha = jnp.exp(m - m_new)
    p = jnp.exp(s - m_new)
    if mask is not None:
        p = jnp.where(mask, p, 0.0)
    l = alpha * l + jnp.sum(p, axis=-1, keepdims=True)
    acc = alpha * acc + _dot(p.astype(BF16), vt)
    return m_new, l, acc


def _flash_step(s, vt_ones, m, acc):
    m_new = jnp.maximum(m, jnp.max(s, axis=-1, keepdims=True))
    alpha = jnp.exp(m - m_new)
    p = jnp.exp(s - m_new).astype(BF16)
    return m_new, alpha * acc + _dot(p, vt_ones)


def _flash_finish(acc):
    return acc / jnp.maximum(pltpu.roll(acc, NSA_HD, 1), 1e-30)


def _lane_block_max(m, s):
    for c in range(s.shape[1] // LANE):
        m = jnp.maximum(m, s[:, c * LANE:(c + 1) * LANE])
    return m


def _mla_kernel(q_ref, k_ref, v_ref, o_ref, s_ref, *, tq, tk, nh):
    i = pl.program_id(2)
    lane = lax.broadcasted_iota(jnp.int32, (tq, LANE), 1)
    col_minus_row = (lax.broadcasted_iota(jnp.int32, (tq, tk), 1)
                     - lax.broadcasted_iota(jnp.int32, (tq, tk), 0))
    tail = (i * tq) // tk
    thr = i * tq - tail * tk
    heads = [slice(h * LANE, (h + 1) * LANE) for h in range(nh)]

    def score_tile(j, mx, masked):
        off = pl.multiple_of(j * tk, tk)
        out = []
        for h, sl in enumerate(heads):
            s = _dot_nt(q_ref[0, :, sl], k_ref[0, pl.ds(off, tk), sl])
            if masked:
                s = jnp.where(col_minus_row <= thr, s, NEG_INF)
            s_ref[h, j] = s
            out.append(_lane_block_max(mx[h], s))
        return tuple(out)

    mx = lax.fori_loop(0, tail, lambda j, mx: score_tile(j, mx, False),
                       (jnp.full((tq, LANE), NEG_INF, F32),) * nh)
    mx = score_tile(tail, mx, True)
    m = [jnp.max(mx[h], axis=-1, keepdims=True) for h in range(nh)]

    def absorb(j, acc):
        off = pl.multiple_of(j * tk, tk)
        return tuple(acc[h] + _dot(jnp.exp(s_ref[h, j] - m[h]).astype(BF16), v_ref[0, pl.ds(off, tk), sl])
                     for h, sl in enumerate(heads))

    acc = lax.fori_loop(0, tail + 1, absorb, (jnp.zeros((tq, LANE), F32),) * nh)
    for p in range(nh // 2):
        o = jnp.where(lane < MLA_V, _flash_finish(acc[2 * p]),
                      pltpu.roll(_flash_finish(acc[2 * p + 1]), MLA_V, 1))
        o_ref[0, :, p * LANE:(p + 1) * LANE] = o.astype(o_ref.dtype)


def _mla_attention(qm, km, vm):
    B, S, _ = qm.shape
    t = MLA_TQ
    nh = MLA_HEADS_PER_STEP
    ngrp = MLA_HEADS // nh
    return pl.pallas_call(
        functools.partial(_mla_kernel, tq=t, tk=MLA_TK, nh=nh),
        grid=(B, ngrp, S // t),
        in_specs=[
            pl.BlockSpec((1, t, nh * LANE), lambda b, p, i: (b, i, p)),
            pl.BlockSpec((1, S, nh * LANE), lambda b, p, i: (b, 0, p)),
            pl.BlockSpec((1, S, nh * LANE), lambda b, p, i: (b, 0, p)),
        ],
        out_specs=pl.BlockSpec((1, t, nh * MLA_V), lambda b, p, i: (b, i, p)),
        out_shape=jax.ShapeDtypeStruct((B, S, MLA_HEADS * MLA_V), BF16),
        scratch_shapes=[pltpu.VMEM((nh, S // MLA_TK, t, MLA_TK), F32)],
        compiler_params=_cparams(("parallel", "parallel", "arbitrary")),
        name="mla_attn",
    )(qm, km, vm)


def _stack_heads(q, lane):
    first = _first_head(lane)
    parts = []
    for h in range(NSA_REP):
        blk = q[:, (h // 2) * LANE:(h // 2 + 1) * LANE]
        parts.append(jnp.where(first if h % 2 == 0 else ~first, blk, jnp.zeros_like(blk)))
    return parts


def _pair_lanes(lane, a, b):
    return jnp.where(lane < NSA_HD, a, b)


def _tree_sum(terms):
    while len(terms) > 1:
        terms = [a + b for a, b in zip(terms[0::2], terms[1::2])] + (terms[-1:] if len(terms) % 2 else [])
    return terms[0]


def _nsa_cmp_kernel(q_ref, kc_ref, vc_ref, ovt_ref, ocmp_ref, bias_ref, *, t, n_slc):
    i = pl.program_id(1)
    gw = NSA_REP * NSA_HD
    lane = lax.broadcasted_iota(jnp.int32, (t, LANE), 1)
    row = lax.broadcasted_iota(jnp.int32, (t, LANE), 0)
    visible = lane * CMP_STRIDE + (CMP_LEN - 1) <= i * t + row
    blk = lax.broadcasted_iota(jnp.int32, (n_slc, t), 0)
    cur = (i * t + lax.broadcasted_iota(jnp.int32, (n_slc, t), 1)) // SLC_LEN
    forced = (blk == 0) | (blk == cur) | (blk == cur - 1)
    for g in range(NSA_GROUPS):
        kc = kc_ref[0, :, g * LANE:(g + 1) * LANE]
        vc = vc_ref[0, :, g * LANE:(g + 1) * LANE]
        psum = jnp.zeros((t, LANE), F32)
        outs = []
        for qh in _stack_heads(q_ref[0, :, g * gw:(g + 1) * gw], lane):
            s = jnp.where(visible, _dot_nt(qh, kc), NEG_INF)
            m = jnp.max(s, axis=-1, keepdims=True)
            e = jnp.where(visible, jnp.exp(s - m), 0.0)
            p = e / jnp.maximum(jnp.sum(e, axis=-1, keepdims=True), 1e-30)
            psum = psum + p
            outs.append(_dot(p.astype(BF16), vc))
        ocmp_ref[0, :, g * gw:g * gw + LANE] = _pair_lanes(lane, outs[0], outs[1])
        ocmp_ref[0, :, g * gw + LANE:(g + 1) * gw] = _pair_lanes(lane, outs[2], outs[3])
        hi = psum.astype(BF16)
        lo = (psum - hi.astype(F32)).astype(BF16)
        imp = (_dot_nt(ovt_ref[...], hi) + _dot_nt(ovt_ref[...], lo))[:n_slc]
        score = jnp.where(blk > cur, -1.0, jnp.where(forced, FORCE_SCORE, imp))
        terms = []
        for c in range(n_slc):
            sc = score[c:c + 1, :]
            before = (sc > score) | ((sc == score) & (blk > c))
            terms.append(jnp.where(before, 1.0, 0.0))
        bias_t = jnp.where(_tree_sum(terms) < float(min(SLC_TOPK, n_slc)), 0.0, NEG_INF)
        bias_t = jnp.concatenate([bias_t, jnp.full((LANE - n_slc, t), NEG_INF, F32)], axis=0)
        bias_ref[0, g] = bias_t.T.astype(BF16)


def _nsa_cmp(qn, kc, vc, overlap):
    B, S, _ = qn.shape
    t = CMP_T
    nkv = NSA_GROUPS * LANE
    return pl.pallas_call(
        functools.partial(_nsa_cmp_kernel, t=t, n_slc=S // SLC_LEN),
        grid=(B, S // t),
        in_specs=[
            pl.BlockSpec((1, t, NSA_HEADS * NSA_HD), lambda b, i: (b, i, 0)),
            pl.BlockSpec((1, kc.shape[1], nkv), lambda b, i: (b, 0, 0)),
            pl.BlockSpec((1, vc.shape[1], nkv), lambda b, i: (b, 0, 0)),
            pl.BlockSpec((LANE, LANE), lambda b, i: (0, 0)),
        ],
        out_specs=[pl.BlockSpec((1, t, NSA_HEADS * NSA_HD), lambda b, i: (b, i, 0)),
                   pl.BlockSpec((1, NSA_GROUPS, t, LANE), lambda b, i: (b, 0, i, 0))],
        out_shape=[jax.ShapeDtypeStruct((B, S, NSA_HEADS * NSA_HD), F32),
                   jax.ShapeDtypeStruct((B, NSA_GROUPS, S, LANE), BF16)],
        compiler_params=_cparams(("parallel", "parallel")),
        name="nsa_cmp",
    )(qn, kc, vc, overlap)


def _nsa_attn_kernel(q_ref, sk_ref, sv_ref, wk_ref, wv_ref, bias_ref, blk_ref, ocmp_ref, gate_ref,
                     o_ref, s_ref, mix_ref, *, t, tk):
    i = pl.program_id(1)
    r4 = NSA_REP * t
    gw = NSA_REP * NSA_HD
    groups = range(NSA_GROUPS)
    lane = lax.broadcasted_iota(jnp.int32, (t, LANE), 1)
    gl = [slice(g * LANE, (g + 1) * LANE) for g in groups]
    q4 = [jnp.concatenate(_stack_heads(q_ref[0, :, g * gw:(g + 1) * gw], lane), axis=0) for g in groups]
    q4b = [jnp.concatenate([q4[g], jnp.concatenate([bias_ref[0, g]] * NSA_REP, axis=0)], axis=1)
           for g in groups]

    rmc = ((lax.broadcasted_iota(jnp.int32, (r4, t), 0) & (t - 1))
           - lax.broadcasted_iota(jnp.int32, (r4, t), 1))

    def mask_blocks(s, visible):
        return jnp.concatenate([jnp.where(visible(c), s[:, c * t:(c + 1) * t], NEG_INF)
                                for c in range(s.shape[1] // t)], axis=1)

    tail = (i * t) // tk
    thr = i * t - tail * tk

    def score_tile(j, mx, masked):
        off = pl.multiple_of(j * tk, tk)
        onehot = blk_ref[pl.ds(off, tk), :]
        out = []
        for g in groups:
            s = _dot_nt(q4b[g], jnp.concatenate([sk_ref[0, pl.ds(off, tk), gl[g]], onehot], axis=1))
            if masked:
                s = mask_blocks(s, lambda c: rmc >= c * t - thr)
            s_ref[g, j] = s
            out.append(_lane_block_max(mx[g], s))
        return tuple(out)

    mx = lax.fori_loop(0, tail, lambda j, mx: score_tile(j, mx, False),
                       (jnp.full((r4, LANE), NEG_INF, F32),) * NSA_GROUPS)
    mx = score_tile(tail, mx, True)
    m_s = [jnp.max(mx[g], axis=-1, keepdims=True) for g in groups]

    def gate_pair(g, br, p2):
        c0 = g * LANE + br * NSA_REP + 2 * p2
        return _pair_lanes(lane, gate_ref[0, :, c0:c0 + 1], gate_ref[0, :, c0 + 1:c0 + 2])

    def head_pair(o, p2):
        a, b = 2 * p2, 2 * p2 + 1
        return _pair_lanes(lane, o[a * t:(a + 1) * t], pltpu.roll(o[b * t:(b + 1) * t], NSA_HD, 1))

    wkeys = WIN + t
    start = pl.multiple_of(jnp.maximum(i * t - WIN, 0), t)
    lead = i * t - start

    def in_window(c):
        return pltpu.bitcast(rmc + (lead - c * t), jnp.uint32) < jnp.uint32(WIN)

    for g in groups:
        s = mask_blocks(_dot_nt(q4[g], wk_ref[0, pl.ds(start, wkeys), gl[g]]), in_window)
        p = jnp.exp(s - jnp.max(s, axis=-1, keepdims=True)).astype(BF16)
        o_w = _flash_finish(_dot(p, wv_ref[0, pl.ds(start, wkeys), gl[g]]))
        for p2 in range(NSA_REP // 2):
            cols = slice(g * gw + p2 * LANE, g * gw + (p2 + 1) * LANE)
            mix_ref[:, cols] = gate_pair(g, 0, p2) * ocmp_ref[0, :, cols] + gate_pair(g, 2, p2) * head_pair(o_w, p2)

    def absorb(j, acc):
        off = pl.multiple_of(j * tk, tk)
        return tuple(acc[g] + _dot(jnp.exp(s_ref[g, j] - m_s[g]).astype(BF16), sv_ref[0, pl.ds(off, tk), gl[g]])
                     for g in groups)

    acc_s = lax.fori_loop(0, tail + 1, absorb, (jnp.zeros((r4, LANE), F32),) * NSA_GROUPS)
    for g in groups:
        o_s = _flash_finish(acc_s[g])
        for p2 in range(NSA_REP // 2):
            cols = slice(g * gw + p2 * LANE, g * gw + (p2 + 1) * LANE)
            o_ref[0, :, cols] = (mix_ref[:, cols] + gate_pair(g, 1, p2) * head_pair(o_s, p2)).astype(o_ref.dtype)


def _nsa_attention(qn, sk, sv, wk, wv, bias, blk_onehot, ocmp, gates):
    B, S, _ = qn.shape
    t = ATT_T
    hw = NSA_HEADS * NSA_HD
    nkv = NSA_GROUPS * LANE
    kv = pl.BlockSpec((1, S, nkv), lambda b, i: (b, 0, 0))
    return pl.pallas_call(
        functools.partial(_nsa_attn_kernel, t=t, tk=NSA_TK),
        grid=(B, S // t),
        in_specs=[
            pl.BlockSpec((1, t, hw), lambda b, i: (b, i, 0)),
            kv, kv, kv, kv,
            pl.BlockSpec((1, NSA_GROUPS, t, LANE), lambda b, i: (b, 0, i, 0)),
            pl.BlockSpec((S, LANE), lambda b, i: (0, 0)),
            pl.BlockSpec((1, t, hw), lambda b, i: (b, i, 0)),
            pl.BlockSpec((1, t, nkv), lambda b, i: (b, i, 0)),
        ],
        out_specs=pl.BlockSpec((1, t, hw), lambda b, i: (b, i, 0)),
        out_shape=jax.ShapeDtypeStruct((B, S, hw), BF16),
        scratch_shapes=[pltpu.VMEM((NSA_GROUPS, S // NSA_TK, NSA_REP * t, NSA_TK), F32),
                        pltpu.VMEM((t, hw), F32)],
        compiler_params=_cparams(("parallel", "arbitrary")),
        name="nsa_attn",
    )(qn, sk, sv, wk, wv, bias, blk_onehot, ocmp, gates)


def _merge_kernel(x_ref, g_ref, osb_ref, omla_ref, onsa_ref, wg_ref, psb_ref, pmla_ref, pnsa_ref,
                  wout_ref, o_ref):
    x = x_ref[...]
    h = _rms(x, g_ref[...]).astype(BF16)
    y = jnp.zeros(x.shape, F32)
    for br, (o_r, p_r) in enumerate(((osb_ref, psb_ref), (omla_ref, pmla_ref), (onsa_ref, pnsa_ref))):
        gate = jax.nn.sigmoid(_dot(h, wg_ref[:, br * D_MODEL:(br + 1) * D_MODEL]))
        y = y + gate * _dot(o_r[...], p_r[...])
    o_ref[...] = x + _dot(y.astype(BF16), wout_ref[...])


def _merge(x, g, osb, omla, onsa, w):
    T, D = x.shape
    tm = MERGE_TM
    tok = lambda width: pl.BlockSpec((tm, width), lambda i: (i, 0))
    full = lambda shape: pl.BlockSpec(shape, lambda i: (0,) * len(shape))
    return pl.pallas_call(
        _merge_kernel,
        grid=(T // tm,),
        in_specs=[tok(D), full((1, D)), tok(osb.shape[1]), tok(omla.shape[1]), tok(onsa.shape[1]),
                  full((D, N_BRANCH * D)), full(w["proj_sb"].shape), full(w["proj_mla"].shape),
                  full(w["proj_nsa"].shape), full((D, D))],
        out_specs=tok(D),
        out_shape=jax.ShapeDtypeStruct((T, D), F32),
        compiler_params=_cparams(("parallel",)),
        name="merge",
    )(x, g, osb, omla, onsa, w["w_gate"], w["proj_sb"], w["proj_mla"], w["proj_nsa"], w["w_out"])


def _cos_sin(pos, d):
    half = d // 2
    inv = jnp.exp(-math.log(ROPE_THETA) * jnp.arange(half, dtype=F32) * (2.0 / d))
    ang = pos.astype(F32)[:, None] * inv[None, :]
    return jnp.cos(ang), jnp.sin(ang)


def _pair_rope_tables(pos):
    cos, sin = _cos_sin(pos, NSA_HD)
    return jnp.concatenate([cos] * 4, axis=1), jnp.concatenate([-sin, -sin, sin, sin], axis=1)


def _mla_rope_tables(pos):
    cos, sin = _cos_sin(pos, MLA_ROPE)
    n, half = cos.shape
    one = lambda k: jnp.ones((n, k), F32)
    zero = lambda k: jnp.zeros((n, k), F32)
    rest = LANE // 2 - half
    c = jnp.concatenate([cos, one(rest), cos, one(MLA_NOPE - rest), zero(LANE - MLA_QK)], axis=1)
    s = jnp.concatenate([-sin, zero(rest), sin, zero(LANE // 2 - half)], axis=1)
    return c, s


def _perm_pair(a, b):
    h = NSA_HD // 2
    return jnp.concatenate([a[..., :h], b[..., :h], a[..., h:], b[..., h:]], axis=-1)


def _perm_mla(nope, rope):
    h = MLA_ROPE // 2
    rest = LANE // 2 - h
    pad = jnp.zeros(nope.shape[:-1] + (LANE - MLA_QK,), nope.dtype)
    return jnp.concatenate([rope[..., :h], nope[..., :rest], rope[..., h:], nope[..., rest:], pad], axis=-1)


def _tables(S):
    pos = jnp.arange(S, dtype=jnp.int32)
    nchunk = S // CMP_STRIDE
    ends = jnp.arange(nchunk, dtype=jnp.int32) * CMP_STRIDE + (CMP_LEN - 1)
    t = {}
    t["nsa_c"], t["nsa_s"] = _pair_rope_tables(pos)
    t["cmp_c"], t["cmp_s"] = _pair_rope_tables(ends)
    t["mla_c"], t["mla_s"] = _mla_rope_tables(pos)
    t["blk_onehot"] = (pos[:, None] // SLC_LEN == jnp.arange(LANE, dtype=jnp.int32)[None, :]).astype(BF16)
    n_cmp = (S - CMP_LEN) // CMP_STRIDE + 1
    n_slc = S // SLC_LEN
    c0 = np.arange(n_cmp)[:, None] * CMP_STRIDE
    s0 = np.arange(n_slc)[None, :] * SLC_LEN
    ov = np.clip(np.minimum(c0 + CMP_LEN, s0 + SLC_LEN) - np.maximum(c0, s0), 0, None) / CMP_LEN
    full = np.zeros((LANE, LANE), np.float32)
    full[:n_slc, :n_cmp] = ov.T
    t["overlap"] = jnp.asarray(full, BF16)
    return t


def _pad_cols(w, width):
    return jnp.pad(w, ((0, 0), (0, width - w.shape[1])))


def _dup_groups(w, permuted):
    parts = []
    for g in range(NSA_GROUPS):
        blk = w[:, g * NSA_HD:(g + 1) * NSA_HD]
        parts.append(_perm_pair(blk, blk) if permuted else jnp.concatenate([blk, blk], axis=1))
    return jnp.concatenate(parts, axis=1)


def _blockdiag2(a):
    z = jnp.zeros_like(a)
    return jnp.concatenate([jnp.concatenate([a, z], axis=1), jnp.concatenate([z, a], axis=1)], axis=0)


def _layer_weights(l, p):
    w = {}
    row = lambda v: v[l][None, :].astype(F32)
    offs = np.concatenate([[0], np.cumsum(IN_SIZES)])
    cols = [p["w_in"][l][:, offs[k]:offs[k + 1]] for k in range(len(IN_SIZES))]
    (sb_q, sb_k, sb_v, cq, ckv, kr, nq, ck, cv, sk, sv, wk, wv, ngate, mgate) = cols
    sb_scale = 1.0 / math.sqrt(SB_HD)
    w["w_sb"] = jnp.concatenate([sb_q * sb_scale, sb_k, sb_v], axis=1).astype(BF16)
    kr_blk = _perm_mla(jnp.zeros((D_MODEL, MLA_NOPE), kr.dtype), kr)
    nq = jnp.concatenate([_perm_pair(nq[:, b * LANE:b * LANE + NSA_HD], nq[:, b * LANE + NSA_HD:(b + 1) * LANE])
                          for b in range(NSA_HEADS * NSA_HD // LANE)], axis=1)
    ng = ngate.reshape(D_MODEL, N_BRANCH, NSA_GROUPS, NSA_REP)
    ng = jnp.concatenate([_pad_cols(ng[:, :, g, :].reshape(D_MODEL, N_BRANCH * NSA_REP), LANE)
                          for g in range(NSA_GROUPS)], axis=1)
    w["w_z2"] = jnp.concatenate([cq, ckv, kr_blk, nq, ck, cv, _dup_groups(sk, True), _dup_groups(sv, False),
                                 _dup_groups(wk, True), _dup_groups(wv, False), ng], axis=1).astype(BF16)
    w["w_gate"] = mgate.astype(BF16)
    w["mix_norm"] = row(p["mix_norm"])
    w["mla_q_norm"] = row(p["mla_q_norm"])
    w["mla_kv_norm"] = row(p["mla_kv_norm"])
    wuq = p["mla_w_uq"][l].reshape(MLA_Q_LORA, MLA_HEADS, MLA_QK)
    w["wuq"] = _perm_mla(wuq[:, :, :MLA_NOPE], wuq[:, :, MLA_NOPE:]).reshape(MLA_Q_LORA, -1).astype(BF16)
    wukv = p["mla_w_ukv"][l].reshape(MLA_KV_LORA, MLA_HEADS, MLA_NOPE + MLA_V)
    no_rope = jnp.zeros((MLA_KV_LORA, MLA_HEADS, MLA_ROPE), wukv.dtype)
    w["wuk"] = _perm_mla(wukv[:, :, :MLA_NOPE], no_rope).reshape(MLA_KV_LORA, -1).astype(BF16)
    w["wuv"] = jnp.pad(wukv[:, :, MLA_NOPE:], ((0, 0), (0, 0), (0, LANE - MLA_V))).reshape(MLA_KV_LORA, -1).astype(BF16)
    gq, gk = row(p["mla_qk_gain_q"]), row(p["mla_qk_gain_k"])
    w["mla_gq"] = _perm_mla(gq[:, :MLA_NOPE], gq[:, MLA_NOPE:])
    w["mla_gk"] = _perm_mla(gk[:, :MLA_NOPE], gk[:, MLA_NOPE:])
    w["nsa_qg"] = _perm_pair(row(p["nsa_q_gain"]), row(p["nsa_q_gain"]))
    w["nsa_kg"] = _perm_pair(row(p["nsa_k_gain"]), row(p["nsa_k_gain"]))
    w["pe_k"] = jnp.tile(p["cmp_pos_k"][l], (1, 2)).astype(F32)
    w["pe_v"] = jnp.tile(p["cmp_pos_v"][l], (1, 2)).astype(F32)
    for nm, w1, w2 in (("k", p["cmp_wk1"][l], p["cmp_wk2"][l]), ("v", p["cmp_wv1"][l], p["cmp_wv2"][l])):
        w1 = w1.reshape(CMP_LEN, NSA_HD, CMP_HIDDEN)
        bd = jax.vmap(_blockdiag2)(w1).astype(BF16)
        w["w1%s_lo" % nm] = bd[:CMP_STRIDE]
        w["w1%s_hi" % nm] = bd[CMP_STRIDE:]
        w["w2" + nm] = _blockdiag2(_perm_pair(w2, w2) if nm == "k" else jnp.concatenate([w2, w2], axis=1)).astype(BF16)
    for nm in ("proj_sb", "proj_mla", "proj_nsa", "w_out"):
        w[nm] = p[nm][l].astype(BF16)
    for nm in ("ffn1", "ffn2"):
        w[nm + "_norm"] = row(p[nm + "_norm"])
    return w


def kernel(x, ffn1_norm, ffn1_wi, ffn1_wo, mix_norm, w_in, mla_q_norm, mla_w_uq, mla_kv_norm, mla_w_ukv, mla_qk_gain_q, mla_qk_gain_k, nsa_q_gain, nsa_k_gain, cmp_pos_k, cmp_pos_v, cmp_wk1, cmp_wk2, cmp_wv1, cmp_wv2, proj_sb, proj_mla, proj_nsa, w_out, ffn2_norm, ffn2_wi, ffn2_wo):
    p = dict(ffn1_norm=ffn1_norm, ffn1_wi=ffn1_wi, ffn1_wo=ffn1_wo, mix_norm=mix_norm, w_in=w_in,
             mla_q_norm=mla_q_norm, mla_w_uq=mla_w_uq, mla_kv_norm=mla_kv_norm, mla_w_ukv=mla_w_ukv,
             mla_qk_gain_q=mla_qk_gain_q, mla_qk_gain_k=mla_qk_gain_k, nsa_q_gain=nsa_q_gain,
             nsa_k_gain=nsa_k_gain, cmp_pos_k=cmp_pos_k, cmp_pos_v=cmp_pos_v, cmp_wk1=cmp_wk1,
             cmp_wk2=cmp_wk2, cmp_wv1=cmp_wv1, cmp_wv2=cmp_wv2, proj_sb=proj_sb, proj_mla=proj_mla,
             proj_nsa=proj_nsa, w_out=w_out, ffn2_norm=ffn2_norm, ffn2_wi=ffn2_wi, ffn2_wo=ffn2_wo)
    B, S, D = x.shape
    assert S // CMP_STRIDE == LANE and S // SLC_LEN <= LANE and D == D_MODEL
    tabs = _tables(S)
    xt = x.reshape(B * S, D)
    for l in range(ffn1_wi.shape[0]):
        w = _layer_weights(l, p)
        xt = _ffn(xt, w["ffn1_norm"], ffn1_wi, ffn1_wo, l)
        zsb, z2 = _inproj(xt, w["mix_norm"], w["w_sb"], w["w_z2"])
        zsb = zsb.reshape(B, S, -1)
        z2 = z2.reshape(B, S, -1)
        qm, km, vm, qn, sk, sv, wk, wv, gates = _prep(z2, w, tabs)
        kc, vc = _compress(z2, w, tabs)
        o_sb = _sb_attention(zsb)
        o_mla = _mla_attention(qm, km, vm)
        o_cmp, bias = _nsa_cmp(qn, kc, vc, tabs["overlap"])
        o_nsa = _nsa_attention(qn, sk, sv, wk, wv, bias, tabs["blk_onehot"], o_cmp, gates)
        xt = _merge(xt, w["mix_norm"], o_sb.reshape(B * S, -1), o_mla.reshape(B * S, -1),
                    o_nsa.reshape(B * S, -1), w)
        xt = _ffn(xt, w["ffn2_norm"], ffn2_wi, ffn2_wo, l)
    return xt.reshape(B, S, D)
```

```python
import functools
import math

import numpy as np
import jax
import jax.numpy as jnp
from jax import lax
from jax.experimental import pallas as pl
from jax.experimental.pallas import tpu as pltpu

D_MODEL = 1024
D_FF = 2816
EPS = 1e-6
ROPE_THETA = 10000.0
N_BRANCH = 3
SB_HEADS = 8
SB_HD = 64
MLA_HEADS = 8
MLA_NOPE = 64
MLA_ROPE = 32
MLA_V = 64
MLA_Q_LORA = 256
MLA_KV_LORA = 128
MLA_QK = MLA_NOPE + MLA_ROPE
NSA_HEADS = 8
NSA_GROUPS = 2
NSA_REP = NSA_HEADS // NSA_GROUPS
NSA_HD = 64
CMP_LEN = 32
CMP_STRIDE = 16
CMP_HIDDEN = 128
SLC_LEN = 64
SLC_TOPK = 16
WIN = 512
FORCE_SCORE = 1e3
NEG_INF = -1e30
SB_W = SB_HEADS * SB_HD
NSA_KV_W = NSA_GROUPS * NSA_HD
IN_SIZES = (SB_W, SB_W, SB_W, MLA_Q_LORA, MLA_KV_LORA, MLA_ROPE, NSA_HEADS * NSA_HD,
            NSA_KV_W, NSA_KV_W, NSA_KV_W, NSA_KV_W, NSA_KV_W, NSA_KV_W,
            N_BRANCH * NSA_HEADS, N_BRANCH * D_MODEL)

LANE = 128
V7X_VMEM_LIMIT = 56 * 1024 * 1024

FFN_TM = 1024
FFN_TF = 256
PROJ_TM = 512
PREP_TS = 512
ATT_T = 128
NSA_TK = 512
CMP_T = 256
SB_T = 256
SB_HEADS_PER_STEP = 4
MLA_TQ = 512
MLA_HEADS_PER_STEP = 4
MLA_TK = 512
MERGE_TM = 512

Z2_CQ = 0
Z2_CKV = 256
Z2_KR = 384
Z2_NQ = 512
Z2_CK = 1024
Z2_CV = 1152
Z2_SK = 1280
Z2_SV = 1536
Z2_WK = 1792
Z2_WV = 2048
Z2_GATE = 2304
Z2_W = 2560

BF16 = jnp.bfloat16
F32 = jnp.float32


def _cparams(sem, vmem=V7X_VMEM_LIMIT):
    return pltpu.CompilerParams(dimension_semantics=sem, vmem_limit_bytes=vmem)


def _dot(a, b):
    return jnp.dot(a, b, preferred_element_type=F32)


def _dot_nt(a, b):
    return lax.dot_general(a, b, (((1,), (1,)), ((), ())), preferred_element_type=F32)


def _split_rows(dot, a, b):
    h = a.shape[0] // 2
    return jnp.concatenate([dot(a[:h], b), dot(a[h:], b)], axis=0)


def _rms(x, g):
    y = x * lax.rsqrt(jnp.mean(x * x, axis=-1, keepdims=True) + EPS)
    return y * g


def _ffn_kernel(x_ref, g_ref, wa_ref, wb_ref, wo_ref, o_ref, h_ref, acc_ref):
    j = pl.program_id(1)

    @pl.when(j == 0)
    def _():
        h_ref[...] = _rms(x_ref[...], g_ref[...]).astype(BF16)
        acc_ref[...] = jnp.zeros_like(acc_ref)

    h = h_ref[...]
    a = _dot(h, wa_ref[...].astype(BF16))
    b = _dot(h, wb_ref[...].astype(BF16))
    u = (jax.nn.silu(a) * b).astype(BF16)
    acc_ref[...] += _dot(u, wo_ref[...].astype(BF16))

    @pl.when(j == pl.num_programs(1) - 1)
    def _():
        o_ref[...] = x_ref[...] + 0.5 * acc_ref[...]


def _ffn(x, g, wi, wo, l):
    T, D = x.shape
    F = wo.shape[1]
    tm, tf = FFN_TM, FFN_TF
    nf = F // tf
    return pl.pallas_call(
        _ffn_kernel,
        grid=(T // tm, nf),
        in_specs=[
            pl.BlockSpec((tm, D), lambda i, j: (i, 0)),
            pl.BlockSpec((1, D), lambda i, j: (0, 0)),
            pl.BlockSpec((None, D, tf), lambda i, j: (l, 0, j)),
            pl.BlockSpec((None, D, tf), lambda i, j: (l, 0, j + nf)),
            pl.BlockSpec((None, tf, D), lambda i, j: (l, j, 0)),
        ],
        out_specs=pl.BlockSpec((tm, D), lambda i, j: (i, 0)),
        out_shape=jax.ShapeDtypeStruct((T, D), F32),
        scratch_shapes=[pltpu.VMEM((tm, D), BF16), pltpu.VMEM((tm, D), F32)],
        compiler_params=_cparams(("parallel", "arbitrary")),
        name="ffn",
    )(x, g, wi, wi, wo)


def _rope_mix(y, c, s):
    return y * c + pltpu.roll(y, LANE // 2, 1) * s


def _first_head(lane):
    return (lane & (NSA_HD // 2)) == 0


def _norm_rope_pair(x, gain, c, s, scale):
    first = _first_head(lax.broadcasted_iota(jnp.int32, x.shape, 1))
    sq = x * x
    ss_a = jnp.sum(jnp.where(first, sq, 0.0), axis=-1, keepdims=True)
    ss_b = jnp.sum(jnp.where(first, 0.0, sq), axis=-1, keepdims=True)
    r = jnp.where(first, lax.rsqrt(ss_a * (1.0 / NSA_HD) + EPS), lax.rsqrt(ss_b * (1.0 / NSA_HD) + EPS))
    y = _rope_mix(x * r * gain, c, s)
    return y * scale if scale != 1.0 else y


def _norm_rope_mla(x, gain, c, s, scale):
    ss = jnp.sum(x * x, axis=-1, keepdims=True)
    y = _rope_mix(x * lax.rsqrt(ss * (1.0 / MLA_QK) + EPS) * gain, c, s)
    return y * scale if scale != 1.0 else y


def _inproj_kernel(x_ref, g_ref, wa_ref, wb_ref, oa_ref, ob_ref):
    h = _rms(x_ref[...], g_ref[...]).astype(BF16)
    oa_ref[...] = _dot(h, wa_ref[...]).astype(oa_ref.dtype)
    ob_ref[...] = _dot(h, wb_ref[...]).astype(ob_ref.dtype)


def _inproj(x, g, wa, wb):
    T, D = x.shape
    tm = PROJ_TM
    na, nb = wa.shape[1], wb.shape[1]
    return pl.pallas_call(
        _inproj_kernel,
        grid=(T // tm,),
        in_specs=[
            pl.BlockSpec((tm, D), lambda i: (i, 0)),
            pl.BlockSpec((1, D), lambda i: (0, 0)),
            pl.BlockSpec((D, na), lambda i: (0, 0)),
            pl.BlockSpec((D, nb), lambda i: (0, 0)),
        ],
        out_specs=[pl.BlockSpec((tm, na), lambda i: (i, 0)),
                   pl.BlockSpec((tm, nb), lambda i: (i, 0))],
        out_shape=[jax.ShapeDtypeStruct((T, na), BF16), jax.ShapeDtypeStruct((T, nb), F32)],
        compiler_params=_cparams(("parallel",)),
        name="inproj",
    )(x, g, wa, wb)


def _prep_kernel(z_ref, qn_ref, kvn_ref, wuq_ref, wuk_ref, wuv_ref, gq_ref, gk_ref,
                 mc_ref, ms_ref, nqg_ref, nkg_ref, nc_ref, ns_ref,
                 qm_ref, km_ref, vm_ref, qn_out, sk_out, sv_out, wk_out, wv_out, gate_out):
    zs = lambda start, width: z_ref[0, :, start:start + width]
    mc, ms = mc_ref[...], ms_ref[...]
    nc, ns = nc_ref[...], ns_ref[...]
    cq = _rms(zs(Z2_CQ, MLA_Q_LORA), qn_ref[...]).astype(BF16)
    ckv = _rms(zs(Z2_CKV, MLA_KV_LORA), kvn_ref[...]).astype(BF16)
    kr = zs(Z2_KR, LANE)
    q = _dot(cq, wuq_ref[...])
    kn = _dot(ckv, wuk_ref[...])
    v = _dot(ckv, wuv_ref[...])
    vlane = lax.broadcasted_iota(jnp.int32, (v.shape[0], LANE), 1)
    q_scale = 1.0 / math.sqrt(MLA_QK)
    for h in range(MLA_HEADS):
        sl = slice(h * LANE, (h + 1) * LANE)
        vm_ref[0, :, sl] = jnp.where(vlane < MLA_V, v[:, sl], 1.0).astype(BF16)
        qm_ref[0, :, sl] = _norm_rope_mla(q[:, sl], gq_ref[...], mc, ms, q_scale).astype(BF16)
        km_ref[0, :, sl] = _norm_rope_mla(kn[:, sl] + kr, gk_ref[...], mc, ms, 1.0).astype(BF16)
    nq_scale = 1.0 / math.sqrt(NSA_HD)
    for b in range(NSA_HEADS * NSA_HD // LANE):
        sl = slice(b * LANE, (b + 1) * LANE)
        qn_out[0, :, sl] = _norm_rope_pair(zs(Z2_NQ + b * LANE, LANE), nqg_ref[...], nc, ns,
                                           nq_scale).astype(BF16)
    for b in range(2):
        sl = slice(b * LANE, (b + 1) * LANE)
        sk_out[0, :, sl] = _norm_rope_pair(zs(Z2_SK + b * LANE, LANE), nkg_ref[...], nc, ns, 1.0).astype(BF16)
        wk_out[0, :, sl] = _norm_rope_pair(zs(Z2_WK + b * LANE, LANE), nkg_ref[...], nc, ns, 1.0).astype(BF16)
    glane = lax.broadcasted_iota(jnp.int32, (z_ref.shape[1], 2 * LANE), 1) & (LANE - 1)
    sv_out[0] = jnp.where(glane < NSA_HD, zs(Z2_SV, 2 * LANE), 1.0).astype(BF16)
    wv_out[0] = jnp.where(glane < NSA_HD, zs(Z2_WV, 2 * LANE), 1.0).astype(BF16)
    gate_out[0] = jax.nn.sigmoid(zs(Z2_GATE, 2 * LANE))


def _prep(z2, w, tabs):
    B, S, _ = z2.shape
    ts = PREP_TS
    full = lambda shape: pl.BlockSpec(shape, lambda b, i: (0,) * len(shape))
    tab = pl.BlockSpec((ts, LANE), lambda b, i: (i, 0))
    tok = lambda width: pl.BlockSpec((1, ts, width), lambda b, i: (b, i, 0))
    out_w = (MLA_HEADS * LANE, MLA_HEADS * LANE, MLA_HEADS * LANE, NSA_HEADS * NSA_HD,
             2 * LANE, 2 * LANE, 2 * LANE, 2 * LANE, 2 * LANE)
    out_dt = (BF16,) * 8 + (F32,)
    return pl.pallas_call(
        _prep_kernel,
        grid=(B, S // ts),
        in_specs=[
            tok(Z2_W),
            full((1, MLA_Q_LORA)), full((1, MLA_KV_LORA)),
            full((MLA_Q_LORA, MLA_HEADS * LANE)), full((MLA_KV_LORA, MLA_HEADS * LANE)),
            full((MLA_KV_LORA, MLA_HEADS * LANE)),
            full((1, LANE)), full((1, LANE)),
            tab, tab,
            full((1, LANE)), full((1, LANE)),
            tab, tab,
        ],
        out_specs=[tok(wd) for wd in out_w],
        out_shape=[jax.ShapeDtypeStruct((B, S, wd), dt) for wd, dt in zip(out_w, out_dt)],
        compiler_params=_cparams(("parallel", "parallel")),
        name="prep",
    )(z2, w["mla_q_norm"], w["mla_kv_norm"], w["wuq"], w["wuk"], w["wuv"], w["mla_gq"], w["mla_gk"],
      tabs["mla_c"], tabs["mla_s"], w["nsa_qg"], w["nsa_kg"], tabs["nsa_c"], tabs["nsa_s"])


def _compress_kernel(ck_ref, cv_ref, pek_ref, pev_ref, w1kl_ref, w1kh_ref, w2k_ref,
                     w1vl_ref, w1vh_ref, w2v_ref, kg_ref, c_ref, s_ref,
                     kc_ref, vc_ref, *, nchunk):
    def branch(t_ref, pe_ref, w1l_ref, w1h_ref, w2_ref):
        lo = jnp.zeros((nchunk, 2 * CMP_HIDDEN), F32)
        hi = jnp.zeros((nchunk, 2 * CMP_HIDDEN), F32)
        for l in range(CMP_STRIDE):
            x = t_ref[0, pl.ds(l, nchunk, stride=CMP_STRIDE), :]
            lo = lo + _dot((x + pe_ref[l:l + 1, :]).astype(BF16), w1l_ref[l])
            hi = hi + _dot((x + pe_ref[CMP_STRIDE + l:CMP_STRIDE + l + 1, :]).astype(BF16), w1h_ref[l])
        pre = lo + pltpu.roll(hi, nchunk - 1, 0)
        return _dot(jax.nn.gelu(pre).astype(BF16), w2_ref[...])

    kc = branch(ck_ref, pek_ref, w1kl_ref, w1kh_ref, w2k_ref)
    vc = branch(cv_ref, pev_ref, w1vl_ref, w1vh_ref, w2v_ref)
    for b in range(2):
        sl = slice(b * LANE, (b + 1) * LANE)
        kc_ref[0, :, sl] = _norm_rope_pair(kc[:, sl], kg_ref[...], c_ref[...], s_ref[...], 1.0).astype(BF16)
    vc_ref[0] = vc.astype(BF16)


def _compress(z2, w, tabs):
    B, S, _ = z2.shape
    nchunk = S // CMP_STRIDE
    full = lambda shape: pl.BlockSpec(shape, lambda b: (0,) * len(shape))
    tab = full((nchunk, LANE))
    return pl.pallas_call(
        functools.partial(_compress_kernel, nchunk=nchunk),
        grid=(B,),
        in_specs=[
            pl.BlockSpec((1, S, LANE), lambda b: (b, 0, Z2_CK // LANE)),
            pl.BlockSpec((1, S, LANE), lambda b: (b, 0, Z2_CV // LANE)),
            full((CMP_LEN, LANE)), full((CMP_LEN, LANE)),
            full((CMP_STRIDE, LANE, 2 * CMP_HIDDEN)), full((CMP_STRIDE, LANE, 2 * CMP_HIDDEN)),
            full((2 * CMP_HIDDEN, 2 * LANE)),
            full((CMP_STRIDE, LANE, 2 * CMP_HIDDEN)), full((CMP_STRIDE, LANE, 2 * CMP_HIDDEN)),
            full((2 * CMP_HIDDEN, 2 * LANE)),
            full((1, LANE)), tab, tab,
        ],
        out_specs=[pl.BlockSpec((1, nchunk, 2 * LANE), lambda b: (b, 0, 0))] * 2,
        out_shape=[jax.ShapeDtypeStruct((B, nchunk, 2 * LANE), BF16)] * 2,
        compiler_params=_cparams(("parallel",)),
        name="compress",
    )(z2, z2, w["pe_k"], w["pe_v"], w["w1k_lo"], w["w1k_hi"], w["w2k"],
      w["w1v_lo"], w["w1v_hi"], w["w2v"], w["nsa_kg"], tabs["cmp_c"], tabs["cmp_s"])


def _sb_kernel(q_ref, k_ref, v_ref, o_ref, acc_ref, carry_ref, *, t, nh):
    i = pl.program_id(2)
    w = nh * SB_HD
    lane = lax.broadcasted_iota(jnp.int32, (t, w), 1)
    row = lax.broadcasted_iota(jnp.int32, (t, t), 0)
    col = lax.broadcasted_iota(jnp.int32, (t, t), 1)
    upper = jnp.where(row > col, 1.0, 0.0).astype(BF16)
    strict = (lax.broadcasted_iota(jnp.int32, (nh * t, t), 1)
              < (lax.broadcasted_iota(jnp.int32, (nh * t, t), 0) & (t - 1)))
    q = q_ref[0]
    own = [(lane >= SB_HD * h) & (lane < SB_HD * (h + 1)) for h in range(nh)]
    qs = jnp.concatenate([jnp.where(own[h], q, jnp.zeros_like(q)) for h in range(nh)], axis=0)

    def tile(j, carry, masked):
        off = pl.multiple_of(j * t, t)
        z = _split_rows(_dot_nt, qs, k_ref[0, pl.ds(off, t), :])
        lg = jnp.log(1.0 + jnp.exp(-jnp.abs(z)))
        logsig = jnp.minimum(z, 0.0) - lg
        lom = logsig - z
        if masked:
            lom = jnp.where(strict, lom, 0.0)
        suffix = _split_rows(_dot, lom.astype(BF16), upper)
        wgt = jnp.exp(logsig + suffix + carry)
        if masked:
            wgt = jnp.where(strict, wgt, 0.0)
        pv = _split_rows(_dot, wgt.astype(BF16), v_ref[0, pl.ds(off, t), :])
        return carry + jnp.sum(lom, axis=-1, keepdims=True), pv

    carry, pv = tile(i, jnp.zeros((nh * t, 1), F32), True)
    acc_ref[...] = pv
    odd = i % 2

    @pl.when(odd == 1)
    def _():
        c1, pv1 = tile(i - 1, carry, False)
        acc_ref[...] += pv1
        carry_ref[...] = c1

    @pl.when(odd == 0)
    def _():
        carry_ref[...] = carry

    def pair(s, c):
        j = i - 1 - odd - 2 * s
        c, pv_a = tile(j, c, False)
        c, pv_b = tile(j - 1, c, False)
        acc_ref[...] += pv_a + pv_b
        return c

    lax.fori_loop(0, i // 2, pair, carry_ref[...])
    out = acc_ref[0:t, :]
    for h in range(1, nh):
        out = jnp.where(own[h], acc_ref[h * t:(h + 1) * t, :], out)
    o_ref[0] = out.astype(o_ref.dtype)


def _sb_attention(zsb):
    B, S, _ = zsb.shape
    t = SB_T
    nh = SB_HEADS_PER_STEP
    w = nh * SB_HD
    npair = SB_W // w
    return pl.pallas_call(
        functools.partial(_sb_kernel, t=t, nh=nh),
        grid=(B, npair, S // t),
        in_specs=[
            pl.BlockSpec((1, t, w), lambda b, p, i: (b, i, p)),
            pl.BlockSpec((1, S, w), lambda b, p, i: (b, 0, npair + p)),
            pl.BlockSpec((1, S, w), lambda b, p, i: (b, 0, 2 * npair + p)),
        ],
        out_specs=pl.BlockSpec((1, t, w), lambda b, p, i: (b, i, p)),
        out_shape=jax.ShapeDtypeStruct((B, S, SB_W), BF16),
        scratch_shapes=[pltpu.VMEM((nh * t, w), F32), pltpu.VMEM((nh * t, 1), F32)],
        compiler_params=_cparams(("parallel", "parallel", "arbitrary")),
        name="sb_attn",
    )(zsb, zsb, zsb)


def _softmax_tile(s, vt, m, l, acc, mask):
    if mask is not None:
        s = jnp.where(mask, s, NEG_INF)
    m_new = jnp.maximum(m, jnp.max(s, axis=-1, keepdims=True))
    alpha = jnp.exp(m - m_new)
    p = jnp.exp(s - m_new)
    if mask is not None:
        p = jnp.where(mask, p, 0.0)
    l = alpha * l + jnp.sum(p, axis=-1, keepdims=True)
    acc = alpha * acc + _dot(p.astype(BF16), vt)
    return m_new, l, acc


def _flash_step(s, vt_ones, m, acc):
    m_new = jnp.maximum(m, jnp.max(s, axis=-1, keepdims=True))
    alpha = jnp.exp(m - m_new)
    p = jnp.exp(s - m_new).astype(BF16)
    return m_new, alpha * acc + _dot(p, vt_ones)


def _flash_finish(acc):
    return acc / jnp.maximum(pltpu.roll(acc, NSA_HD, 1), 1e-30)


def _run_by_pairs(n, body):
    odd = n % 2

    @pl.when(odd == 1)
    def _():
        body((0,))

    def pair(s, c):
        body((odd + 2 * s, odd + 2 * s + 1))
        return c

    lax.fori_loop(0, n // 2, pair, 0)


def _lane_block_max(m, s):
    for c in range(s.shape[1] // LANE):
        m = jnp.maximum(m, s[:, c * LANE:(c + 1) * LANE])
    return m


def _mla_kernel(q_ref, k_ref, v_ref, o_ref, s_ref, mx_ref, acc_ref, *, tq, tk, nh):
    i = pl.program_id(2)
    lane = lax.broadcasted_iota(jnp.int32, (tq, LANE), 1)
    col_minus_row = (lax.broadcasted_iota(jnp.int32, (tq, tk), 1)
                     - lax.broadcasted_iota(jnp.int32, (tq, tk), 0))
    tail = (i * tq) // tk
    thr = i * tq - tail * tk
    heads = [slice(h * LANE, (h + 1) * LANE) for h in range(nh)]

    def score_tiles(js, masked):
        for h, sl in enumerate(heads):
            mx = None if masked else mx_ref[h]
            for j in js:
                s = _dot_nt(q_ref[0, :, sl], k_ref[0, pl.ds(pl.multiple_of(j * tk, tk), tk), sl])
                if masked:
                    s = jnp.where(col_minus_row <= thr, s, NEG_INF)
                s_ref[h, j] = s
                mx = _lane_block_max(s[:, :LANE] if mx is None else mx, s)
            mx_ref[h] = mx

    score_tiles((tail,), True)
    _run_by_pairs(tail, lambda js: score_tiles(js, False))
    m = [jnp.max(mx_ref[h], axis=-1, keepdims=True) for h in range(nh)]

    def absorb(js, first):
        for h, sl in enumerate(heads):
            pv = _tree_sum([_dot(jnp.exp(s_ref[h, j] - m[h]).astype(BF16),
                                 v_ref[0, pl.ds(pl.multiple_of(j * tk, tk), tk), sl]) for j in js])
            acc_ref[h] = pv if first else acc_ref[h] + pv

    absorb((tail,), True)
    _run_by_pairs(tail, lambda js: absorb(js, False))
    for p in range(nh // 2):
        o = jnp.where(lane < MLA_V, _flash_finish(acc_ref[2 * p]),
                      pltpu.roll(_flash_finish(acc_ref[2 * p + 1]), MLA_V, 1))
        o_ref[0, :, p * LANE:(p + 1) * LANE] = o.astype(o_ref.dtype)


def _mla_attention(qm, km, vm):
    B, S, _ = qm.shape
    t = MLA_TQ
    nh = MLA_HEADS_PER_STEP
    ngrp = MLA_HEADS // nh
    return pl.pallas_call(
        functools.partial(_mla_kernel, tq=t, tk=MLA_TK, nh=nh),
        grid=(B, ngrp, S // t),
        in_specs=[
            pl.BlockSpec((1, t, nh * LANE), lambda b, p, i: (b, i, p)),
            pl.BlockSpec((1, S, nh * LANE), lambda b, p, i: (b, 0, p)),
            pl.BlockSpec((1, S, nh * LANE), lambda b, p, i: (b, 0, p)),
        ],
        out_specs=pl.BlockSpec((1, t, nh * MLA_V), lambda b, p, i: (b, i, p)),
        out_shape=jax.ShapeDtypeStruct((B, S, MLA_HEADS * MLA_V), BF16),
        scratch_shapes=[pltpu.VMEM((nh, S // MLA_TK, t, MLA_TK), F32),
                        pltpu.VMEM((nh, t, LANE), F32), pltpu.VMEM((nh, t, LANE), F32)],
        compiler_params=_cparams(("parallel", "parallel", "arbitrary")),
        name="mla_attn",
    )(qm, km, vm)


def _stack_heads(q, lane):
    first = _first_head(lane)
    parts = []
    for h in range(NSA_REP):
        blk = q[:, (h // 2) * LANE:(h // 2 + 1) * LANE]
        parts.append(jnp.where(first if h % 2 == 0 else ~first, blk, jnp.zeros_like(blk)))
    return parts


def _pair_lanes(lane, a, b):
    return jnp.where(lane < NSA_HD, a, b)


def _tree_sum(terms):
    while len(terms) > 1:
        terms = [a + b for a, b in zip(terms[0::2], terms[1::2])] + (terms[-1:] if len(terms) % 2 else [])
    return terms[0]


def _nsa_cmp_kernel(q_ref, kc_ref, vc_ref, ovt_ref, ocmp_ref, bias_ref, *, t, n_slc):
    i = pl.program_id(1)
    gw = NSA_REP * NSA_HD
    lane = lax.broadcasted_iota(jnp.int32, (t, LANE), 1)
    row = lax.broadcasted_iota(jnp.int32, (t, LANE), 0)
    visible = lane * CMP_STRIDE + (CMP_LEN - 1) <= i * t + row
    blk = lax.broadcasted_iota(jnp.int32, (n_slc, t), 0)
    cur = (i * t + lax.broadcasted_iota(jnp.int32, (n_slc, t), 1)) // SLC_LEN
    forced = (blk == 0) | (blk == cur) | (blk == cur - 1)
    for g in range(NSA_GROUPS):
        kc = kc_ref[0, :, g * LANE:(g + 1) * LANE]
        vc = vc_ref[0, :, g * LANE:(g + 1) * LANE]
        psum = jnp.zeros((t, LANE), F32)
        outs = []
        for qh in _stack_heads(q_ref[0, :, g * gw:(g + 1) * gw], lane):
            s = jnp.where(visible, _dot_nt(qh, kc), NEG_INF)
            m = jnp.max(s, axis=-1, keepdims=True)
            e = jnp.where(visible, jnp.exp(s - m), 0.0)
            p = e / jnp.maximum(jnp.sum(e, axis=-1, keepdims=True), 1e-30)
            psum = psum + p
            outs.append(_dot(p.astype(BF16), vc))
        ocmp_ref[0, :, g * gw:g * gw + LANE] = _pair_lanes(lane, outs[0], outs[1])
        ocmp_ref[0, :, g * gw + LANE:(g + 1) * gw] = _pair_lanes(lane, outs[2], outs[3])
        hi = psum.astype(BF16)
        lo = (psum - hi.astype(F32)).astype(BF16)
        imp = (_dot_nt(ovt_ref[...], hi) + _dot_nt(ovt_ref[...], lo))[:n_slc]
        score = jnp.where(blk > cur, -1.0, jnp.where(forced, FORCE_SCORE, imp))
        terms = []
        for c in range(n_slc):
            sc = score[c:c + 1, :]
            before = (sc > score) | ((sc == score) & (blk > c))
            terms.append(jnp.where(before, 1.0, 0.0))
        bias_t = jnp.where(_tree_sum(terms) < float(min(SLC_TOPK, n_slc)), 0.0, NEG_INF)
        bias_t = jnp.concatenate([bias_t, jnp.full((LANE - n_slc, t), NEG_INF, F32)], axis=0)
        bias_ref[0, g] = bias_t.T.astype(BF16)


def _nsa_cmp(qn, kc, vc, overlap):
    B, S, _ = qn.shape
    t = CMP_T
    nkv = NSA_GROUPS * LANE
    return pl.pallas_call(
        functools.partial(_nsa_cmp_kernel, t=t, n_slc=S // SLC_LEN),
        grid=(B, S // t),
        in_specs=[
            pl.BlockSpec((1, t, NSA_HEADS * NSA_HD), lambda b, i: (b, i, 0)),
            pl.BlockSpec((1, kc.shape[1], nkv), lambda b, i: (b, 0, 0)),
            pl.BlockSpec((1, vc.shape[1], nkv), lambda b, i: (b, 0, 0)),
            pl.BlockSpec((LANE, LANE), lambda b, i: (0, 0)),
        ],
        out_specs=[pl.BlockSpec((1, t, NSA_HEADS * NSA_HD), lambda b, i: (b, i, 0)),
                   pl.BlockSpec((1, NSA_GROUPS, t, LANE), lambda b, i: (b, 0, i, 0))],
        out_shape=[jax.ShapeDtypeStruct((B, S, NSA_HEADS * NSA_HD), F32),
                   jax.ShapeDtypeStruct((B, NSA_GROUPS, S, LANE), BF16)],
        compiler_params=_cparams(("parallel", "parallel")),
        name="nsa_cmp",
    )(qn, kc, vc, overlap)


def _nsa_attn_kernel(q_ref, sk_ref, sv_ref, wk_ref, wv_ref, bias_ref, blk_ref, ocmp_ref, gate_ref,
                     o_ref, s_ref, mix_ref, mx_ref, acc_ref, *, t, tk):
    i = pl.program_id(1)
    r4 = NSA_REP * t
    gw = NSA_REP * NSA_HD
    groups = range(NSA_GROUPS)
    lane = lax.broadcasted_iota(jnp.int32, (t, LANE), 1)
    gl = [slice(g * LANE, (g + 1) * LANE) for g in groups]
    q4 = [jnp.concatenate(_stack_heads(q_ref[0, :, g * gw:(g + 1) * gw], lane), axis=0) for g in groups]
    q4b = [jnp.concatenate([q4[g], jnp.concatenate([bias_ref[0, g]] * NSA_REP, axis=0)], axis=1)
           for g in groups]

    rmc = ((lax.broadcasted_iota(jnp.int32, (r4, t), 0) & (t - 1))
           - lax.broadcasted_iota(jnp.int32, (r4, t), 1))

    def mask_blocks(s, visible):
        return jnp.concatenate([jnp.where(visible(c), s[:, c * t:(c + 1) * t], NEG_INF)
                                for c in range(s.shape[1] // t)], axis=1)

    tail = (i * t) // tk
    thr = i * t - tail * tk

    def score_tiles(js, masked):
        for g in groups:
            mx = None if masked else mx_ref[g]
            for j in js:
                off = pl.multiple_of(j * tk, tk)
                kt = jnp.concatenate([sk_ref[0, pl.ds(off, tk), gl[g]], blk_ref[pl.ds(off, tk), :]], axis=1)
                s = _dot_nt(q4b[g], kt)
                if masked:
                    s = mask_blocks(s, lambda c: rmc >= c * t - thr)
                s_ref[g, j] = s
                mx = _lane_block_max(s[:, :LANE] if mx is None else mx, s)
            mx_ref[g] = mx

    score_tiles((tail,), True)
    _run_by_pairs(tail, lambda js: score_tiles(js, False))
    m_s = [jnp.max(mx_ref[g], axis=-1, keepdims=True) for g in groups]

    def gate_pair(g, br, p2):
        c0 = g * LANE + br * NSA_REP + 2 * p2
        return _pair_lanes(lane, gate_ref[0, :, c0:c0 + 1], gate_ref[0, :, c0 + 1:c0 + 2])

    def head_pair(o, p2):
        a, b = 2 * p2, 2 * p2 + 1
        return _pair_lanes(lane, o[a * t:(a + 1) * t], pltpu.roll(o[b * t:(b + 1) * t], NSA_HD, 1))

    wkeys = WIN + t
    start = pl.multiple_of(jnp.maximum(i * t - WIN, 0), t)
    lead = i * t - start

    def in_window(c):
        return pltpu.bitcast(rmc + (lead - c * t), jnp.uint32) < jnp.uint32(WIN)

    for g in groups:
        s = mask_blocks(_dot_nt(q4[g], wk_ref[0, pl.ds(start, wkeys), gl[g]]), in_window)
        p = jnp.exp(s - jnp.max(s, axis=-1, keepdims=True)).astype(BF16)
        o_w = _flash_finish(_dot(p, wv_ref[0, pl.ds(start, wkeys), gl[g]]))
        for p2 in range(NSA_REP // 2):
            cols = slice(g * gw + p2 * LANE, g * gw + (p2 + 1) * LANE)
            mix_ref[:, cols] = gate_pair(g, 0, p2) * ocmp_ref[0, :, cols] + gate_pair(g, 2, p2) * head_pair(o_w, p2)

    def absorb(js, first):
        for g in groups:
            pv = _tree_sum([_dot(jnp.exp(s_ref[g, j] - m_s[g]).astype(BF16),
                                 sv_ref[0, pl.ds(pl.multiple_of(j * tk, tk), tk), gl[g]]) for j in js])
            acc_ref[g] = pv if first else acc_ref[g] + pv

    absorb((tail,), True)
    _run_by_pairs(tail, lambda js: absorb(js, False))
    for g in groups:
        o_s = _flash_finish(acc_ref[g])
        for p2 in range(NSA_REP // 2):
            cols = slice(g * gw + p2 * LANE, g * gw + (p2 + 1) * LANE)
            o_ref[0, :, cols] = (mix_ref[:, cols] + gate_pair(g, 1, p2) * head_pair(o_s, p2)).astype(o_ref.dtype)


def _nsa_attention(qn, sk, sv, wk, wv, bias, blk_onehot, ocmp, gates):
    B, S, _ = qn.shape
    t = ATT_T
    hw = NSA_HEADS * NSA_HD
    nkv = NSA_GROUPS * LANE
    kv = pl.BlockSpec((1, S, nkv), lambda b, i: (b, 0, 0))
    return pl.pallas_call(
        functools.partial(_nsa_attn_kernel, t=t, tk=NSA_TK),
        grid=(B, S // t),
        in_specs=[
            pl.BlockSpec((1, t, hw), lambda b, i: (b, i, 0)),
            kv, kv, kv, kv,
            pl.BlockSpec((1, NSA_GROUPS, t, LANE), lambda b, i: (b, 0, i, 0)),
            pl.BlockSpec((S, LANE), lambda b, i: (0, 0)),
            pl.BlockSpec((1, t, hw), lambda b, i: (b, i, 0)),
            pl.BlockSpec((1, t, nkv), lambda b, i: (b, i, 0)),
        ],
        out_specs=pl.BlockSpec((1, t, hw), lambda b, i: (b, i, 0)),
        out_shape=jax.ShapeDtypeStruct((B, S, hw), BF16),
        scratch_shapes=[pltpu.VMEM((NSA_GROUPS, S // NSA_TK, NSA_REP * t, NSA_TK), F32),
                        pltpu.VMEM((t, hw), F32),
                        pltpu.VMEM((NSA_GROUPS, NSA_REP * t, LANE), F32),
                        pltpu.VMEM((NSA_GROUPS, NSA_REP * t, LANE), F32)],
        compiler_params=_cparams(("parallel", "arbitrary")),
        name="nsa_attn",
    )(qn, sk, sv, wk, wv, bias, blk_onehot, ocmp, gates)


def _merge_kernel(x_ref, g_ref, osb_ref, omla_ref, onsa_ref, wg_ref, psb_ref, pmla_ref, pnsa_ref,
                  wout_ref, o_ref):
    x = x_ref[...]
    h = _rms(x, g_ref[...]).astype(BF16)
    y = jnp.zeros(x.shape, F32)
    for br, (o_r, p_r) in enumerate(((osb_ref, psb_ref), (omla_ref, pmla_ref), (onsa_ref, pnsa_ref))):
        gate = jax.nn.sigmoid(_dot(h, wg_ref[:, br * D_MODEL:(br + 1) * D_MODEL]))
        y = y + gate * _dot(o_r[...], p_r[...])
    o_ref[...] = x + _dot(y.astype(BF16), wout_ref[...])


def _merge(x, g, osb, omla, onsa, w):
    T, D = x.shape
    tm = MERGE_TM
    tok = lambda width: pl.BlockSpec((tm, width), lambda i: (i, 0))
    full = lambda shape: pl.BlockSpec(shape, lambda i: (0,) * len(shape))
    return pl.pallas_call(
        _merge_kernel,
        grid=(T // tm,),
        in_specs=[tok(D), full((1, D)), tok(osb.shape[1]), tok(omla.shape[1]), tok(onsa.shape[1]),
                  full((D, N_BRANCH * D)), full(w["proj_sb"].shape), full(w["proj_mla"].shape),
                  full(w["proj_nsa"].shape), full((D, D))],
        out_specs=tok(D),
        out_shape=jax.ShapeDtypeStruct((T, D), F32),
        compiler_params=_cparams(("parallel",)),
        name="merge",
    )(x, g, osb, omla, onsa, w["w_gate"], w["proj_sb"], w["proj_mla"], w["proj_nsa"], w["w_out"])


def _cos_sin(pos, d):
    half = d // 2
    inv = jnp.exp(-math.log(ROPE_THETA) * jnp.arange(half, dtype=F32) * (2.0 / d))
    ang = pos.astype(F32)[:, None] * inv[None, :]
    return jnp.cos(ang), jnp.sin(ang)


def _pair_rope_tables(pos):
    cos, sin = _cos_sin(pos, NSA_HD)
    return jnp.concatenate([cos] * 4, axis=1), jnp.concatenate([-sin, -sin, sin, sin], axis=1)


def _mla_rope_tables(pos):
    cos, sin = _cos_sin(pos, MLA_ROPE)
    n, half = cos.shape
    one = lambda k: jnp.ones((n, k), F32)
    zero = lambda k: jnp.zeros((n, k), F32)
    rest = LANE // 2 - half
    c = jnp.concatenate([cos, one(rest), cos, one(MLA_NOPE - rest), zero(LANE - MLA_QK)], axis=1)
    s = jnp.concatenate([-sin, zero(rest), sin, zero(LANE // 2 - half)], axis=1)
    return c, s


def _perm_pair(a, b):
    h = NSA_HD // 2
    return jnp.concatenate([a[..., :h], b[..., :h], a[..., h:], b[..., h:]], axis=-1)


def _perm_mla(nope, rope):
    h = MLA_ROPE // 2
    rest = LANE // 2 - h
    pad = jnp.zeros(nope.shape[:-1] + (LANE - MLA_QK,), nope.dtype)
    return jnp.concatenate([rope[..., :h], nope[..., :rest], rope[..., h:], nope[..., rest:], pad], axis=-1)


def _tables(S):
    pos = jnp.arange(S, dtype=jnp.int32)
    nchunk = S // CMP_STRIDE
    ends = jnp.arange(nchunk, dtype=jnp.int32) * CMP_STRIDE + (CMP_LEN - 1)
    t = {}
    t["nsa_c"], t["nsa_s"] = _pair_rope_tables(pos)
    t["cmp_c"], t["cmp_s"] = _pair_rope_tables(ends)
    t["mla_c"], t["mla_s"] = _mla_rope_tables(pos)
    t["blk_onehot"] = (pos[:, None] // SLC_LEN == jnp.arange(LANE, dtype=jnp.int32)[None, :]).astype(BF16)
    n_cmp = (S - CMP_LEN) // CMP_STRIDE + 1
    n_slc = S // SLC_LEN
    c0 = np.arange(n_cmp)[:, None] * CMP_STRIDE
    s0 = np.arange(n_slc)[None, :] * SLC_LEN
    ov = np.clip(np.minimum(c0 + CMP_LEN, s0 + SLC_LEN) - np.maximum(c0, s0), 0, None) / CMP_LEN
    full = np.zeros((LANE, LANE), np.float32)
    full[:n_slc, :n_cmp] = ov.T
    t["overlap"] = jnp.asarray(full, BF16)
    return t


def _pad_cols(w, width):
    return jnp.pad(w, ((0, 0), (0, width - w.shape[1])))


def _dup_groups(w, permuted):
    parts = []
    for g in range(NSA_GROUPS):
        blk = w[:, g * NSA_HD:(g + 1) * NSA_HD]
        parts.append(_perm_pair(blk, blk) if permuted else jnp.concatenate([blk, blk], axis=1))
    return jnp.concatenate(parts, axis=1)


def _blockdiag2(a):
    z = jnp.zeros_like(a)
    return jnp.concatenate([jnp.concatenate([a, z], axis=1), jnp.concatenate([z, a], axis=1)], axis=0)


def _layer_weights(l, p):
    w = {}
    row = lambda v: v[l][None, :].astype(F32)
    offs = np.concatenate([[0], np.cumsum(IN_SIZES)])
    cols = [p["w_in"][l][:, offs[k]:offs[k + 1]] for k in range(len(IN_SIZES))]
    (sb_q, sb_k, sb_v, cq, ckv, kr, nq, ck, cv, sk, sv, wk, wv, ngate, mgate) = cols
    sb_scale = 1.0 / math.sqrt(SB_HD)
    w["w_sb"] = jnp.concatenate([sb_q * sb_scale, sb_k, sb_v], axis=1).astype(BF16)
    kr_blk = _perm_mla(jnp.zeros((D_MODEL, MLA_NOPE), kr.dtype), kr)
    nq = jnp.concatenate([_perm_pair(nq[:, b * LANE:b * LANE + NSA_HD], nq[:, b * LANE + NSA_HD:(b + 1) * LANE])
                          for b in range(NSA_HEADS * NSA_HD // LANE)], axis=1)
    ng = ngate.reshape(D_MODEL, N_BRANCH, NSA_GROUPS, NSA_REP)
    ng = jnp.concatenate([_pad_cols(ng[:, :, g, :].reshape(D_MODEL, N_BRANCH * NSA_REP), LANE)
                          for g in range(NSA_GROUPS)], axis=1)
    w["w_z2"] = jnp.concatenate([cq, ckv, kr_blk, nq, ck, cv, _dup_groups(sk, True), _dup_groups(sv, False),
                                 _dup_groups(wk, True), _dup_groups(wv, False), ng], axis=1).astype(BF16)
    w["w_gate"] = mgate.astype(BF16)
    w["mix_norm"] = row(p["mix_norm"])
    w["mla_q_norm"] = row(p["mla_q_norm"])
    w["mla_kv_norm"] = row(p["mla_kv_norm"])
    wuq = p["mla_w_uq"][l].reshape(MLA_Q_LORA, MLA_HEADS, MLA_QK)
    w["wuq"] = _perm_mla(wuq[:, :, :MLA_NOPE], wuq[:, :, MLA_NOPE:]).reshape(MLA_Q_LORA, -1).astype(BF16)
    wukv = p["mla_w_ukv"][l].reshape(MLA_KV_LORA, MLA_HEADS, MLA_NOPE + MLA_V)
    no_rope = jnp.zeros((MLA_KV_LORA, MLA_HEADS, MLA_ROPE), wukv.dtype)
    w["wuk"] = _perm_mla(wukv[:, :, :MLA_NOPE], no_rope).reshape(MLA_KV_LORA, -1).astype(BF16)
    w["wuv"] = jnp.pad(wukv[:, :, MLA_NOPE:], ((0, 0), (0, 0), (0, LANE - MLA_V))).reshape(MLA_KV_LORA, -1).astype(BF16)
    gq, gk = row(p["mla_qk_gain_q"]), row(p["mla_qk_gain_k"])
    w["mla_gq"] = _perm_mla(gq[:, :MLA_NOPE], gq[:, MLA_NOPE:])
    w["mla_gk"] = _perm_mla(gk[:, :MLA_NOPE], gk[:, MLA_NOPE:])
    w["nsa_qg"] = _perm_pair(row(p["nsa_q_gain"]), row(p["nsa_q_gain"]))
    w["nsa_kg"] = _perm_pair(row(p["nsa_k_gain"]), row(p["nsa_k_gain"]))
    w["pe_k"] = jnp.tile(p["cmp_pos_k"][l], (1, 2)).astype(F32)
    w["pe_v"] = jnp.tile(p["cmp_pos_v"][l], (1, 2)).astype(F32)
    for nm, w1, w2 in (("k", p["cmp_wk1"][l], p["cmp_wk2"][l]), ("v", p["cmp_wv1"][l], p["cmp_wv2"][l])):
        w1 = w1.reshape(CMP_LEN, NSA_HD, CMP_HIDDEN)
        bd = jax.vmap(_blockdiag2)(w1).astype(BF16)
        w["w1%s_lo" % nm] = bd[:CMP_STRIDE]
        w["w1%s_hi" % nm] = bd[CMP_STRIDE:]
        w["w2" + nm] = _blockdiag2(_perm_pair(w2, w2) if nm == "k" else jnp.concatenate([w2, w2], axis=1)).astype(BF16)
    for nm in ("proj_sb", "proj_mla", "proj_nsa", "w_out"):
        w[nm] = p[nm][l].astype(BF16)
    for nm in ("ffn1", "ffn2"):
        w[nm + "_norm"] = row(p[nm + "_norm"])
    return w


def kernel(x, ffn1_norm, ffn1_wi, ffn1_wo, mix_norm, w_in, mla_q_norm, mla_w_uq, mla_kv_norm, mla_w_ukv, mla_qk_gain_q, mla_qk_gain_k, nsa_q_gain, nsa_k_gain, cmp_pos_k, cmp_pos_v, cmp_wk1, cmp_wk2, cmp_wv1, cmp_wv2, proj_sb, proj_mla, proj_nsa, w_out, ffn2_norm, ffn2_wi, ffn2_wo):
    p = dict(ffn1_norm=ffn1_norm, ffn1_wi=ffn1_wi, ffn1_wo=ffn1_wo, mix_norm=mix_norm, w_in=w_in,
             mla_q_norm=mla_q_norm, mla_w_uq=mla_w_uq, mla_kv_norm=mla_kv_norm, mla_w_ukv=mla_w_ukv,
             mla_qk_gain_q=mla_qk_gain_q, mla_qk_gain_k=mla_qk_gain_k, nsa_q_gain=nsa_q_gain,
             nsa_k_gain=nsa_k_gain, cmp_pos_k=cmp_pos_k, cmp_pos_v=cmp_pos_v, cmp_wk1=cmp_wk1,
             cmp_wk2=cmp_wk2, cmp_wv1=cmp_wv1, cmp_wv2=cmp_wv2, proj_sb=proj_sb, proj_mla=proj_mla,
             proj_nsa=proj_nsa, w_out=w_out, ffn2_norm=ffn2_norm, ffn2_wi=ffn2_wi, ffn2_wo=ffn2_wo)
    B, S, D = x.shape
    assert S // CMP_STRIDE == LANE and S // SLC_LEN <= LANE and D == D_MODEL
    tabs = _tables(S)
    xt = x.reshape(B * S, D)
    for l in range(ffn1_wi.shape[0]):
        w = _layer_weights(l, p)
        xt = _ffn(xt, w["ffn1_norm"], ffn1_wi, ffn1_wo, l)
        zsb, z2 = _inproj(xt, w["mix_norm"], w["w_sb"], w["w_z2"])
        zsb = zsb.reshape(B, S, -1)
        z2 = z2.reshape(B, S, -1)
        qm, km, vm, qn, sk, sv, wk, wv, gates = _prep(z2, w, tabs)
        kc, vc = _compress(z2, w, tabs)
        o_sb = _sb_attention(zsb)
        o_mla = _mla_attention(qm, km, vm)
        o_cmp, bias = _nsa_cmp(qn, kc, vc, tabs["overlap"])
        o_nsa = _nsa_attention(qn, sk, sv, wk, wv, bias, tabs["blk_onehot"], o_cmp, gates)
        xt = _merge(xt, w["mix_norm"], o_sb.reshape(B * S, -1), o_mla.reshape(B * S, -1),
                    o_nsa.reshape(B * S, -1), w)
        xt = _ffn(xt, w["ffn2_norm"], ffn2_wi, ffn2_wo, l)
    return xt.reshape(B, S, D)
```

```python
import functools
import math

import numpy as np
import jax
import jax.numpy as jnp
from jax import lax
from jax.experimental import pallas as pl
from jax.experimental.pallas import tpu as pltpu

D_MODEL = 1024
D_FF = 2816
EPS = 1e-6
ROPE_THETA = 10000.0
N_BRANCH = 3
SB_HEADS = 8
SB_HD = 64
MLA_HEADS = 8
MLA_NOPE = 64
MLA_ROPE = 32
MLA_V = 64
MLA_Q_LORA = 256
MLA_KV_LORA = 128
MLA_QK = MLA_NOPE + MLA_ROPE
NSA_HEADS = 8
NSA_GROUPS = 2
NSA_REP = NSA_HEADS // NSA_GROUPS
NSA_HD = 64
CMP_LEN = 32
CMP_STRIDE = 16
CMP_HIDDEN = 128
SLC_LEN = 64
SLC_TOPK = 16
WIN = 512
FORCE_SCORE = 1e3
NEG_INF = -1e30
SB_W = SB_HEADS * SB_HD
NSA_KV_W = NSA_GROUPS * NSA_HD
IN_SIZES = (SB_W, SB_W, SB_W, MLA_Q_LORA, MLA_KV_LORA, MLA_ROPE, NSA_HEADS * NSA_HD,
            NSA_KV_W, NSA_KV_W, NSA_KV_W, NSA_KV_W, NSA_KV_W, NSA_KV_W,
            N_BRANCH * NSA_HEADS, N_BRANCH * D_MODEL)

LANE = 128
V7X_VMEM_LIMIT = 56 * 1024 * 1024

FFN_TM = 1024
FFN_TF = 256
PROJ_TM = 512
PREP_TS = 512
ATT_T = 128
NSA_TK = 512
CMP_T = 512
SB_T = 256
SB_HEADS_PER_STEP = 4
MLA_TQ = 512
MLA_HEADS_PER_STEP = 4
MLA_TK = 512
MERGE_TM = 512

Z2_CQ = 0
Z2_CKV = 256
Z2_KR = 384
Z2_NQ = 512
Z2_CK = 1024
Z2_CV = 1152
Z2_SK = 1280
Z2_SV = 1536
Z2_WK = 1792
Z2_WV = 2048
Z2_GATE = 2304
Z2_W = 2560

BF16 = jnp.bfloat16
F32 = jnp.float32


def _cparams(sem, vmem=V7X_VMEM_LIMIT):
    return pltpu.CompilerParams(dimension_semantics=sem, vmem_limit_bytes=vmem)


def _dot(a, b):
    return jnp.dot(a, b, preferred_element_type=F32)


def _dot_nt(a, b):
    return lax.dot_general(a, b, (((1,), (1,)), ((), ())), preferred_element_type=F32)


def _split_rows(dot, a, b):
    h = a.shape[0] // 2
    return jnp.concatenate([dot(a[:h], b), dot(a[h:], b)], axis=0)


def _rms(x, g):
    y = x * lax.rsqrt(jnp.mean(x * x, axis=-1, keepdims=True) + EPS)
    return y * g


def _ffn_kernel(x_ref, g_ref, wa_ref, wb_ref, wo_ref, o_ref, h_ref, acc_ref):
    j = pl.program_id(1)

    @pl.when(j == 0)
    def _():
        h_ref[...] = _rms(x_ref[...], g_ref[...]).astype(BF16)
        acc_ref[...] = jnp.zeros_like(acc_ref)

    h = h_ref[...]
    a = _dot(h, wa_ref[...].astype(BF16))
    b = _dot(h, wb_ref[...].astype(BF16))
    u = (jax.nn.silu(a) * b).astype(BF16)
    acc_ref[...] += _dot(u, wo_ref[...].astype(BF16))

    @pl.when(j == pl.num_programs(1) - 1)
    def _():
        o_ref[...] = x_ref[...] + 0.5 * acc_ref[...]


def _ffn(x, g, wi, wo, l):
    T, D = x.shape
    F = wo.shape[1]
    tm, tf = FFN_TM, FFN_TF
    nf = F // tf
    return pl.pallas_call(
        _ffn_kernel,
        grid=(T // tm, nf),
        in_specs=[
            pl.BlockSpec((tm, D), lambda i, j: (i, 0)),
            pl.BlockSpec((1, D), lambda i, j: (0, 0)),
            pl.BlockSpec((None, D, tf), lambda i, j: (l, 0, j)),
            pl.BlockSpec((None, D, tf), lambda i, j: (l, 0, j + nf)),
            pl.BlockSpec((None, tf, D), lambda i, j: (l, j, 0)),
        ],
        out_specs=pl.BlockSpec((tm, D), lambda i, j: (i, 0)),
        out_shape=jax.ShapeDtypeStruct((T, D), F32),
        scratch_shapes=[pltpu.VMEM((tm, D), BF16), pltpu.VMEM((tm, D), F32)],
        compiler_params=_cparams(("parallel", "arbitrary")),
        name="ffn",
    )(x, g, wi, wi, wo)


def _rope_mix(y, c, s):
    return y * c + pltpu.roll(y, LANE // 2, 1) * s


def _first_head(lane):
    return (lane & (NSA_HD // 2)) == 0


def _norm_rope_pair(x, gain, c, s, scale):
    first = _first_head(lax.broadcasted_iota(jnp.int32, x.shape, 1))
    sq = x * x
    ss_a = jnp.sum(jnp.where(first, sq, 0.0), axis=-1, keepdims=True)
    ss_b = jnp.sum(jnp.where(first, 0.0, sq), axis=-1, keepdims=True)
    r = jnp.where(first, lax.rsqrt(ss_a * (1.0 / NSA_HD) + EPS), lax.rsqrt(ss_b * (1.0 / NSA_HD) + EPS))
    y = _rope_mix(x * r * gain, c, s)
    return y * scale if scale != 1.0 else y


def _norm_rope_mla(x, gain, c, s, scale):
    ss = jnp.sum(x * x, axis=-1, keepdims=True)
    y = _rope_mix(x * lax.rsqrt(ss * (1.0 / MLA_QK) + EPS) * gain, c, s)
    return y * scale if scale != 1.0 else y


def _inproj_kernel(x_ref, g_ref, wa_ref, wb_ref, oa_ref, ob_ref):
    h = _rms(x_ref[...], g_ref[...]).astype(BF16)
    oa_ref[...] = _dot(h, wa_ref[...]).astype(oa_ref.dtype)
    ob_ref[...] = _dot(h, wb_ref[...]).astype(ob_ref.dtype)


def _inproj(x, g, wa, wb):
    T, D = x.shape
    tm = PROJ_TM
    na, nb = wa.shape[1], wb.shape[1]
    return pl.pallas_call(
        _inproj_kernel,
        grid=(T // tm,),
        in_specs=[
            pl.BlockSpec((tm, D), lambda i: (i, 0)),
            pl.BlockSpec((1, D), lambda i: (0, 0)),
            pl.BlockSpec((D, na), lambda i: (0, 0)),
            pl.BlockSpec((D, nb), lambda i: (0, 0)),
        ],
        out_specs=[pl.BlockSpec((tm, na), lambda i: (i, 0)),
                   pl.BlockSpec((tm, nb), lambda i: (i, 0))],
        out_shape=[jax.ShapeDtypeStruct((T, na), BF16), jax.ShapeDtypeStruct((T, nb), F32)],
        compiler_params=_cparams(("parallel",)),
        name="inproj",
    )(x, g, wa, wb)


def _prep_kernel(z_ref, qn_ref, kvn_ref, wuq_ref, wuk_ref, wuv_ref, gq_ref, gk_ref,
                 mc_ref, ms_ref, nqg_ref, nkg_ref, nc_ref, ns_ref,
                 qm_ref, km_ref, vm_ref, qn_out, sk_out, sv_out, wk_out, wv_out, gate_out):
    zs = lambda start, width: z_ref[0, :, start:start + width]
    mc, ms = mc_ref[...], ms_ref[...]
    nc, ns = nc_ref[...], ns_ref[...]
    cq = _rms(zs(Z2_CQ, MLA_Q_LORA), qn_ref[...]).astype(BF16)
    ckv = _rms(zs(Z2_CKV, MLA_KV_LORA), kvn_ref[...]).astype(BF16)
    kr = zs(Z2_KR, LANE)
    q = _dot(cq, wuq_ref[...])
    kn = _dot(ckv, wuk_ref[...])
    v = _dot(ckv, wuv_ref[...])
    vlane = lax.broadcasted_iota(jnp.int32, (v.shape[0], LANE), 1)
    q_scale = 1.0 / math.sqrt(MLA_QK)
    for h in range(MLA_HEADS):
        sl = slice(h * LANE, (h + 1) * LANE)
        vm_ref[0, :, sl] = jnp.where(vlane < MLA_V, v[:, sl], 1.0).astype(BF16)
        qm_ref[0, :, sl] = _norm_rope_mla(q[:, sl], gq_ref[...], mc, ms, q_scale).astype(BF16)
        km_ref[0, :, sl] = _norm_rope_mla(kn[:, sl] + kr, gk_ref[...], mc, ms, 1.0).astype(BF16)
    nq_scale = 1.0 / math.sqrt(NSA_HD)
    for b in range(NSA_HEADS * NSA_HD // LANE):
        sl = slice(b * LANE, (b + 1) * LANE)
        qn_out[0, :, sl] = _norm_rope_pair(zs(Z2_NQ + b * LANE, LANE), nqg_ref[...], nc, ns,
                                           nq_scale).astype(BF16)
    for b in range(2):
        sl = slice(b * LANE, (b + 1) * LANE)
        sk_out[0, :, sl] = _norm_rope_pair(zs(Z2_SK + b * LANE, LANE), nkg_ref[...], nc, ns, 1.0).astype(BF16)
        wk_out[0, :, sl] = _norm_rope_pair(zs(Z2_WK + b * LANE, LANE), nkg_ref[...], nc, ns, 1.0).astype(BF16)
    glane = lax.broadcasted_iota(jnp.int32, (z_ref.shape[1], 2 * LANE), 1) & (LANE - 1)
    sv_out[0] = jnp.where(glane < NSA_HD, zs(Z2_SV, 2 * LANE), 1.0).astype(BF16)
    wv_out[0] = jnp.where(glane < NSA_HD, zs(Z2_WV, 2 * LANE), 1.0).astype(BF16)
    gate_out[0] = jax.nn.sigmoid(zs(Z2_GATE, 2 * LANE))


def _prep(z2, w, tabs):
    B, S, _ = z2.shape
    ts = PREP_TS
    full = lambda shape: pl.BlockSpec(shape, lambda b, i: (0,) * len(shape))
    tab = pl.BlockSpec((ts, LANE), lambda b, i: (i, 0))
    tok = lambda width: pl.BlockSpec((1, ts, width), lambda b, i: (b, i, 0))
    out_w = (MLA_HEADS * LANE, MLA_HEADS * LANE, MLA_HEADS * LANE, NSA_HEADS * NSA_HD,
             2 * LANE, 2 * LANE, 2 * LANE, 2 * LANE, 2 * LANE)
    out_dt = (BF16,) * 8 + (F32,)
    return pl.pallas_call(
        _prep_kernel,
        grid=(B, S // ts),
        in_specs=[
            tok(Z2_W),
            full((1, MLA_Q_LORA)), full((1, MLA_KV_LORA)),
            full((MLA_Q_LORA, MLA_HEADS * LANE)), full((MLA_KV_LORA, MLA_HEADS * LANE)),
            full((MLA_KV_LORA, MLA_HEADS * LANE)),
            full((1, LANE)), full((1, LANE)),
            tab, tab,
            full((1, LANE)), full((1, LANE)),
            tab, tab,
        ],
        out_specs=[tok(wd) for wd in out_w],
        out_shape=[jax.ShapeDtypeStruct((B, S, wd), dt) for wd, dt in zip(out_w, out_dt)],
        compiler_params=_cparams(("parallel", "parallel")),
        name="prep",
    )(z2, w["mla_q_norm"], w["mla_kv_norm"], w["wuq"], w["wuk"], w["wuv"], w["mla_gq"], w["mla_gk"],
      tabs["mla_c"], tabs["mla_s"], w["nsa_qg"], w["nsa_kg"], tabs["nsa_c"], tabs["nsa_s"])


def _compress_kernel(ck_ref, cv_ref, pek_ref, pev_ref, w1kl_ref, w1kh_ref, w2k_ref,
                     w1vl_ref, w1vh_ref, w2v_ref, kg_ref, c_ref, s_ref,
                     kc_ref, vc_ref, *, nchunk):
    def branch(t_ref, pe_ref, w1l_ref, w1h_ref, w2_ref):
        lo = jnp.zeros((nchunk, 2 * CMP_HIDDEN), F32)
        hi = jnp.zeros((nchunk, 2 * CMP_HIDDEN), F32)
        for l in range(CMP_STRIDE):
            x = t_ref[0, pl.ds(l, nchunk, stride=CMP_STRIDE), :]
            lo = lo + _dot((x + pe_ref[l:l + 1, :]).astype(BF16), w1l_ref[l])
            hi = hi + _dot((x + pe_ref[CMP_STRIDE + l:CMP_STRIDE + l + 1, :]).astype(BF16), w1h_ref[l])
        pre = lo + pltpu.roll(hi, nchunk - 1, 0)
        return _dot(jax.nn.gelu(pre).astype(BF16), w2_ref[...])

    kc = branch(ck_ref, pek_ref, w1kl_ref, w1kh_ref, w2k_ref)
    vc = branch(cv_ref, pev_ref, w1vl_ref, w1vh_ref, w2v_ref)
    for b in range(2):
        sl = slice(b * LANE, (b + 1) * LANE)
        kc_ref[0, :, sl] = _norm_rope_pair(kc[:, sl], kg_ref[...], c_ref[...], s_ref[...], 1.0).astype(BF16)
    vc_ref[0] = vc.astype(BF16)


def _compress(z2, w, tabs):
    B, S, _ = z2.shape
    nchunk = S // CMP_STRIDE
    full = lambda shape: pl.BlockSpec(shape, lambda b: (0,) * len(shape))
    tab = full((nchunk, LANE))
    return pl.pallas_call(
        functools.partial(_compress_kernel, nchunk=nchunk),
        grid=(B,),
        in_specs=[
            pl.BlockSpec((1, S, LANE), lambda b: (b, 0, Z2_CK // LANE)),
            pl.BlockSpec((1, S, LANE), lambda b: (b, 0, Z2_CV // LANE)),
            full((CMP_LEN, LANE)), full((CMP_LEN, LANE)),
            full((CMP_STRIDE, LANE, 2 * CMP_HIDDEN)), full((CMP_STRIDE, LANE, 2 * CMP_HIDDEN)),
            full((2 * CMP_HIDDEN, 2 * LANE)),
            full((CMP_STRIDE, LANE, 2 * CMP_HIDDEN)), full((CMP_STRIDE, LANE, 2 * CMP_HIDDEN)),
            full((2 * CMP_HIDDEN, 2 * LANE)),
            full((1, LANE)), tab, tab,
        ],
        out_specs=[pl.BlockSpec((1, nchunk, 2 * LANE), lambda b: (b, 0, 0))] * 2,
        out_shape=[jax.ShapeDtypeStruct((B, nchunk, 2 * LANE), BF16)] * 2,
        compiler_params=_cparams(("parallel",)),
        name="compress",
    )(z2, z2, w["pe_k"], w["pe_v"], w["w1k_lo"], w["w1k_hi"], w["w2k"],
      w["w1v_lo"], w["w1v_hi"], w["w2v"], w["nsa_kg"], tabs["cmp_c"], tabs["cmp_s"])


def _sb_kernel(q_ref, k_ref, v_ref, o_ref, acc_ref, carry_ref, *, t, nh):
    i = pl.program_id(2)
    w = nh * SB_HD
    lane = lax.broadcasted_iota(jnp.int32, (t, w), 1)
    row = lax.broadcasted_iota(jnp.int32, (t, t), 0)
    col = lax.broadcasted_iota(jnp.int32, (t, t), 1)
    upper = jnp.where(row > col, 1.0, 0.0).astype(BF16)
    strict = (lax.broadcasted_iota(jnp.int32, (nh * t, t), 1)
              < (lax.broadcasted_iota(jnp.int32, (nh * t, t), 0) & (t - 1)))
    q = q_ref[0]
    own = [(lane >= SB_HD * h) & (lane < SB_HD * (h + 1)) for h in range(nh)]
    qs = jnp.concatenate([jnp.where(own[h], q, jnp.zeros_like(q)) for h in range(nh)], axis=0)

    def tile(j, carry, masked):
        off = pl.multiple_of(j * t, t)
        z = _split_rows(_dot_nt, qs, k_ref[0, pl.ds(off, t), :])
        lg = jnp.log(1.0 + jnp.exp(-jnp.abs(z)))
        logsig = jnp.minimum(z, 0.0) - lg
        lom = logsig - z
        if masked:
            lom = jnp.where(strict, lom, 0.0)
        suffix = _split_rows(_dot, lom.astype(BF16), upper)
        wgt = jnp.exp(logsig + suffix + carry)
        if masked:
            wgt = jnp.where(strict, wgt, 0.0)
        pv = _split_rows(_dot, wgt.astype(BF16), v_ref[0, pl.ds(off, t), :])
        return carry + jnp.sum(lom, axis=-1, keepdims=True), pv

    carry, pv = tile(i, jnp.zeros((nh * t, 1), F32), True)
    acc_ref[...] = pv
    odd = i % 2

    @pl.when(odd == 1)
    def _():
        c1, pv1 = tile(i - 1, carry, False)
        acc_ref[...] += pv1
        carry_ref[...] = c1

    @pl.when(odd == 0)
    def _():
        carry_ref[...] = carry

    def pair(s, c):
        j = i - 1 - odd - 2 * s
        c, pv_a = tile(j, c, False)
        c, pv_b = tile(j - 1, c, False)
        acc_ref[...] += pv_a + pv_b
        return c

    lax.fori_loop(0, i // 2, pair, carry_ref[...])
    out = acc_ref[0:t, :]
    for h in range(1, nh):
        out = jnp.where(own[h], acc_ref[h * t:(h + 1) * t, :], out)
    o_ref[0] = out.astype(o_ref.dtype)


def _sb_attention(zsb):
    B, S, _ = zsb.shape
    t = SB_T
    nh = SB_HEADS_PER_STEP
    w = nh * SB_HD
    npair = SB_W // w
    return pl.pallas_call(
        functools.partial(_sb_kernel, t=t, nh=nh),
        grid=(B, npair, S // t),
        in_specs=[
            pl.BlockSpec((1, t, w), lambda b, p, i: (b, i, p)),
            pl.BlockSpec((1, S, w), lambda b, p, i: (b, 0, npair + p)),
            pl.BlockSpec((1, S, w), lambda b, p, i: (b, 0, 2 * npair + p)),
        ],
        out_specs=pl.BlockSpec((1, t, w), lambda b, p, i: (b, i, p)),
        out_shape=jax.ShapeDtypeStruct((B, S, SB_W), BF16),
        scratch_shapes=[pltpu.VMEM((nh * t, w), F32), pltpu.VMEM((nh * t, 1), F32)],
        compiler_params=_cparams(("parallel", "parallel", "arbitrary")),
        name="sb_attn",
    )(zsb, zsb, zsb)


def _flash_finish(acc):
    return acc / jnp.maximum(pltpu.roll(acc, NSA_HD, 1), 1e-30)


def _run_by_pairs(n, body):
    odd = n % 2

    @pl.when(odd == 1)
    def _():
        body((0,))

    def pair(s, c):
        body((odd + 2 * s, odd + 2 * s + 1))
        return c

    lax.fori_loop(0, n // 2, pair, 0)


def _lane_block_max(m, s):
    for c in range(s.shape[1] // LANE):
        m = jnp.maximum(m, s[:, c * LANE:(c + 1) * LANE])
    return m


def _mla_kernel(q_ref, k_ref, v_ref, o_ref, s_ref, mx_ref, acc_ref, *, tq, tk, nh):
    i = pl.program_id(2)
    lane = lax.broadcasted_iota(jnp.int32, (tq, LANE), 1)
    col_minus_row = (lax.broadcasted_iota(jnp.int32, (tq, tk), 1)
                     - lax.broadcasted_iota(jnp.int32, (tq, tk), 0))
    tail = (i * tq) // tk
    thr = i * tq - tail * tk
    heads = [slice(h * LANE, (h + 1) * LANE) for h in range(nh)]

    def score_tiles(js, masked):
        for h, sl in enumerate(heads):
            mx = None if masked else mx_ref[h]
            for j in js:
                s = _dot_nt(q_ref[0, :, sl], k_ref[0, pl.ds(pl.multiple_of(j * tk, tk), tk), sl])
                if masked:
                    s = jnp.where(col_minus_row <= thr, s, NEG_INF)
                s_ref[h, j] = s
                mx = _lane_block_max(s[:, :LANE] if mx is None else mx, s)
            mx_ref[h] = mx

    score_tiles((tail,), True)
    _run_by_pairs(tail, lambda js: score_tiles(js, False))
    m = [jnp.max(mx_ref[h], axis=-1, keepdims=True) for h in range(nh)]

    def absorb(js, first):
        for h, sl in enumerate(heads):
            pv = _tree_sum([_dot(jnp.exp(s_ref[h, j] - m[h]).astype(BF16),
                                 v_ref[0, pl.ds(pl.multiple_of(j * tk, tk), tk), sl]) for j in js])
            acc_ref[h] = pv if first else acc_ref[h] + pv

    absorb((tail,), True)
    _run_by_pairs(tail, lambda js: absorb(js, False))
    for p in range(nh // 2):
        o = jnp.where(lane < MLA_V, _flash_finish(acc_ref[2 * p]),
                      pltpu.roll(_flash_finish(acc_ref[2 * p + 1]), MLA_V, 1))
        o_ref[0, :, p * LANE:(p + 1) * LANE] = o.astype(o_ref.dtype)


def _mla_attention(qm, km, vm):
    B, S, _ = qm.shape
    t = MLA_TQ
    nh = MLA_HEADS_PER_STEP
    ngrp = MLA_HEADS // nh
    return pl.pallas_call(
        functools.partial(_mla_kernel, tq=t, tk=MLA_TK, nh=nh),
        grid=(B, ngrp, S // t),
        in_specs=[
            pl.BlockSpec((1, t, nh * LANE), lambda b, p, i: (b, i, p)),
            pl.BlockSpec((1, S, nh * LANE), lambda b, p, i: (b, 0, p)),
            pl.BlockSpec((1, S, nh * LANE), lambda b, p, i: (b, 0, p)),
        ],
        out_specs=pl.BlockSpec((1, t, nh * MLA_V), lambda b, p, i: (b, i, p)),
        out_shape=jax.ShapeDtypeStruct((B, S, MLA_HEADS * MLA_V), BF16),
        scratch_shapes=[pltpu.VMEM((nh, S // MLA_TK, t, MLA_TK), F32),
                        pltpu.VMEM((nh, t, LANE), F32), pltpu.VMEM((nh, t, LANE), F32)],
        compiler_params=_cparams(("parallel", "parallel", "arbitrary")),
        name="mla_attn",
    )(qm, km, vm)


def _stack_heads(q, lane):
    first = _first_head(lane)
    parts = []
    for h in range(NSA_REP):
        blk = q[:, (h // 2) * LANE:(h // 2 + 1) * LANE]
        parts.append(jnp.where(first if h % 2 == 0 else ~first, blk, jnp.zeros_like(blk)))
    return parts


def _pair_lanes(lane, a, b):
    return jnp.where(lane < NSA_HD, a, b)


def _tree_sum(terms):
    while len(terms) > 1:
        terms = [a + b for a, b in zip(terms[0::2], terms[1::2])] + (terms[-1:] if len(terms) % 2 else [])
    return terms[0]


def _nsa_cmp_kernel(q_ref, kc_ref, vc_ref, ovt_ref, ocmp_ref, bias_ref, *, t, n_slc):
    i = pl.program_id(1)
    gw = NSA_REP * NSA_HD
    lane = lax.broadcasted_iota(jnp.int32, (t, LANE), 1)
    row = lax.broadcasted_iota(jnp.int32, (t, LANE), 0)
    visible = lane * CMP_STRIDE + (CMP_LEN - 1) <= i * t + row
    blk = lax.broadcasted_iota(jnp.int32, (n_slc, t), 0)
    cur = (i * t + lax.broadcasted_iota(jnp.int32, (n_slc, t), 1)) // SLC_LEN
    forced = (blk == 0) | (blk == cur) | (blk == cur - 1)
    for g in range(NSA_GROUPS):
        kc = kc_ref[0, :, g * LANE:(g + 1) * LANE]
        vc = vc_ref[0, :, g * LANE:(g + 1) * LANE]
        psum = jnp.zeros((t, LANE), F32)
        outs = []
        for qh in _stack_heads(q_ref[0, :, g * gw:(g + 1) * gw], lane):
            s = jnp.where(visible, _dot_nt(qh, kc), NEG_INF)
            m = jnp.max(s, axis=-1, keepdims=True)
            e = jnp.where(visible, jnp.exp(s - m), 0.0)
            p = e / jnp.maximum(jnp.sum(e, axis=-1, keepdims=True), 1e-30)
            psum = psum + p
            outs.append(_dot(p.astype(BF16), vc))
        ocmp_ref[0, :, g * gw:g * gw + LANE] = _pair_lanes(lane, outs[0], outs[1])
        ocmp_ref[0, :, g * gw + LANE:(g + 1) * gw] = _pair_lanes(lane, outs[2], outs[3])
        hi = psum.astype(BF16)
        lo = (psum - hi.astype(F32)).astype(BF16)
        imp = (_dot_nt(ovt_ref[...], hi) + _dot_nt(ovt_ref[...], lo))[:n_slc]
        score = jnp.where(blk > cur, -1.0, jnp.where(forced, FORCE_SCORE, imp))
        terms = []
        for c in range(n_slc):
            sc = score[c:c + 1, :]
            before = (sc > score) | ((sc == score) & (blk > c))
            terms.append(jnp.where(before, 1.0, 0.0))
        bias_t = jnp.where(_tree_sum(terms) < float(min(SLC_TOPK, n_slc)), 0.0, NEG_INF)
        bias_t = jnp.concatenate([bias_t, jnp.full((LANE - n_slc, t), NEG_INF, F32)], axis=0)
        bias_ref[0, g] = bias_t.T.astype(BF16)


def _nsa_cmp(qn, kc, vc, overlap):
    B, S, _ = qn.shape
    t = CMP_T
    nkv = NSA_GROUPS * LANE
    return pl.pallas_call(
        functools.partial(_nsa_cmp_kernel, t=t, n_slc=S // SLC_LEN),
        grid=(B, S // t),
        in_specs=[
            pl.BlockSpec((1, t, NSA_HEADS * NSA_HD), lambda b, i: (b, i, 0)),
            pl.BlockSpec((1, kc.shape[1], nkv), lambda b, i: (b, 0, 0)),
            pl.BlockSpec((1, vc.shape[1], nkv), lambda b, i: (b, 0, 0)),
            pl.BlockSpec((LANE, LANE), lambda b, i: (0, 0)),
        ],
        out_specs=[pl.BlockSpec((1, t, NSA_HEADS * NSA_HD), lambda b, i: (b, i, 0)),
                   pl.BlockSpec((1, NSA_GROUPS, t, LANE), lambda b, i: (b, 0, i, 0))],
        out_shape=[jax.ShapeDtypeStruct((B, S, NSA_HEADS * NSA_HD), F32),
                   jax.ShapeDtypeStruct((B, NSA_GROUPS, S, LANE), BF16)],
        compiler_params=_cparams(("parallel", "parallel")),
        name="nsa_cmp",
    )(qn, kc, vc, overlap)


def _nsa_attn_kernel(q_ref, sk_ref, sv_ref, wk_ref, wv_ref, bias_ref, blk_ref, ocmp_ref, gate_ref,
                     o_ref, s_ref, mix_ref, mx_ref, acc_ref, *, t, tk):
    i = pl.program_id(1)
    r4 = NSA_REP * t
    gw = NSA_REP * NSA_HD
    groups = range(NSA_GROUPS)
    lane = lax.broadcasted_iota(jnp.int32, (t, LANE), 1)
    gl = [slice(g * LANE, (g + 1) * LANE) for g in groups]
    q4 = [jnp.concatenate(_stack_heads(q_ref[0, :, g * gw:(g + 1) * gw], lane), axis=0) for g in groups]
    q4b = [jnp.concatenate([q4[g], jnp.concatenate([bias_ref[0, g]] * NSA_REP, axis=0)], axis=1)
           for g in groups]

    rmc = ((lax.broadcasted_iota(jnp.int32, (r4, t), 0) & (t - 1))
           - lax.broadcasted_iota(jnp.int32, (r4, t), 1))

    def mask_blocks(s, visible):
        return jnp.concatenate([jnp.where(visible(c), s[:, c * t:(c + 1) * t], NEG_INF)
                                for c in range(s.shape[1] // t)], axis=1)

    tail = (i * t) // tk
    thr = i * t - tail * tk

    def score_tiles(js, masked):
        for g in groups:
            mx = None if masked else mx_ref[g]
            for j in js:
                off = pl.multiple_of(j * tk, tk)
                kt = jnp.concatenate([sk_ref[0, pl.ds(off, tk), gl[g]], blk_ref[pl.ds(off, tk), :]], axis=1)
                s = _dot_nt(q4b[g], kt)
                if masked:
                    s = mask_blocks(s, lambda c: rmc >= c * t - thr)
                s_ref[g, j] = s
                mx = _lane_block_max(s[:, :LANE] if mx is None else mx, s)
            mx_ref[g] = mx

    score_tiles((tail,), True)
    _run_by_pairs(tail, lambda js: score_tiles(js, False))
    m_s = [jnp.max(mx_ref[g], axis=-1, keepdims=True) for g in groups]

    def gate_pair(g, br, p2):
        c0 = g * LANE + br * NSA_REP + 2 * p2
        return _pair_lanes(lane, gate_ref[0, :, c0:c0 + 1], gate_ref[0, :, c0 + 1:c0 + 2])

    def head_pair(o, p2):
        a, b = 2 * p2, 2 * p2 + 1
        return _pair_lanes(lane, o[a * t:(a + 1) * t], pltpu.roll(o[b * t:(b + 1) * t], NSA_HD, 1))

    wkeys = WIN + t
    start = pl.multiple_of(jnp.maximum(i * t - WIN, 0), t)
    lead = i * t - start

    def in_window(c):
        return pltpu.bitcast(rmc + (lead - c * t), jnp.uint32) < jnp.uint32(WIN)

    for g in groups:
        s = mask_blocks(_dot_nt(q4[g], wk_ref[0, pl.ds(start, wkeys), gl[g]]), in_window)
        p = jnp.exp(s - jnp.max(s, axis=-1, keepdims=True)).astype(BF16)
        o_w = _flash_finish(_dot(p, wv_ref[0, pl.ds(start, wkeys), gl[g]]))
        for p2 in range(NSA_REP // 2):
            cols = slice(g * gw + p2 * LANE, g * gw + (p2 + 1) * LANE)
            mix_ref[:, cols] = gate_pair(g, 0, p2) * ocmp_ref[0, :, cols] + gate_pair(g, 2, p2) * head_pair(o_w, p2)

    def absorb(js, first):
        for g in groups:
            pv = _tree_sum([_dot(jnp.exp(s_ref[g, j] - m_s[g]).astype(BF16),
                                 sv_ref[0, pl.ds(pl.multiple_of(j * tk, tk), tk), gl[g]]) for j in js])
            acc_ref[g] = pv if first else acc_ref[g] + pv

    absorb((tail,), True)
    _run_by_pairs(tail, lambda js: absorb(js, False))
    for g in groups:
        o_s = _flash_finish(acc_ref[g])
        for p2 in range(NSA_REP // 2):
            cols = slice(g * gw + p2 * LANE, g * gw + (p2 + 1) * LANE)
            o_ref[0, :, cols] = (mix_ref[:, cols] + gate_pair(g, 1, p2) * head_pair(o_s, p2)).astype(o_ref.dtype)


def _nsa_attention(qn, sk, sv, wk, wv, bias, blk_onehot, ocmp, gates):
    B, S, _ = qn.shape
    t = ATT_T
    hw = NSA_HEADS * NSA_HD
    nkv = NSA_GROUPS * LANE
    kv = pl.BlockSpec((1, S, nkv), lambda b, i: (b, 0, 0))
    return pl.pallas_call(
        functools.partial(_nsa_attn_kernel, t=t, tk=NSA_TK),
        grid=(B, S // t),
        in_specs=[
            pl.BlockSpec((1, t, hw), lambda b, i: (b, i, 0)),
            kv, kv, kv, kv,
            pl.BlockSpec((1, NSA_GROUPS, t, LANE), lambda b, i: (b, 0, i, 0)),
            pl.BlockSpec((S, LANE), lambda b, i: (0, 0)),
            pl.BlockSpec((1, t, hw), lambda b, i: (b, i, 0)),
            pl.BlockSpec((1, t, nkv), lambda b, i: (b, i, 0)),
        ],
        out_specs=pl.BlockSpec((1, t, hw), lambda b, i: (b, i, 0)),
        out_shape=jax.ShapeDtypeStruct((B, S, hw), BF16),
        scratch_shapes=[pltpu.VMEM((NSA_GROUPS, S // NSA_TK, NSA_REP * t, NSA_TK), F32),
                        pltpu.VMEM((t, hw), F32),
                        pltpu.VMEM((NSA_GROUPS, NSA_REP * t, LANE), F32),
                        pltpu.VMEM((NSA_GROUPS, NSA_REP * t, LANE), F32)],
        compiler_params=_cparams(("parallel", "arbitrary")),
        name="nsa_attn",
    )(qn, sk, sv, wk, wv, bias, blk_onehot, ocmp, gates)


def _merge_kernel(x_ref, g_ref, osb_ref, omla_ref, onsa_ref, wg_ref, psb_ref, pmla_ref, pnsa_ref,
                  wout_ref, o_ref):
    x = x_ref[...]
    h = _rms(x, g_ref[...]).astype(BF16)
    y = jnp.zeros(x.shape, F32)
    for br, (o_r, p_r) in enumerate(((osb_ref, psb_ref), (omla_ref, pmla_ref), (onsa_ref, pnsa_ref))):
        gate = jax.nn.sigmoid(_dot(h, wg_ref[:, br * D_MODEL:(br + 1) * D_MODEL]))
        y = y + gate * _dot(o_r[...], p_r[...])
    o_ref[...] = x + _dot(y.astype(BF16), wout_ref[...])


def _merge(x, g, osb, omla, onsa, w):
    T, D = x.shape
    tm = MERGE_TM
    tok = lambda width: pl.BlockSpec((tm, width), lambda i: (i, 0))
    full = lambda shape: pl.BlockSpec(shape, lambda i: (0,) * len(shape))
    return pl.pallas_call(
        _merge_kernel,
        grid=(T // tm,),
        in_specs=[tok(D), full((1, D)), tok(osb.shape[1]), tok(omla.shape[1]), tok(onsa.shape[1]),
                  full((D, N_BRANCH * D)), full(w["proj_sb"].shape), full(w["proj_mla"].shape),
                  full(w["proj_nsa"].shape), full((D, D))],
        out_specs=tok(D),
        out_shape=jax.ShapeDtypeStruct((T, D), F32),
        compiler_params=_cparams(("parallel",)),
        name="merge",
    )(x, g, osb, omla, onsa, w["w_gate"], w["proj_sb"], w["proj_mla"], w["proj_nsa"], w["w_out"])


def _cos_sin(pos, d):
    half = d // 2
    inv = jnp.exp(-math.log(ROPE_THETA) * jnp.arange(half, dtype=F32) * (2.0 / d))
    ang = pos.astype(F32)[:, None] * inv[None, :]
    return jnp.cos(ang), jnp.sin(ang)


def _pair_rope_tables(pos):
    cos, sin = _cos_sin(pos, NSA_HD)
    return jnp.concatenate([cos] * 4, axis=1), jnp.concatenate([-sin, -sin, sin, sin], axis=1)


def _mla_rope_tables(pos):
    cos, sin = _cos_sin(pos, MLA_ROPE)
    n, half = cos.shape
    one = lambda k: jnp.ones((n, k), F32)
    zero = lambda k: jnp.zeros((n, k), F32)
    rest = LANE // 2 - half
    c = jnp.concatenate([cos, one(rest), cos, one(MLA_NOPE - rest), zero(LANE - MLA_QK)], axis=1)
    s = jnp.concatenate([-sin, zero(rest), sin, zero(LANE // 2 - half)], axis=1)
    return c, s


def _perm_pair(a, b):
    h = NSA_HD // 2
    return jnp.concatenate([a[..., :h], b[..., :h], a[..., h:], b[..., h:]], axis=-1)


def _perm_mla(nope, rope):
    h = MLA_ROPE // 2
    rest = LANE // 2 - h
    pad = jnp.zeros(nope.shape[:-1] + (LANE - MLA_QK,), nope.dtype)
    return jnp.concatenate([rope[..., :h], nope[..., :rest], rope[..., h:], nope[..., rest:], pad], axis=-1)


def _tables(S):
    pos = jnp.arange(S, dtype=jnp.int32)
    nchunk = S // CMP_STRIDE
    ends = jnp.arange(nchunk, dtype=jnp.int32) * CMP_STRIDE + (CMP_LEN - 1)
    t = {}
    t["nsa_c"], t["nsa_s"] = _pair_rope_tables(pos)
    t["cmp_c"], t["cmp_s"] = _pair_rope_tables(ends)
    t["mla_c"], t["mla_s"] = _mla_rope_tables(pos)
    t["blk_onehot"] = (pos[:, None] // SLC_LEN == jnp.arange(LANE, dtype=jnp.int32)[None, :]).astype(BF16)
    n_cmp = (S - CMP_LEN) // CMP_STRIDE + 1
    n_slc = S // SLC_LEN
    c0 = np.arange(n_cmp)[:, None] * CMP_STRIDE
    s0 = np.arange(n_slc)[None, :] * SLC_LEN
    ov = np.clip(np.minimum(c0 + CMP_LEN, s0 + SLC_LEN) - np.maximum(c0, s0), 0, None) / CMP_LEN
    full = np.zeros((LANE, LANE), np.float32)
    full[:n_slc, :n_cmp] = ov.T
    t["overlap"] = jnp.asarray(full, BF16)
    return t


def _pad_cols(w, width):
    return jnp.pad(w, ((0, 0), (0, width - w.shape[1])))


def _dup_groups(w, permuted):
    parts = []
    for g in range(NSA_GROUPS):
        blk = w[:, g * NSA_HD:(g + 1) * NSA_HD]
        parts.append(_perm_pair(blk, blk) if permuted else jnp.concatenate([blk, blk], axis=1))
    return jnp.concatenate(parts, axis=1)


def _blockdiag2(a):
    z = jnp.zeros_like(a)
    return jnp.concatenate([jnp.concatenate([a, z], axis=1), jnp.concatenate([z, a], axis=1)], axis=0)


def _layer_weights(l, p):
    w = {}
    row = lambda v: v[l][None, :].astype(F32)
    offs = np.concatenate([[0], np.cumsum(IN_SIZES)])
    cols = [p["w_in"][l][:, offs[k]:offs[k + 1]] for k in range(len(IN_SIZES))]
    (sb_q, sb_k, sb_v, cq, ckv, kr, nq, ck, cv, sk, sv, wk, wv, ngate, mgate) = cols
    sb_scale = 1.0 / math.sqrt(SB_HD)
    w["w_sb"] = jnp.concatenate([sb_q * sb_scale, sb_k, sb_v], axis=1).astype(BF16)
    kr_blk = _perm_mla(jnp.zeros((D_MODEL, MLA_NOPE), kr.dtype), kr)
    nq = jnp.concatenate([_perm_pair(nq[:, b * LANE:b * LANE + NSA_HD], nq[:, b * LANE + NSA_HD:(b + 1) * LANE])
                          for b in range(NSA_HEADS * NSA_HD // LANE)], axis=1)
    ng = ngate.reshape(D_MODEL, N_BRANCH, NSA_GROUPS, NSA_REP)
    ng = jnp.concatenate([_pad_cols(ng[:, :, g, :].reshape(D_MODEL, N_BRANCH * NSA_REP), LANE)
                          for g in range(NSA_GROUPS)], axis=1)
    w["w_z2"] = jnp.concatenate([cq, ckv, kr_blk, nq, ck, cv, _dup_groups(sk, True), _dup_groups(sv, False),
                                 _dup_groups(wk, True), _dup_groups(wv, False), ng], axis=1).astype(BF16)
    w["w_gate"] = mgate.astype(BF16)
    w["mix_norm"] = row(p["mix_norm"])
    w["mla_q_norm"] = row(p["mla_q_norm"])
    w["mla_kv_norm"] = row(p["mla_kv_norm"])
    wuq = p["mla_w_uq"][l].reshape(MLA_Q_LORA, MLA_HEADS, MLA_QK)
    w["wuq"] = _perm_mla(wuq[:, :, :MLA_NOPE], wuq[:, :, MLA_NOPE:]).reshape(MLA_Q_LORA, -1).astype(BF16)
    wukv = p["mla_w_ukv"][l].reshape(MLA_KV_LORA, MLA_HEADS, MLA_NOPE + MLA_V)
    no_rope = jnp.zeros((MLA_KV_LORA, MLA_HEADS, MLA_ROPE), wukv.dtype)
    w["wuk"] = _perm_mla(wukv[:, :, :MLA_NOPE], no_rope).reshape(MLA_KV_LORA, -1).astype(BF16)
    w["wuv"] = jnp.pad(wukv[:, :, MLA_NOPE:], ((0, 0), (0, 0), (0, LANE - MLA_V))).reshape(MLA_KV_LORA, -1).astype(BF16)
    gq, gk = row(p["mla_qk_gain_q"]), row(p["mla_qk_gain_k"])
    w["mla_gq"] = _perm_mla(gq[:, :MLA_NOPE], gq[:, MLA_NOPE:])
    w["mla_gk"] = _perm_mla(gk[:, :MLA_NOPE], gk[:, MLA_NOPE:])
    w["nsa_qg"] = _perm_pair(row(p["nsa_q_gain"]), row(p["nsa_q_gain"]))
    w["nsa_kg"] = _perm_pair(row(p["nsa_k_gain"]), row(p["nsa_k_gain"]))
    w["pe_k"] = jnp.tile(p["cmp_pos_k"][l], (1, 2)).astype(F32)
    w["pe_v"] = jnp.tile(p["cmp_pos_v"][l], (1, 2)).astype(F32)
    for nm, w1, w2 in (("k", p["cmp_wk1"][l], p["cmp_wk2"][l]), ("v", p["cmp_wv1"][l], p["cmp_wv2"][l])):
        w1 = w1.reshape(CMP_LEN, NSA_HD, CMP_HIDDEN)
        bd = jax.vmap(_blockdiag2)(w1).astype(BF16)
        w["w1%s_lo" % nm] = bd[:CMP_STRIDE]
        w["w1%s_hi" % nm] = bd[CMP_STRIDE:]
        w["w2" + nm] = _blockdiag2(_perm_pair(w2, w2) if nm == "k" else jnp.concatenate([w2, w2], axis=1)).astype(BF16)
    for nm in ("proj_sb", "proj_mla", "proj_nsa", "w_out"):
        w[nm] = p[nm][l].astype(BF16)
    for nm in ("ffn1", "ffn2"):
        w[nm + "_norm"] = row(p[nm + "_norm"])
    return w


def kernel(x, ffn1_norm, ffn1_wi, ffn1_wo, mix_norm, w_in, mla_q_norm, mla_w_uq, mla_kv_norm, mla_w_ukv, mla_qk_gain_q, mla_qk_gain_k, nsa_q_gain, nsa_k_gain, cmp_pos_k, cmp_pos_v, cmp_wk1, cmp_wk2, cmp_wv1, cmp_wv2, proj_sb, proj_mla, proj_nsa, w_out, ffn2_norm, ffn2_wi, ffn2_wo):
    p = dict(ffn1_norm=ffn1_norm, ffn1_wi=ffn1_wi, ffn1_wo=ffn1_wo, mix_norm=mix_norm, w_in=w_in,
             mla_q_norm=mla_q_norm, mla_w_uq=mla_w_uq, mla_kv_norm=mla_kv_norm, mla_w_ukv=mla_w_ukv,
             mla_qk_gain_q=mla_qk_gain_q, mla_qk_gain_k=mla_qk_gain_k, nsa_q_gain=nsa_q_gain,
             nsa_k_gain=nsa_k_gain, cmp_pos_k=cmp_pos_k, cmp_pos_v=cmp_pos_v, cmp_wk1=cmp_wk1,
             cmp_wk2=cmp_wk2, cmp_wv1=cmp_wv1, cmp_wv2=cmp_wv2, proj_sb=proj_sb, proj_mla=proj_mla,
             proj_nsa=proj_nsa, w_out=w_out, ffn2_norm=ffn2_norm, ffn2_wi=ffn2_wi, ffn2_wo=ffn2_wo)
    B, S, D = x.shape
    assert S // CMP_STRIDE == LANE and S // SLC_LEN <= LANE and D == D_MODEL
    tabs = _tables(S)
    xt = x.reshape(B * S, D)
    for l in range(ffn1_wi.shape[0]):
        w = _layer_weights(l, p)
        xt = _ffn(xt, w["ffn1_norm"], ffn1_wi, ffn1_wo, l)
        zsb, z2 = _inproj(xt, w["mix_norm"], w["w_sb"], w["w_z2"])
        zsb = zsb.reshape(B, S, -1)
        z2 = z2.reshape(B, S, -1)
        qm, km, vm, qn, sk, sv, wk, wv, gates = _prep(z2, w, tabs)
        kc, vc = _compress(z2, w, tabs)
        o_sb = _sb_attention(zsb)
        o_mla = _mla_attention(qm, km, vm)
        o_cmp, bias = _nsa_cmp(qn, kc, vc, tabs["overlap"])
        o_nsa = _nsa_attention(qn, sk, sv, wk, wv, bias, tabs["blk_onehot"], o_cmp, gates)
        xt = _merge(xt, w["mix_norm"], o_sb.reshape(B * S, -1), o_mla.reshape(B * S, -1),
                    o_nsa.reshape(B * S, -1), w)
        xt = _ffn(xt, w["ffn2_norm"], ffn2_wi, ffn2_wo, l)
    return xt.reshape(B, S, D)
```

```python
import functools
import math

import numpy as np
import jax
import jax.numpy as jnp
from jax import lax
from jax.experimental import pallas as pl
from jax.experimental.pallas import tpu as pltpu

D_MODEL = 1024
D_FF = 2816
EPS = 1e-6
ROPE_THETA = 10000.0
N_BRANCH = 3
SB_HEADS = 8
SB_HD = 64
MLA_HEADS = 8
MLA_NOPE = 64
MLA_ROPE = 32
MLA_V = 64
MLA_Q_LORA = 256
MLA_KV_LORA = 128
MLA_QK = MLA_NOPE + MLA_ROPE
NSA_HEADS = 8
NSA_GROUPS = 2
NSA_REP = NSA_HEADS // NSA_GROUPS
NSA_HD = 64
CMP_LEN = 32
CMP_STRIDE = 16
CMP_HIDDEN = 128
SLC_LEN = 64
SLC_TOPK = 16
WIN = 512
FORCE_SCORE = 1e3
NEG_INF = -1e30
SB_W = SB_HEADS * SB_HD
NSA_KV_W = NSA_GROUPS * NSA_HD
IN_SIZES = (SB_W, SB_W, SB_W, MLA_Q_LORA, MLA_KV_LORA, MLA_ROPE, NSA_HEADS * NSA_HD,
            NSA_KV_W, NSA_KV_W, NSA_KV_W, NSA_KV_W, NSA_KV_W, NSA_KV_W,
            N_BRANCH * NSA_HEADS, N_BRANCH * D_MODEL)

LANE = 128
V7X_VMEM_LIMIT = 56 * 1024 * 1024

FFN_TM = 1024
FFN_TF = 256
PROJ_TM = 512
PREP_TS = 512
ATT_T = 128
NSA_TK = 512
CMP_T = 512
SB_T = 256
SB_HEADS_PER_STEP = 4
MLA_TQ = 512
MLA_HEADS_PER_STEP = 4
MLA_TK = 512
MERGE_TM = 512

Z2_CQ = 0
Z2_CKV = 256
Z2_KR = 384
Z2_NQ = 512
Z2_CK = 1024
Z2_CV = 1152
Z2_SK = 1280
Z2_SV = 1536
Z2_WK = 1792
Z2_WV = 2048
Z2_GATE = 2304
Z2_W = 2560

BF16 = jnp.bfloat16
F32 = jnp.float32


def _cparams(sem, vmem=V7X_VMEM_LIMIT):
    return pltpu.CompilerParams(dimension_semantics=sem, vmem_limit_bytes=vmem)


def _dot(a, b):
    return jnp.dot(a, b, preferred_element_type=F32)


def _dot_nt(a, b):
    return lax.dot_general(a, b, (((1,), (1,)), ((), ())), preferred_element_type=F32)


def _split_rows(dot, a, b):
    h = a.shape[0] // 2
    return jnp.concatenate([dot(a[:h], b), dot(a[h:], b)], axis=0)


def _rms(x, g):
    y = x * lax.rsqrt(jnp.mean(x * x, axis=-1, keepdims=True) + EPS)
    return y * g


def _ffn_kernel(x_ref, g_ref, wa_ref, wb_ref, wo_ref, o_ref, h_ref, acc_ref):
    j = pl.program_id(1)

    @pl.when(j == 0)
    def _():
        h_ref[...] = _rms(x_ref[...], g_ref[...]).astype(BF16)
        acc_ref[...] = jnp.zeros_like(acc_ref)

    h = h_ref[...]
    a = _dot(h, wa_ref[...].astype(BF16))
    b = _dot(h, wb_ref[...].astype(BF16))
    u = (jax.nn.silu(a) * b).astype(BF16)
    acc_ref[...] += _dot(u, wo_ref[...].astype(BF16))

    @pl.when(j == pl.num_programs(1) - 1)
    def _():
        o_ref[...] = x_ref[...] + 0.5 * acc_ref[...]


def _ffn(x, g, wi, wo, l):
    T, D = x.shape
    F = wo.shape[1]
    tm, tf = FFN_TM, FFN_TF
    nf = F // tf
    return pl.pallas_call(
        _ffn_kernel,
        grid=(T // tm, nf),
        in_specs=[
            pl.BlockSpec((tm, D), lambda i, j: (i, 0)),
            pl.BlockSpec((1, D), lambda i, j: (0, 0)),
            pl.BlockSpec((None, D, tf), lambda i, j: (l, 0, j)),
            pl.BlockSpec((None, D, tf), lambda i, j: (l, 0, j + nf)),
            pl.BlockSpec((None, tf, D), lambda i, j: (l, j, 0)),
        ],
        out_specs=pl.BlockSpec((tm, D), lambda i, j: (i, 0)),
        out_shape=jax.ShapeDtypeStruct((T, D), F32),
        scratch_shapes=[pltpu.VMEM((tm, D), BF16), pltpu.VMEM((tm, D), F32)],
        compiler_params=_cparams(("parallel", "arbitrary")),
        name="ffn",
    )(x, g, wi, wi, wo)


def _rope_mix(y, c, s):
    return y * c + pltpu.roll(y, LANE // 2, 1) * s


def _first_head(lane):
    return (lane & (NSA_HD // 2)) == 0


def _norm_rope_pair(x, gain, c, s, scale):
    first = _first_head(lax.broadcasted_iota(jnp.int32, x.shape, 1))
    sq = x * x
    ss_a = jnp.sum(jnp.where(first, sq, 0.0), axis=-1, keepdims=True)
    ss_b = jnp.sum(jnp.where(first, 0.0, sq), axis=-1, keepdims=True)
    r = jnp.where(first, lax.rsqrt(ss_a * (1.0 / NSA_HD) + EPS), lax.rsqrt(ss_b * (1.0 / NSA_HD) + EPS))
    y = _rope_mix(x * r * gain, c, s)
    return y * scale if scale != 1.0 else y


def _norm_rope_mla(x, gain, c, s, scale):
    ss = jnp.sum(x * x, axis=-1, keepdims=True)
    y = _rope_mix(x * lax.rsqrt(ss * (1.0 / MLA_QK) + EPS) * gain, c, s)
    return y * scale if scale != 1.0 else y


def _inproj_kernel(x_ref, g_ref, wa_ref, wb_ref, oa_ref, ob_ref):
    h = _rms(x_ref[...], g_ref[...]).astype(BF16)
    oa_ref[...] = _dot(h, wa_ref[...]).astype(oa_ref.dtype)
    ob_ref[...] = _dot(h, wb_ref[...]).astype(ob_ref.dtype)


def _inproj(x, g, wa, wb):
    T, D = x.shape
    tm = PROJ_TM
    na, nb = wa.shape[1], wb.shape[1]
    return pl.pallas_call(
        _inproj_kernel,
        grid=(T // tm,),
        in_specs=[
            pl.BlockSpec((tm, D), lambda i: (i, 0)),
            pl.BlockSpec((1, D), lambda i: (0, 0)),
            pl.BlockSpec((D, na), lambda i: (0, 0)),
            pl.BlockSpec((D, nb), lambda i: (0, 0)),
        ],
        out_specs=[pl.BlockSpec((tm, na), lambda i: (i, 0)),
                   pl.BlockSpec((tm, nb), lambda i: (i, 0))],
        out_shape=[jax.ShapeDtypeStruct((T, na), BF16), jax.ShapeDtypeStruct((T, nb), F32)],
        compiler_params=_cparams(("parallel",)),
        name="inproj",
    )(x, g, wa, wb)


def _prep_kernel(z_ref, qn_ref, kvn_ref, wuq_ref, wuk_ref, wuv_ref, gq_ref, gk_ref,
                 mc_ref, ms_ref, nqg_ref, nkg_ref, nc_ref, ns_ref,
                 qm_ref, km_ref, vm_ref, qn_out, sk_out, sv_out, wk_out, wv_out, gate_out):
    zs = lambda start, width: z_ref[0, :, start:start + width]
    mc, ms = mc_ref[...], ms_ref[...]
    nc, ns = nc_ref[...], ns_ref[...]
    cq = _rms(zs(Z2_CQ, MLA_Q_LORA), qn_ref[...]).astype(BF16)
    ckv = _rms(zs(Z2_CKV, MLA_KV_LORA), kvn_ref[...]).astype(BF16)
    kr = zs(Z2_KR, LANE)
    q = _dot(cq, wuq_ref[...])
    kn = _dot(ckv, wuk_ref[...])
    v = _dot(ckv, wuv_ref[...])
    vlane = lax.broadcasted_iota(jnp.int32, (v.shape[0], LANE), 1)
    q_scale = 1.0 / math.sqrt(MLA_QK)
    for h in range(MLA_HEADS):
        sl = slice(h * LANE, (h + 1) * LANE)
        vm_ref[0, :, sl] = jnp.where(vlane < MLA_V, v[:, sl], 1.0).astype(BF16)
        qm_ref[0, :, sl] = _norm_rope_mla(q[:, sl], gq_ref[...], mc, ms, q_scale).astype(BF16)
        km_ref[0, :, sl] = _norm_rope_mla(kn[:, sl] + kr, gk_ref[...], mc, ms, 1.0).astype(BF16)
    nq_scale = 1.0 / math.sqrt(NSA_HD)
    for b in range(NSA_HEADS * NSA_HD // LANE):
        sl = slice(b * LANE, (b + 1) * LANE)
        qn_out[0, :, sl] = _norm_rope_pair(zs(Z2_NQ + b * LANE, LANE), nqg_ref[...], nc, ns,
                                           nq_scale).astype(BF16)
    for b in range(2):
        sl = slice(b * LANE, (b + 1) * LANE)
        sk_out[0, :, sl] = _norm_rope_pair(zs(Z2_SK + b * LANE, LANE), nkg_ref[...], nc, ns, 1.0).astype(BF16)
        wk_out[0, :, sl] = _norm_rope_pair(zs(Z2_WK + b * LANE, LANE), nkg_ref[...], nc, ns, 1.0).astype(BF16)
    glane = lax.broadcasted_iota(jnp.int32, (z_ref.shape[1], 2 * LANE), 1) & (LANE - 1)
    sv_out[0] = jnp.where(glane < NSA_HD, zs(Z2_SV, 2 * LANE), 1.0).astype(BF16)
    wv_out[0] = jnp.where(glane < NSA_HD, zs(Z2_WV, 2 * LANE), 1.0).astype(BF16)
    gate_out[0] = jax.nn.sigmoid(zs(Z2_GATE, 2 * LANE))


def _prep(z2, w, tabs):
    B, S, _ = z2.shape
    ts = PREP_TS
    full = lambda shape: pl.BlockSpec(shape, lambda b, i: (0,) * len(shape))
    tab = pl.BlockSpec((ts, LANE), lambda b, i: (i, 0))
    tok = lambda width: pl.BlockSpec((1, ts, width), lambda b, i: (b, i, 0))
    out_w = (MLA_HEADS * LANE, MLA_HEADS * LANE, MLA_HEADS * LANE, NSA_HEADS * NSA_HD,
             2 * LANE, 2 * LANE, 2 * LANE, 2 * LANE, 2 * LANE)
    out_dt = (BF16,) * 8 + (F32,)
    return pl.pallas_call(
        _prep_kernel,
        grid=(B, S // ts),
        in_specs=[
            tok(Z2_W),
            full((1, MLA_Q_LORA)), full((1, MLA_KV_LORA)),
            full((MLA_Q_LORA, MLA_HEADS * LANE)), full((MLA_KV_LORA, MLA_HEADS * LANE)),
            full((MLA_KV_LORA, MLA_HEADS * LANE)),
            full((1, LANE)), full((1, LANE)),
            tab, tab,
            full((1, LANE)), full((1, LANE)),
            tab, tab,
        ],
        out_specs=[tok(wd) for wd in out_w],
        out_shape=[jax.ShapeDtypeStruct((B, S, wd), dt) for wd, dt in zip(out_w, out_dt)],
        compiler_params=_cparams(("parallel", "parallel")),
        name="prep",
    )(z2, w["mla_q_norm"], w["mla_kv_norm"], w["wuq"], w["wuk"], w["wuv"], w["mla_gq"], w["mla_gk"],
      tabs["mla_c"], tabs["mla_s"], w["nsa_qg"], w["nsa_kg"], tabs["nsa_c"], tabs["nsa_s"])


def _compress_kernel(ck_ref, cv_ref, pek_ref, pev_ref, w1kl_ref, w1kh_ref, w2k_ref,
                     w1vl_ref, w1vh_ref, w2v_ref, kg_ref, c_ref, s_ref,
                     kc_ref, vc_ref, *, nchunk):
    def branch(t_ref, pe_ref, w1l_ref, w1h_ref, w2_ref):
        lo = jnp.zeros((nchunk, 2 * CMP_HIDDEN), F32)
        hi = jnp.zeros((nchunk, 2 * CMP_HIDDEN), F32)
        for l in range(CMP_STRIDE):
            x = t_ref[0, pl.ds(l, nchunk, stride=CMP_STRIDE), :]
            lo = lo + _dot((x + pe_ref[l:l + 1, :]).astype(BF16), w1l_ref[l])
            hi = hi + _dot((x + pe_ref[CMP_STRIDE + l:CMP_STRIDE + l + 1, :]).astype(BF16), w1h_ref[l])
        pre = lo + pltpu.roll(hi, nchunk - 1, 0)
        return _dot(jax.nn.gelu(pre).astype(BF16), w2_ref[...])

    kc = branch(ck_ref, pek_ref, w1kl_ref, w1kh_ref, w2k_ref)
    vc = branch(cv_ref, pev_ref, w1vl_ref, w1vh_ref, w2v_ref)
    for b in range(2):
        sl = slice(b * LANE, (b + 1) * LANE)
        kc_ref[0, :, sl] = _norm_rope_pair(kc[:, sl], kg_ref[...], c_ref[...], s_ref[...], 1.0).astype(BF16)
    vc_ref[0] = vc.astype(BF16)


def _compress(z2, w, tabs):
    B, S, _ = z2.shape
    nchunk = S // CMP_STRIDE
    full = lambda shape: pl.BlockSpec(shape, lambda b: (0,) * len(shape))
    tab = full((nchunk, LANE))
    return pl.pallas_call(
        functools.partial(_compress_kernel, nchunk=nchunk),
        grid=(B,),
        in_specs=[
            pl.BlockSpec((1, S, LANE), lambda b: (b, 0, Z2_CK // LANE)),
            pl.BlockSpec((1, S, LANE), lambda b: (b, 0, Z2_CV // LANE)),
            full((CMP_LEN, LANE)), full((CMP_LEN, LANE)),
            full((CMP_STRIDE, LANE, 2 * CMP_HIDDEN)), full((CMP_STRIDE, LANE, 2 * CMP_HIDDEN)),
            full((2 * CMP_HIDDEN, 2 * LANE)),
            full((CMP_STRIDE, LANE, 2 * CMP_HIDDEN)), full((CMP_STRIDE, LANE, 2 * CMP_HIDDEN)),
            full((2 * CMP_HIDDEN, 2 * LANE)),
            full((1, LANE)), tab, tab,
        ],
        out_specs=[pl.BlockSpec((1, nchunk, 2 * LANE), lambda b: (b, 0, 0))] * 2,
        out_shape=[jax.ShapeDtypeStruct((B, nchunk, 2 * LANE), BF16)] * 2,
        compiler_params=_cparams(("parallel",)),
        name="compress",
    )(z2, z2, w["pe_k"], w["pe_v"], w["w1k_lo"], w["w1k_hi"], w["w2k"],
      w["w1v_lo"], w["w1v_hi"], w["w2v"], w["nsa_kg"], tabs["cmp_c"], tabs["cmp_s"])


def _sb_kernel(q_ref, k_ref, v_ref, o_ref, acc_ref, carry_ref, *, t, nh):
    i = pl.program_id(2)
    w = nh * SB_HD
    lane = lax.broadcasted_iota(jnp.int32, (t, w), 1)
    row = lax.broadcasted_iota(jnp.int32, (t, t), 0)
    col = lax.broadcasted_iota(jnp.int32, (t, t), 1)
    upper = jnp.where(row > col, 1.0, 0.0).astype(BF16)
    strict = (lax.broadcasted_iota(jnp.int32, (nh * t, t), 1)
              < (lax.broadcasted_iota(jnp.int32, (nh * t, t), 0) & (t - 1)))
    q = q_ref[0]
    own = [(lane >= SB_HD * h) & (lane < SB_HD * (h + 1)) for h in range(nh)]
    qs = jnp.concatenate([jnp.where(own[h], q, jnp.zeros_like(q)) for h in range(nh)], axis=0)

    def tile(j, carry, masked):
        off = pl.multiple_of(j * t, t)
        z = _split_rows(_dot_nt, qs, k_ref[0, pl.ds(off, t), :])
        lg = jnp.log(1.0 + jnp.exp(-jnp.abs(z)))
        logsig = jnp.minimum(z, 0.0) - lg
        lom = logsig - z
        if masked:
            lom = jnp.where(strict, lom, 0.0)
        suffix = _split_rows(_dot, lom.astype(BF16), upper)
        wgt = jnp.exp(logsig + suffix + carry)
        if masked:
            wgt = jnp.where(strict, wgt, 0.0)
        pv = _split_rows(_dot, wgt.astype(BF16), v_ref[0, pl.ds(off, t), :])
        return carry + jnp.sum(lom, axis=-1, keepdims=True), pv

    carry, pv = tile(i, jnp.zeros((nh * t, 1), F32), True)
    acc_ref[...] = pv
    odd = i % 2

    @pl.when(odd == 1)
    def _():
        c1, pv1 = tile(i - 1, carry, False)
        acc_ref[...] += pv1
        carry_ref[...] = c1

    @pl.when(odd == 0)
    def _():
        carry_ref[...] = carry

    def pair(s, c):
        j = i - 1 - odd - 2 * s
        c, pv_a = tile(j, c, False)
        c, pv_b = tile(j - 1, c, False)
        acc_ref[...] += pv_a + pv_b
        return c

    lax.fori_loop(0, i // 2, pair, carry_ref[...])
    out = acc_ref[0:t, :]
    for h in range(1, nh):
        out = jnp.where(own[h], acc_ref[h * t:(h + 1) * t, :], out)
    o_ref[0] = out.astype(o_ref.dtype)


def _sb_attention(zsb):
    B, S, _ = zsb.shape
    t = SB_T
    nh = SB_HEADS_PER_STEP
    w = nh * SB_HD
    npair = SB_W // w
    return pl.pallas_call(
        functools.partial(_sb_kernel, t=t, nh=nh),
        grid=(B, npair, S // t),
        in_specs=[
            pl.BlockSpec((1, t, w), lambda b, p, i: (b, i, p)),
            pl.BlockSpec((1, S, w), lambda b, p, i: (b, 0, npair + p)),
            pl.BlockSpec((1, S, w), lambda b, p, i: (b, 0, 2 * npair + p)),
        ],
        out_specs=pl.BlockSpec((1, t, w), lambda b, p, i: (b, i, p)),
        out_shape=jax.ShapeDtypeStruct((B, S, SB_W), BF16),
        scratch_shapes=[pltpu.VMEM((nh * t, w), F32), pltpu.VMEM((nh * t, 1), F32)],
        compiler_params=_cparams(("parallel", "parallel", "arbitrary")),
        name="sb_attn",
    )(zsb, zsb, zsb)


def _flash_finish(acc):
    return acc / jnp.maximum(pltpu.roll(acc, NSA_HD, 1), 1e-30)


def _run_by_pairs(n, body):
    odd = n % 2

    @pl.when(odd == 1)
    def _():
        body((0,))

    def pair(s, c):
        body((odd + 2 * s, odd + 2 * s + 1))
        return c

    lax.fori_loop(0, n // 2, pair, 0)


def _lane_block_max(m, s):
    for c in range(s.shape[1] // LANE):
        m = jnp.maximum(m, s[:, c * LANE:(c + 1) * LANE])
    return m


def _mla_kernel(q_ref, k_ref, v_ref, o_ref, s_ref, mx_ref, acc_ref, *, tq, tk, nh):
    i = pl.program_id(2)
    lane = lax.broadcasted_iota(jnp.int32, (tq, LANE), 1)
    col_minus_row = (lax.broadcasted_iota(jnp.int32, (tq, tk), 1)
                     - lax.broadcasted_iota(jnp.int32, (tq, tk), 0))
    tail = (i * tq) // tk
    thr = i * tq - tail * tk
    heads = [slice(h * LANE, (h + 1) * LANE) for h in range(nh)]

    def score_tiles(js, masked):
        for h, sl in enumerate(heads):
            mx = None if masked else mx_ref[h]
            for j in js:
                s = _dot_nt(q_ref[0, :, sl], k_ref[0, pl.ds(pl.multiple_of(j * tk, tk), tk), sl])
                if masked:
                    s = jnp.where(col_minus_row <= thr, s, NEG_INF)
                s_ref[h, j] = s
                mx = _lane_block_max(s[:, :LANE] if mx is None else mx, s)
            mx_ref[h] = mx

    score_tiles((tail,), True)
    _run_by_pairs(tail, lambda js: score_tiles(js, False))
    m = [jnp.max(mx_ref[h], axis=-1, keepdims=True) for h in range(nh)]

    def absorb(js, first):
        for h, sl in enumerate(heads):
            pv = _tree_sum([_dot(jnp.exp(s_ref[h, j] - m[h]).astype(BF16),
                                 v_ref[0, pl.ds(pl.multiple_of(j * tk, tk), tk), sl]) for j in js])
            acc_ref[h] = pv if first else acc_ref[h] + pv

    absorb((tail,), True)
    _run_by_pairs(tail, lambda js: absorb(js, False))
    for p in range(nh // 2):
        o = jnp.where(lane < MLA_V, _flash_finish(acc_ref[2 * p]),
                      pltpu.roll(_flash_finish(acc_ref[2 * p + 1]), MLA_V, 1))
        o_ref[0, :, p * LANE:(p + 1) * LANE] = o.astype(o_ref.dtype)


def _mla_attention(qm, km, vm):
    B, S, _ = qm.shape
    t = MLA_TQ
    nh = MLA_HEADS_PER_STEP
    ngrp = MLA_HEADS // nh
    return pl.pallas_call(
        functools.partial(_mla_kernel, tq=t, tk=MLA_TK, nh=nh),
        grid=(B, ngrp, S // t),
        in_specs=[
            pl.BlockSpec((1, t, nh * LANE), lambda b, p, i: (b, i, p)),
            pl.BlockSpec((1, S, nh * LANE), lambda b, p, i: (b, 0, p)),
            pl.BlockSpec((1, S, nh * LANE), lambda b, p, i: (b, 0, p)),
        ],
        out_specs=pl.BlockSpec((1, t, nh * MLA_V), lambda b, p, i: (b, i, p)),
        out_shape=jax.ShapeDtypeStruct((B, S, MLA_HEADS * MLA_V), BF16),
        scratch_shapes=[pltpu.VMEM((nh, S // MLA_TK, t, MLA_TK), F32),
                        pltpu.VMEM((nh, t, LANE), F32), pltpu.VMEM((nh, t, LANE), F32)],
        compiler_params=_cparams(("parallel", "parallel", "arbitrary")),
        name="mla_attn",
    )(qm, km, vm)


def _stack_heads(q, lane):
    first = _first_head(lane)
    parts = []
    for h in range(NSA_REP):
        blk = q[:, (h // 2) * LANE:(h // 2 + 1) * LANE]
        parts.append(jnp.where(first if h % 2 == 0 else ~first, blk, jnp.zeros_like(blk)))
    return parts


def _pair_lanes(lane, a, b):
    return jnp.where(lane < NSA_HD, a, b)


def _tree_sum(terms):
    while len(terms) > 1:
        terms = [a + b for a, b in zip(terms[0::2], terms[1::2])] + (terms[-1:] if len(terms) % 2 else [])
    return terms[0]


def _nsa_cmp_kernel(q_ref, kc_ref, vc_ref, ovt_ref, ocmp_ref, bias_ref, *, t, n_slc):
    i = pl.program_id(1)
    gw = NSA_REP * NSA_HD
    lane = lax.broadcasted_iota(jnp.int32, (t, LANE), 1)
    row = lax.broadcasted_iota(jnp.int32, (t, LANE), 0)
    visible = lane * CMP_STRIDE + (CMP_LEN - 1) <= i * t + row
    blk = lax.broadcasted_iota(jnp.int32, (n_slc, t), 0)
    cur = (i * t + lax.broadcasted_iota(jnp.int32, (n_slc, t), 1)) // SLC_LEN
    forced = (blk == 0) | (blk == cur) | (blk == cur - 1)
    for g in range(NSA_GROUPS):
        kc = kc_ref[0, :, g * LANE:(g + 1) * LANE]
        vc = vc_ref[0, :, g * LANE:(g + 1) * LANE]
        psum = jnp.zeros((t, LANE), F32)
        outs = []
        for qh in _stack_heads(q_ref[0, :, g * gw:(g + 1) * gw], lane):
            s = jnp.where(visible, _dot_nt(qh, kc), NEG_INF)
            m = jnp.max(s, axis=-1, keepdims=True)
            e = jnp.where(visible, jnp.exp(s - m), 0.0)
            p = e / jnp.maximum(jnp.sum(e, axis=-1, keepdims=True), 1e-30)
            psum = psum + p
            outs.append(_dot(p.astype(BF16), vc))
        ocmp_ref[0, :, g * gw:g * gw + LANE] = _pair_lanes(lane, outs[0], outs[1])
        ocmp_ref[0, :, g * gw + LANE:(g + 1) * gw] = _pair_lanes(lane, outs[2], outs[3])
        hi = psum.astype(BF16)
        lo = (psum - hi.astype(F32)).astype(BF16)
        imp = (_dot_nt(ovt_ref[...], hi) + _dot_nt(ovt_ref[...], lo))[:n_slc]
        score = jnp.where(blk > cur, -1.0, jnp.where(forced, FORCE_SCORE, imp))
        terms = []
        for c in range(n_slc):
            sc = score[c:c + 1, :]
            before = (sc > score) | ((sc == score) & (blk > c))
            terms.append(jnp.where(before, 1.0, 0.0))
        bias_t = jnp.where(_tree_sum(terms) < float(min(SLC_TOPK, n_slc)), 0.0, NEG_INF)
        bias_t = jnp.concatenate([bias_t, jnp.full((LANE - n_slc, t), NEG_INF, F32)], axis=0)
        bias_ref[0, g] = bias_t.T.astype(BF16)


def _nsa_cmp(qn, kc, vc, overlap):
    B, S, _ = qn.shape
    t = CMP_T
    nkv = NSA_GROUPS * LANE
    return pl.pallas_call(
        functools.partial(_nsa_cmp_kernel, t=t, n_slc=S // SLC_LEN),
        grid=(B, S // t),
        in_specs=[
            pl.BlockSpec((1, t, NSA_HEADS * NSA_HD), lambda b, i: (b, i, 0)),
            pl.BlockSpec((1, kc.shape[1], nkv), lambda b, i: (b, 0, 0)),
            pl.BlockSpec((1, vc.shape[1], nkv), lambda b, i: (b, 0, 0)),
            pl.BlockSpec((LANE, LANE), lambda b, i: (0, 0)),
        ],
        out_specs=[pl.BlockSpec((1, t, NSA_HEADS * NSA_HD), lambda b, i: (b, i, 0)),
                   pl.BlockSpec((1, NSA_GROUPS, t, LANE), lambda b, i: (b, 0, i, 0))],
        out_shape=[jax.ShapeDtypeStruct((B, S, NSA_HEADS * NSA_HD), F32),
                   jax.ShapeDtypeStruct((B, NSA_GROUPS, S, LANE), BF16)],
        compiler_params=_cparams(("parallel", "parallel")),
        name="nsa_cmp",
    )(qn, kc, vc, overlap)


def _nsa_attn_kernel(q_ref, sk_ref, sv_ref, wk_ref, wv_ref, bias_ref, blk_ref, ocmp_ref, gate_ref,
                     o_ref, s_ref, mix_ref, mx_ref, acc_ref, *, t, tk):
    i = pl.program_id(1)
    r4 = NSA_REP * t
    gw = NSA_REP * NSA_HD
    groups = range(NSA_GROUPS)
    lane = lax.broadcasted_iota(jnp.int32, (t, LANE), 1)
    gl = [slice(g * LANE, (g + 1) * LANE) for g in groups]
    q4 = [jnp.concatenate(_stack_heads(q_ref[0, :, g * gw:(g + 1) * gw], lane), axis=0) for g in groups]
    q4b = [jnp.concatenate([q4[g], jnp.concatenate([bias_ref[0, g]] * NSA_REP, axis=0)], axis=1)
           for g in groups]

    rmc = ((lax.broadcasted_iota(jnp.int32, (r4, t), 0) & (t - 1))
           - lax.broadcasted_iota(jnp.int32, (r4, t), 1))

    def mask_blocks(s, visible):
        return jnp.concatenate([jnp.where(visible(c), s[:, c * t:(c + 1) * t], NEG_INF)
                                for c in range(s.shape[1] // t)], axis=1)

    tail = (i * t) // tk
    thr = i * t - tail * tk

    def score_tiles(js, masked):
        s = {}
        for n, j in enumerate(js):
            off = pl.multiple_of(j * tk, tk)
            onehot = blk_ref[pl.ds(off, tk), :]
            for g in groups:
                s[g, n] = _dot_nt(q4b[g], jnp.concatenate([sk_ref[0, pl.ds(off, tk), gl[g]], onehot], axis=1))
        for g in groups:
            mx = None if masked else mx_ref[g]
            for n, j in enumerate(js):
                sj = s[g, n]
                if masked:
                    sj = mask_blocks(sj, lambda c: rmc >= c * t - thr)
                s_ref[g, j] = sj
                mx = _lane_block_max(sj[:, :LANE] if mx is None else mx, sj)
            mx_ref[g] = mx

    score_tiles((tail,), True)
    _run_by_pairs(tail, lambda js: score_tiles(js, False))
    m_s = [jnp.max(mx_ref[g], axis=-1, keepdims=True) for g in groups]

    def gate_pair(g, br, p2):
        c0 = g * LANE + br * NSA_REP + 2 * p2
        return _pair_lanes(lane, gate_ref[0, :, c0:c0 + 1], gate_ref[0, :, c0 + 1:c0 + 2])

    def head_pair(o, p2):
        a, b = 2 * p2, 2 * p2 + 1
        return _pair_lanes(lane, o[a * t:(a + 1) * t], pltpu.roll(o[b * t:(b + 1) * t], NSA_HD, 1))

    wkeys = WIN + t
    start = pl.multiple_of(jnp.maximum(i * t - WIN, 0), t)
    lead = i * t - start

    def in_window(c):
        return pltpu.bitcast(rmc + (lead - c * t), jnp.uint32) < jnp.uint32(WIN)

    s_w = [mask_blocks(_dot_nt(q4[g], wk_ref[0, pl.ds(start, wkeys), gl[g]]), in_window) for g in groups]
    p_w = [jnp.exp(s_w[g] - jnp.max(s_w[g], axis=-1, keepdims=True)).astype(BF16) for g in groups]
    pv_w = [_dot(p_w[g], wv_ref[0, pl.ds(start, wkeys), gl[g]]) for g in groups]
    for g in groups:
        o_w = _flash_finish(pv_w[g])
        for p2 in range(NSA_REP // 2):
            cols = slice(g * gw + p2 * LANE, g * gw + (p2 + 1) * LANE)
            mix_ref[:, cols] = gate_pair(g, 0, p2) * ocmp_ref[0, :, cols] + gate_pair(g, 2, p2) * head_pair(o_w, p2)

    def absorb(js, first):
        p = {(g, n): jnp.exp(s_ref[g, j] - m_s[g]).astype(BF16) for n, j in enumerate(js) for g in groups}
        for g in groups:
            pv = _tree_sum([_dot(p[g, n], sv_ref[0, pl.ds(pl.multiple_of(j * tk, tk), tk), gl[g]])
                            for n, j in enumerate(js)])
            acc_ref[g] = pv if first else acc_ref[g] + pv

    absorb((tail,), True)
    _run_by_pairs(tail, lambda js: absorb(js, False))
    for g in groups:
        o_s = _flash_finish(acc_ref[g])
        for p2 in range(NSA_REP // 2):
            cols = slice(g * gw + p2 * LANE, g * gw + (p2 + 1) * LANE)
            o_ref[0, :, cols] = (mix_ref[:, cols] + gate_pair(g, 1, p2) * head_pair(o_s, p2)).astype(o_ref.dtype)


def _nsa_attention(qn, sk, sv, wk, wv, bias, blk_onehot, ocmp, gates):
    B, S, _ = qn.shape
    t = ATT_T
    hw = NSA_HEADS * NSA_HD
    nkv = NSA_GROUPS * LANE
    kv = pl.BlockSpec((1, S, nkv), lambda b, i: (b, 0, 0))
    return pl.pallas_call(
        functools.partial(_nsa_attn_kernel, t=t, tk=NSA_TK),
        grid=(B, S // t),
        in_specs=[
            pl.BlockSpec((1, t, hw), lambda b, i: (b, i, 0)),
            kv, kv, kv, kv,
            pl.BlockSpec((1, NSA_GROUPS, t, LANE), lambda b, i: (b, 0, i, 0)),
            pl.BlockSpec((S, LANE), lambda b, i: (0, 0)),
            pl.BlockSpec((1, t, hw), lambda b, i: (b, i, 0)),
            pl.BlockSpec((1, t, nkv), lambda b, i: (b, i, 0)),
        ],
        out_specs=pl.BlockSpec((1, t, hw), lambda b, i: (b, i, 0)),
        out_shape=jax.ShapeDtypeStruct((B, S, hw), BF16),
        scratch_shapes=[pltpu.VMEM((NSA_GROUPS, S // NSA_TK, NSA_REP * t, NSA_TK), F32),
                        pltpu.VMEM((t, hw), F32),
                        pltpu.VMEM((NSA_GROUPS, NSA_REP * t, LANE), F32),
                        pltpu.VMEM((NSA_GROUPS, NSA_REP * t, LANE), F32)],
        compiler_params=_cparams(("parallel", "arbitrary")),
        name="nsa_attn",
    )(qn, sk, sv, wk, wv, bias, blk_onehot, ocmp, gates)


def _merge_kernel(x_ref, g_ref, osb_ref, omla_ref, onsa_ref, wg_ref, psb_ref, pmla_ref, pnsa_ref,
                  wout_ref, o_ref):
    x = x_ref[...]
    h = _rms(x, g_ref[...]).astype(BF16)
    y = jnp.zeros(x.shape, F32)
    for br, (o_r, p_r) in enumerate(((osb_ref, psb_ref), (omla_ref, pmla_ref), (onsa_ref, pnsa_ref))):
        gate = jax.nn.sigmoid(_dot(h, wg_ref[:, br * D_MODEL:(br + 1) * D_MODEL]))
        y = y + gate * _dot(o_r[...], p_r[...])
    o_ref[...] = x + _dot(y.astype(BF16), wout_ref[...])


def _merge(x, g, osb, omla, onsa, w):
    T, D = x.shape
    tm = MERGE_TM
    tok = lambda width: pl.BlockSpec((tm, width), lambda i: (i, 0))
    full = lambda shape: pl.BlockSpec(shape, lambda i: (0,) * len(shape))
    return pl.pallas_call(
        _merge_kernel,
        grid=(T // tm,),
        in_specs=[tok(D), full((1, D)), tok(osb.shape[1]), tok(omla.shape[1]), tok(onsa.shape[1]),
                  full((D, N_BRANCH * D)), full(w["proj_sb"].shape), full(w["proj_mla"].shape),
                  full(w["proj_nsa"].shape), full((D, D))],
        out_specs=tok(D),
        out_shape=jax.ShapeDtypeStruct((T, D), F32),
        compiler_params=_cparams(("parallel",)),
        name="merge",
    )(x, g, osb, omla, onsa, w["w_gate"], w["proj_sb"], w["proj_mla"], w["proj_nsa"], w["w_out"])


def _cos_sin(pos, d):
    half = d // 2
    inv = jnp.exp(-math.log(ROPE_THETA) * jnp.arange(half, dtype=F32) * (2.0 / d))
    ang = pos.astype(F32)[:, None] * inv[None, :]
    return jnp.cos(ang), jnp.sin(ang)


def _pair_rope_tables(pos):
    cos, sin = _cos_sin(pos, NSA_HD)
    return jnp.concatenate([cos] * 4, axis=1), jnp.concatenate([-sin, -sin, sin, sin], axis=1)


def _mla_rope_tables(pos):
    cos, sin = _cos_sin(pos, MLA_ROPE)
    n, half = cos.shape
    one = lambda k: jnp.ones((n, k), F32)
    zero = lambda k: jnp.zeros((n, k), F32)
    rest = LANE // 2 - half
    c = jnp.concatenate([cos, one(rest), cos, one(MLA_NOPE - rest), zero(LANE - MLA_QK)], axis=1)
    s = jnp.concatenate([-sin, zero(rest), sin, zero(LANE // 2 - half)], axis=1)
    return c, s


def _perm_pair(a, b):
    h = NSA_HD // 2
    return jnp.concatenate([a[..., :h], b[..., :h], a[..., h:], b[..., h:]], axis=-1)


def _perm_mla(nope, rope):
    h = MLA_ROPE // 2
    rest = LANE // 2 - h
    pad = jnp.zeros(nope.shape[:-1] + (LANE - MLA_QK,), nope.dtype)
    return jnp.concatenate([rope[..., :h], nope[..., :rest], rope[..., h:], nope[..., rest:], pad], axis=-1)


def _tables(S):
    pos = jnp.arange(S, dtype=jnp.int32)
    nchunk = S // CMP_STRIDE
    ends = jnp.arange(nchunk, dtype=jnp.int32) * CMP_STRIDE + (CMP_LEN - 1)
    t = {}
    t["nsa_c"], t["nsa_s"] = _pair_rope_tables(pos)
    t["cmp_c"], t["cmp_s"] = _pair_rope_tables(ends)
    t["mla_c"], t["mla_s"] = _mla_rope_tables(pos)
    t["blk_onehot"] = (pos[:, None] // SLC_LEN == jnp.arange(LANE, dtype=jnp.int32)[None, :]).astype(BF16)
    n_cmp = (S - CMP_LEN) // CMP_STRIDE + 1
    n_slc = S // SLC_LEN
    c0 = np.arange(n_cmp)[:, None] * CMP_STRIDE
    s0 = np.arange(n_slc)[None, :] * SLC_LEN
    ov = np.clip(np.minimum(c0 + CMP_LEN, s0 + SLC_LEN) - np.maximum(c0, s0), 0, None) / CMP_LEN
    full = np.zeros((LANE, LANE), np.float32)
    full[:n_slc, :n_cmp] = ov.T
    t["overlap"] = jnp.asarray(full, BF16)
    return t


def _pad_cols(w, width):
    return jnp.pad(w, ((0, 0), (0, width - w.shape[1])))


def _dup_groups(w, permuted):
    parts = []
    for g in range(NSA_GROUPS):
        blk = w[:, g * NSA_HD:(g + 1) * NSA_HD]
        parts.append(_perm_pair(blk, blk) if permuted else jnp.concatenate([blk, blk], axis=1))
    return jnp.concatenate(parts, axis=1)


def _blockdiag2(a):
    z = jnp.zeros_like(a)
    return jnp.concatenate([jnp.concatenate([a, z], axis=1), jnp.concatenate([z, a], axis=1)], axis=0)


def _layer_weights(l, p):
    w = {}
    row = lambda v: v[l][None, :].astype(F32)
    offs = np.concatenate([[0], np.cumsum(IN_SIZES)])
    cols = [p["w_in"][l][:, offs[k]:offs[k + 1]] for k in range(len(IN_SIZES))]
    (sb_q, sb_k, sb_v, cq, ckv, kr, nq, ck, cv, sk, sv, wk, wv, ngate, mgate) = cols
    sb_scale = 1.0 / math.sqrt(SB_HD)
    w["w_sb"] = jnp.concatenate([sb_q * sb_scale, sb_k, sb_v], axis=1).astype(BF16)
    kr_blk = _perm_mla(jnp.zeros((D_MODEL, MLA_NOPE), kr.dtype), kr)
    nq = jnp.concatenate([_perm_pair(nq[:, b * LANE:b * LANE + NSA_HD], nq[:, b * LANE + NSA_HD:(b + 1) * LANE])
                          for b in range(NSA_HEADS * NSA_HD // LANE)], axis=1)
    ng = ngate.reshape(D_MODEL, N_BRANCH, NSA_GROUPS, NSA_REP)
    ng = jnp.concatenate([_pad_cols(ng[:, :, g, :].reshape(D_MODEL, N_BRANCH * NSA_REP), LANE)
                          for g in range(NSA_GROUPS)], axis=1)
    w["w_z2"] = jnp.concatenate([cq, ckv, kr_blk, nq, ck, cv, _dup_groups(sk, True), _dup_groups(sv, False),
                                 _dup_groups(wk, True), _dup_groups(wv, False), ng], axis=1).astype(BF16)
    w["w_gate"] = mgate.astype(BF16)
    w["mix_norm"] = row(p["mix_norm"])
    w["mla_q_norm"] = row(p["mla_q_norm"])
    w["mla_kv_norm"] = row(p["mla_kv_norm"])
    wuq = p["mla_w_uq"][l].reshape(MLA_Q_LORA, MLA_HEADS, MLA_QK)
    w["wuq"] = _perm_mla(wuq[:, :, :MLA_NOPE], wuq[:, :, MLA_NOPE:]).reshape(MLA_Q_LORA, -1).astype(BF16)
    wukv = p["mla_w_ukv"][l].reshape(MLA_KV_LORA, MLA_HEADS, MLA_NOPE + MLA_V)
    no_rope = jnp.zeros((MLA_KV_LORA, MLA_HEADS, MLA_ROPE), wukv.dtype)
    w["wuk"] = _perm_mla(wukv[:, :, :MLA_NOPE], no_rope).reshape(MLA_KV_LORA, -1).astype(BF16)
    w["wuv"] = jnp.pad(wukv[:, :, MLA_NOPE:], ((0, 0), (0, 0), (0, LANE - MLA_V))).reshape(MLA_KV_LORA, -1).astype(BF16)
    gq, gk = row(p["mla_qk_gain_q"]), row(p["mla_qk_gain_k"])
    w["mla_gq"] = _perm_mla(gq[:, :MLA_NOPE], gq[:, MLA_NOPE:])
    w["mla_gk"] = _perm_mla(gk[:, :MLA_NOPE], gk[:, MLA_NOPE:])
    w["nsa_qg"] = _perm_pair(row(p["nsa_q_gain"]), row(p["nsa_q_gain"]))
    w["nsa_kg"] = _perm_pair(row(p["nsa_k_gain"]), row(p["nsa_k_gain"]))
    w["pe_k"] = jnp.tile(p["cmp_pos_k"][l], (1, 2)).astype(F32)
    w["pe_v"] = jnp.tile(p["cmp_pos_v"][l], (1, 2)).astype(F32)
    for nm, w1, w2 in (("k", p["cmp_wk1"][l], p["cmp_wk2"][l]), ("v", p["cmp_wv1"][l], p["cmp_wv2"][l])):
        w1 = w1.reshape(CMP_LEN, NSA_HD, CMP_HIDDEN)
        bd = jax.vmap(_blockdiag2)(w1).astype(BF16)
        w["w1%s_lo" % nm] = bd[:CMP_STRIDE]
        w["w1%s_hi" % nm] = bd[CMP_STRIDE:]
        w["w2" + nm] = _blockdiag2(_perm_pair(w2, w2) if nm == "k" else jnp.concatenate([w2, w2], axis=1)).astype(BF16)
    for nm in ("proj_sb", "proj_mla", "proj_nsa", "w_out"):
        w[nm] = p[nm][l].astype(BF16)
    for nm in ("ffn1", "ffn2"):
        w[nm + "_norm"] = row(p[nm + "_norm"])
    return w


def kernel(x, ffn1_norm, ffn1_wi, ffn1_wo, mix_norm, w_in, mla_q_norm, mla_w_uq, mla_kv_norm, mla_w_ukv, mla_qk_gain_q, mla_qk_gain_k, nsa_q_gain, nsa_k_gain, cmp_pos_k, cmp_pos_v, cmp_wk1, cmp_wk2, cmp_wv1, cmp_wv2, proj_sb, proj_mla, proj_nsa, w_out, ffn2_norm, ffn2_wi, ffn2_wo):
    p = dict(ffn1_norm=ffn1_norm, ffn1_wi=ffn1_wi, ffn1_wo=ffn1_wo, mix_norm=mix_norm, w_in=w_in,
             mla_q_norm=mla_q_norm, mla_w_uq=mla_w_uq, mla_kv_norm=mla_kv_norm, mla_w_ukv=mla_w_ukv,
             mla_qk_gain_q=mla_qk_gain_q, mla_qk_gain_k=mla_qk_gain_k, nsa_q_gain=nsa_q_gain,
             nsa_k_gain=nsa_k_gain, cmp_pos_k=cmp_pos_k, cmp_pos_v=cmp_pos_v, cmp_wk1=cmp_wk1,
             cmp_wk2=cmp_wk2, cmp_wv1=cmp_wv1, cmp_wv2=cmp_wv2, proj_sb=proj_sb, proj_mla=proj_mla,
             proj_nsa=proj_nsa, w_out=w_out, ffn2_norm=ffn2_norm, ffn2_wi=ffn2_wi, ffn2_wo=ffn2_wo)
    B, S, D = x.shape
    assert S // CMP_STRIDE == LANE and S // SLC_LEN <= LANE and D == D_MODEL
    tabs = _tables(S)
    xt = x.reshape(B * S, D)
    for l in range(ffn1_wi.shape[0]):
        w = _layer_weights(l, p)
        xt = _ffn(xt, w["ffn1_norm"], ffn1_wi, ffn1_wo, l)
        zsb, z2 = _inproj(xt, w["mix_norm"], w["w_sb"], w["w_z2"])
        zsb = zsb.reshape(B, S, -1)
        z2 = z2.reshape(B, S, -1)
        qm, km, vm, qn, sk, sv, wk, wv, gates = _prep(z2, w, tabs)
        kc, vc = _compress(z2, w, tabs)
        o_sb = _sb_attention(zsb)
        o_mla = _mla_attention(qm, km, vm)
        o_cmp, bias = _nsa_cmp(qn, kc, vc, tabs["overlap"])
        o_nsa = _nsa_attention(qn, sk, sv, wk, wv, bias, tabs["blk_onehot"], o_cmp, gates)
        xt = _merge(xt, w["mix_norm"], o_sb.reshape(B * S, -1), o_mla.reshape(B * S, -1),
                    o_nsa.reshape(B * S, -1), w)
        xt = _ffn(xt, w["ffn2_norm"], ffn2_wi, ffn2_wo, l)
    return xt.reshape(B, S, D)
```

```python
import functools
import math

import numpy as np
import jax
import jax.numpy as jnp
from jax import lax
from jax.experimental import pallas as pl
from jax.experimental.pallas import tpu as pltpu

D_MODEL = 1024
D_FF = 2816
EPS = 1e-6
ROPE_THETA = 10000.0
N_BRANCH = 3
SB_HEADS = 8
SB_HD = 64
MLA_HEADS = 8
MLA_NOPE = 64
MLA_ROPE = 32
MLA_V = 64
MLA_Q_LORA = 256
MLA_KV_LORA = 128
MLA_QK = MLA_NOPE + MLA_ROPE
NSA_HEADS = 8
NSA_GROUPS = 2
NSA_REP = NSA_HEADS // NSA_GROUPS
NSA_HD = 64
CMP_LEN = 32
CMP_STRIDE = 16
CMP_HIDDEN = 128
SLC_LEN = 64
SLC_TOPK = 16
WIN = 512
FORCE_SCORE = 1e3
NEG_INF = -1e30
SB_W = SB_HEADS * SB_HD
NSA_KV_W = NSA_GROUPS * NSA_HD
IN_SIZES = (SB_W, SB_W, SB_W, MLA_Q_LORA, MLA_KV_LORA, MLA_ROPE, NSA_HEADS * NSA_HD,
            NSA_KV_W, NSA_KV_W, NSA_KV_W, NSA_KV_W, NSA_KV_W, NSA_KV_W,
            N_BRANCH * NSA_HEADS, N_BRANCH * D_MODEL)

LANE = 128
V7X_VMEM_LIMIT = 56 * 1024 * 1024

FFN_TM = 1024
FFN_TF = 256
PROJ_TM = 512
PREP_TS = 512
ATT_T = 128
NSA_TK = 512
CMP_T = 512
SB_T = 256
SB_HEADS_PER_STEP = 4
MLA_TQ = 512
MLA_HEADS_PER_STEP = 4
MLA_TK = 512
MERGE_TM = 512

Z2_CQ = 0
Z2_CKV = 256
Z2_KR = 384
Z2_NQ = 512
Z2_CK = 1024
Z2_CV = 1152
Z2_SK = 1280
Z2_SV = 1536
Z2_WK = 1792
Z2_WV = 2048
Z2_GATE = 2304
Z2_W = 2560

BF16 = jnp.bfloat16
F32 = jnp.float32


def _cparams(sem, vmem=V7X_VMEM_LIMIT):
    return pltpu.CompilerParams(dimension_semantics=sem, vmem_limit_bytes=vmem)


def _dot(a, b):
    return jnp.dot(a, b, preferred_element_type=F32)


def _dot_nt(a, b):
    return lax.dot_general(a, b, (((1,), (1,)), ((), ())), preferred_element_type=F32)


def _split_rows(dot, a, b):
    h = a.shape[0] // 2
    return jnp.concatenate([dot(a[:h], b), dot(a[h:], b)], axis=0)


def _rms(x, g):
    y = x * lax.rsqrt(jnp.mean(x * x, axis=-1, keepdims=True) + EPS)
    return y * g


def _ffn_kernel(x_ref, g_ref, wa_ref, wb_ref, wo_ref, o_ref, h_ref, acc_ref):
    j = pl.program_id(1)

    @pl.when(j == 0)
    def _():
        h_ref[...] = _rms(x_ref[...], g_ref[...]).astype(BF16)
        acc_ref[...] = jnp.zeros_like(acc_ref)

    h = h_ref[...]
    a = _dot(h, wa_ref[...].astype(BF16))
    b = _dot(h, wb_ref[...].astype(BF16))
    u = (jax.nn.silu(a) * b).astype(BF16)
    acc_ref[...] += _dot(u, wo_ref[...].astype(BF16))

    @pl.when(j == pl.num_programs(1) - 1)
    def _():
        o_ref[...] = x_ref[...] + 0.5 * acc_ref[...]


def _ffn(x, g, wi, wo, l):
    T, D = x.shape
    F = wo.shape[1]
    tm, tf = FFN_TM, FFN_TF
    nf = F // tf
    return pl.pallas_call(
        _ffn_kernel,
        grid=(T // tm, nf),
        in_specs=[
            pl.BlockSpec((tm, D), lambda i, j: (i, 0)),
            pl.BlockSpec((1, D), lambda i, j: (0, 0)),
            pl.BlockSpec((None, D, tf), lambda i, j: (l, 0, j)),
            pl.BlockSpec((None, D, tf), lambda i, j: (l, 0, j + nf)),
            pl.BlockSpec((None, tf, D), lambda i, j: (l, j, 0)),
        ],
        out_specs=pl.BlockSpec((tm, D), lambda i, j: (i, 0)),
        out_shape=jax.ShapeDtypeStruct((T, D), F32),
        scratch_shapes=[pltpu.VMEM((tm, D), BF16), pltpu.VMEM((tm, D), F32)],
        compiler_params=_cparams(("parallel", "arbitrary")),
        name="ffn",
    )(x, g, wi, wi, wo)


def _rope_mix(y, c, s):
    return y * c + pltpu.roll(y, LANE // 2, 1) * s


def _first_head(lane):
    return (lane & (NSA_HD // 2)) == 0


def _norm_rope_pair(x, gain, c, s, scale):
    first = _first_head(lax.broadcasted_iota(jnp.int32, x.shape, 1))
    sq = x * x
    ss_a = jnp.sum(jnp.where(first, sq, 0.0), axis=-1, keepdims=True)
    ss_b = jnp.sum(jnp.where(first, 0.0, sq), axis=-1, keepdims=True)
    r = jnp.where(first, lax.rsqrt(ss_a * (1.0 / NSA_HD) + EPS), lax.rsqrt(ss_b * (1.0 / NSA_HD) + EPS))
    y = _rope_mix(x * r * gain, c, s)
    return y * scale if scale != 1.0 else y


def _norm_rope_mla(x, gain, c, s, scale):
    ss = jnp.sum(x * x, axis=-1, keepdims=True)
    y = _rope_mix(x * lax.rsqrt(ss * (1.0 / MLA_QK) + EPS) * gain, c, s)
    return y * scale if scale != 1.0 else y


def _inproj_kernel(x_ref, g_ref, wa_ref, wb_ref, oa_ref, ob_ref):
    h = _rms(x_ref[...], g_ref[...]).astype(BF16)
    oa_ref[...] = _dot(h, wa_ref[...]).astype(oa_ref.dtype)
    ob_ref[...] = _dot(h, wb_ref[...]).astype(ob_ref.dtype)


def _inproj(x, g, wa, wb):
    T, D = x.shape
    tm = PROJ_TM
    na, nb = wa.shape[1], wb.shape[1]
    return pl.pallas_call(
        _inproj_kernel,
        grid=(T // tm,),
        in_specs=[
            pl.BlockSpec((tm, D), lambda i: (i, 0)),
            pl.BlockSpec((1, D), lambda i: (0, 0)),
            pl.BlockSpec((D, na), lambda i: (0, 0)),
            pl.BlockSpec((D, nb), lambda i: (0, 0)),
        ],
        out_specs=[pl.BlockSpec((tm, na), lambda i: (i, 0)),
                   pl.BlockSpec((tm, nb), lambda i: (i, 0))],
        out_shape=[jax.ShapeDtypeStruct((T, na), BF16), jax.ShapeDtypeStruct((T, nb), F32)],
        compiler_params=_cparams(("parallel",)),
        name="inproj",
    )(x, g, wa, wb)


def _prep_kernel(z_ref, qn_ref, kvn_ref, wuq_ref, wuk_ref, wuv_ref, gq_ref, gk_ref,
                 mc_ref, ms_ref, nqg_ref, nkg_ref, nc_ref, ns_ref,
                 qm_ref, km_ref, vm_ref, qn_out, sk_out, sv_out, wk_out, wv_out, gate_out):
    zs = lambda start, width: z_ref[0, :, start:start + width]
    mc, ms = mc_ref[...], ms_ref[...]
    nc, ns = nc_ref[...], ns_ref[...]
    cq = _rms(zs(Z2_CQ, MLA_Q_LORA), qn_ref[...]).astype(BF16)
    ckv = _rms(zs(Z2_CKV, MLA_KV_LORA), kvn_ref[...]).astype(BF16)
    kr = zs(Z2_KR, LANE)
    q = _dot(cq, wuq_ref[...])
    kn = _dot(ckv, wuk_ref[...])
    v = _dot(ckv, wuv_ref[...])
    vlane = lax.broadcasted_iota(jnp.int32, (v.shape[0], LANE), 1)
    q_scale = 1.0 / math.sqrt(MLA_QK)
    for h in range(MLA_HEADS):
        sl = slice(h * LANE, (h + 1) * LANE)
        vm_ref[0, :, sl] = jnp.where((vlane < MLA_V) == (h % 2 == 0), v[:, sl], 1.0).astype(BF16)
        qm_ref[0, :, sl] = _norm_rope_mla(q[:, sl], gq_ref[...], mc, ms, q_scale).astype(BF16)
        km_ref[0, :, sl] = _norm_rope_mla(kn[:, sl] + kr, gk_ref[...], mc, ms, 1.0).astype(BF16)
    nq_scale = 1.0 / math.sqrt(NSA_HD)
    for b in range(NSA_HEADS * NSA_HD // LANE):
        sl = slice(b * LANE, (b + 1) * LANE)
        qn_out[0, :, sl] = _norm_rope_pair(zs(Z2_NQ + b * LANE, LANE), nqg_ref[...], nc, ns,
                                           nq_scale).astype(BF16)
    for b in range(2):
        sl = slice(b * LANE, (b + 1) * LANE)
        sk_out[0, :, sl] = _norm_rope_pair(zs(Z2_SK + b * LANE, LANE), nkg_ref[...], nc, ns, 1.0).astype(BF16)
        wk_out[0, :, sl] = _norm_rope_pair(zs(Z2_WK + b * LANE, LANE), nkg_ref[...], nc, ns, 1.0).astype(BF16)
    glane = lax.broadcasted_iota(jnp.int32, (z_ref.shape[1], 2 * LANE), 1) & (LANE - 1)
    sv_out[0] = jnp.where(glane < NSA_HD, zs(Z2_SV, 2 * LANE), 1.0).astype(BF16)
    wv_out[0] = jnp.where(glane < NSA_HD, zs(Z2_WV, 2 * LANE), 1.0).astype(BF16)
    gate_out[0] = jax.nn.sigmoid(zs(Z2_GATE, 2 * LANE))


def _prep(z2, w, tabs):
    B, S, _ = z2.shape
    ts = PREP_TS
    full = lambda shape: pl.BlockSpec(shape, lambda b, i: (0,) * len(shape))
    tab = pl.BlockSpec((ts, LANE), lambda b, i: (i, 0))
    tok = lambda width: pl.BlockSpec((1, ts, width), lambda b, i: (b, i, 0))
    out_w = (MLA_HEADS * LANE, MLA_HEADS * LANE, MLA_HEADS * LANE, NSA_HEADS * NSA_HD,
             2 * LANE, 2 * LANE, 2 * LANE, 2 * LANE, 2 * LANE)
    out_dt = (BF16,) * 8 + (F32,)
    return pl.pallas_call(
        _prep_kernel,
        grid=(B, S // ts),
        in_specs=[
            tok(Z2_W),
            full((1, MLA_Q_LORA)), full((1, MLA_KV_LORA)),
            full((MLA_Q_LORA, MLA_HEADS * LANE)), full((MLA_KV_LORA, MLA_HEADS * LANE)),
            full((MLA_KV_LORA, MLA_HEADS * LANE)),
            full((1, LANE)), full((1, LANE)),
            tab, tab,
            full((1, LANE)), full((1, LANE)),
            tab, tab,
        ],
        out_specs=[tok(wd) for wd in out_w],
        out_shape=[jax.ShapeDtypeStruct((B, S, wd), dt) for wd, dt in zip(out_w, out_dt)],
        compiler_params=_cparams(("parallel", "parallel")),
        name="prep",
    )(z2, w["mla_q_norm"], w["mla_kv_norm"], w["wuq"], w["wuk"], w["wuv"], w["mla_gq"], w["mla_gk"],
      tabs["mla_c"], tabs["mla_s"], w["nsa_qg"], w["nsa_kg"], tabs["nsa_c"], tabs["nsa_s"])


def _compress_kernel(ck_ref, cv_ref, pek_ref, pev_ref, w1kl_ref, w1kh_ref, w2k_ref,
                     w1vl_ref, w1vh_ref, w2v_ref, kg_ref, c_ref, s_ref,
                     kc_ref, vc_ref, *, nchunk):
    def branch(t_ref, pe_ref, w1l_ref, w1h_ref, w2_ref):
        lo = jnp.zeros((nchunk, 2 * CMP_HIDDEN), F32)
        hi = jnp.zeros((nchunk, 2 * CMP_HIDDEN), F32)
        for l in range(CMP_STRIDE):
            x = t_ref[0, pl.ds(l, nchunk, stride=CMP_STRIDE), :]
            lo = lo + _dot((x + pe_ref[l:l + 1, :]).astype(BF16), w1l_ref[l])
            hi = hi + _dot((x + pe_ref[CMP_STRIDE + l:CMP_STRIDE + l + 1, :]).astype(BF16), w1h_ref[l])
        pre = lo + pltpu.roll(hi, nchunk - 1, 0)
        return _dot(jax.nn.gelu(pre).astype(BF16), w2_ref[...])

    kc = branch(ck_ref, pek_ref, w1kl_ref, w1kh_ref, w2k_ref)
    vc = branch(cv_ref, pev_ref, w1vl_ref, w1vh_ref, w2v_ref)
    for b in range(2):
        sl = slice(b * LANE, (b + 1) * LANE)
        kc_ref[0, :, sl] = _norm_rope_pair(kc[:, sl], kg_ref[...], c_ref[...], s_ref[...], 1.0).astype(BF16)
    vc_ref[0] = vc.astype(BF16)


def _compress(z2, w, tabs):
    B, S, _ = z2.shape
    nchunk = S // CMP_STRIDE
    full = lambda shape: pl.BlockSpec(shape, lambda b: (0,) * len(shape))
    tab = full((nchunk, LANE))
    return pl.pallas_call(
        functools.partial(_compress_kernel, nchunk=nchunk),
        grid=(B,),
        in_specs=[
            pl.BlockSpec((1, S, LANE), lambda b: (b, 0, Z2_CK // LANE)),
            pl.BlockSpec((1, S, LANE), lambda b: (b, 0, Z2_CV // LANE)),
            full((CMP_LEN, LANE)), full((CMP_LEN, LANE)),
            full((CMP_STRIDE, LANE, 2 * CMP_HIDDEN)), full((CMP_STRIDE, LANE, 2 * CMP_HIDDEN)),
            full((2 * CMP_HIDDEN, 2 * LANE)),
            full((CMP_STRIDE, LANE, 2 * CMP_HIDDEN)), full((CMP_STRIDE, LANE, 2 * CMP_HIDDEN)),
            full((2 * CMP_HIDDEN, 2 * LANE)),
            full((1, LANE)), tab, tab,
        ],
        out_specs=[pl.BlockSpec((1, nchunk, 2 * LANE), lambda b: (b, 0, 0))] * 2,
        out_shape=[jax.ShapeDtypeStruct((B, nchunk, 2 * LANE), BF16)] * 2,
        compiler_params=_cparams(("parallel",)),
        name="compress",
    )(z2, z2, w["pe_k"], w["pe_v"], w["w1k_lo"], w["w1k_hi"], w["w2k"],
      w["w1v_lo"], w["w1v_hi"], w["w2v"], w["nsa_kg"], tabs["cmp_c"], tabs["cmp_s"])


def _sb_kernel(q_ref, k_ref, v_ref, o_ref, acc_ref, carry_ref, *, t, nh):
    i = pl.program_id(2)
    w = nh * SB_HD
    lane = lax.broadcasted_iota(jnp.int32, (t, w), 1)
    row = lax.broadcasted_iota(jnp.int32, (t, t), 0)
    col = lax.broadcasted_iota(jnp.int32, (t, t), 1)
    upper = jnp.where(row > col, 1.0, 0.0).astype(BF16)
    strict = (lax.broadcasted_iota(jnp.int32, (nh * t, t), 1)
              < (lax.broadcasted_iota(jnp.int32, (nh * t, t), 0) & (t - 1)))
    q = q_ref[0]
    own = [(lane >= SB_HD * h) & (lane < SB_HD * (h + 1)) for h in range(nh)]
    qs = jnp.concatenate([jnp.where(own[h], q, jnp.zeros_like(q)) for h in range(nh)], axis=0)

    def tile(j, carry, masked):
        off = pl.multiple_of(j * t, t)
        z = _split_rows(_dot_nt, qs, k_ref[0, pl.ds(off, t), :])
        lg = jnp.log(1.0 + jnp.exp(-jnp.abs(z)))
        logsig = jnp.minimum(z, 0.0) - lg
        lom = logsig - z
        if masked:
            lom = jnp.where(strict, lom, 0.0)
        suffix = _split_rows(_dot, lom.astype(BF16), upper)
        wgt = jnp.exp(logsig + suffix + carry)
        if masked:
            wgt = jnp.where(strict, wgt, 0.0)
        pv = _split_rows(_dot, wgt.astype(BF16), v_ref[0, pl.ds(off, t), :])
        return carry + jnp.sum(lom, axis=-1, keepdims=True), pv

    carry, pv = tile(i, jnp.zeros((nh * t, 1), F32), True)
    acc_ref[...] = pv
    odd = i % 2

    @pl.when(odd == 1)
    def _():
        c1, pv1 = tile(i - 1, carry, False)
        acc_ref[...] += pv1
        carry_ref[...] = c1

    @pl.when(odd == 0)
    def _():
        carry_ref[...] = carry

    def pair(s, c):
        j = i - 1 - odd - 2 * s
        c, pv_a = tile(j, c, False)
        c, pv_b = tile(j - 1, c, False)
        acc_ref[...] += pv_a + pv_b
        return c

    lax.fori_loop(0, i // 2, pair, carry_ref[...])
    out = acc_ref[0:t, :]
    for h in range(1, nh):
        out = jnp.where(own[h], acc_ref[h * t:(h + 1) * t, :], out)
    o_ref[0] = out.astype(o_ref.dtype)


def _sb_attention(zsb):
    B, S, _ = zsb.shape
    t = SB_T
    nh = SB_HEADS_PER_STEP
    w = nh * SB_HD
    npair = SB_W // w
    return pl.pallas_call(
        functools.partial(_sb_kernel, t=t, nh=nh),
        grid=(B, npair, S // t),
        in_specs=[
            pl.BlockSpec((1, t, w), lambda b, p, i: (b, i, p)),
            pl.BlockSpec((1, S, w), lambda b, p, i: (b, 0, npair + p)),
            pl.BlockSpec((1, S, w), lambda b, p, i: (b, 0, 2 * npair + p)),
        ],
        out_specs=pl.BlockSpec((1, t, w), lambda b, p, i: (b, i, p)),
        out_shape=jax.ShapeDtypeStruct((B, S, SB_W), BF16),
        scratch_shapes=[pltpu.VMEM((nh * t, w), F32), pltpu.VMEM((nh * t, 1), F32)],
        compiler_params=_cparams(("parallel", "parallel", "arbitrary")),
        name="sb_attn",
    )(zsb, zsb, zsb)


def _finish_pair(lane, acc_a, acc_b):
    num = jnp.where(lane < NSA_HD, acc_a, pltpu.roll(acc_b, NSA_HD, 1))
    den = jnp.where(lane < NSA_HD, pltpu.roll(acc_a, NSA_HD, 1), acc_b)
    return num / jnp.maximum(den, 1e-30)


def _finish_pair_mirrored(lane, acc_a, acc_b):
    num = jnp.where(lane < NSA_HD, acc_a, acc_b)
    den = pltpu.roll(jnp.where(lane < NSA_HD, acc_b, acc_a), NSA_HD, 1)
    return num / jnp.maximum(den, 1e-30)


def _run_by_pairs(n, body):
    odd = n % 2

    @pl.when(odd == 1)
    def _():
        body((0,))

    def pair(s, c):
        body((odd + 2 * s, odd + 2 * s + 1))
        return c

    lax.fori_loop(0, n // 2, pair, 0)


def _lane_block_max(m, s):
    for c in range(s.shape[1] // LANE):
        m = jnp.maximum(m, s[:, c * LANE:(c + 1) * LANE])
    return m


def _mla_kernel(q_ref, k_ref, v_ref, o_ref, s_ref, mx_ref, acc_ref, *, tq, tk, nh):
    i = pl.program_id(2)
    lane = lax.broadcasted_iota(jnp.int32, (tq, LANE), 1)
    col_minus_row = (lax.broadcasted_iota(jnp.int32, (tq, tk), 1)
                     - lax.broadcasted_iota(jnp.int32, (tq, tk), 0))
    tail = (i * tq) // tk
    thr = i * tq - tail * tk
    heads = [slice(h * LANE, (h + 1) * LANE) for h in range(nh)]

    def score_tiles(js, masked):
        for h, sl in enumerate(heads):
            mx = None if masked else mx_ref[h]
            for j in js:
                s = _dot_nt(q_ref[0, :, sl], k_ref[0, pl.ds(pl.multiple_of(j * tk, tk), tk), sl])
                if masked:
                    s = jnp.where(col_minus_row <= thr, s, NEG_INF)
                s_ref[h, j] = s
                mx = _lane_block_max(s[:, :LANE] if mx is None else mx, s)
            mx_ref[h] = mx

    score_tiles((tail,), True)
    _run_by_pairs(tail, lambda js: score_tiles(js, False))
    m = [jnp.max(mx_ref[h], axis=-1, keepdims=True) for h in range(nh)]

    def absorb(js, first):
        for h, sl in enumerate(heads):
            pv = _tree_sum([_dot(jnp.exp(s_ref[h, j] - m[h]).astype(BF16),
                                 v_ref[0, pl.ds(pl.multiple_of(j * tk, tk), tk), sl]) for j in js])
            acc_ref[h] = pv if first else acc_ref[h] + pv

    absorb((tail,), True)
    _run_by_pairs(tail, lambda js: absorb(js, False))
    for p in range(nh // 2):
        o = _finish_pair_mirrored(lane, acc_ref[2 * p], acc_ref[2 * p + 1])
        o_ref[0, :, p * LANE:(p + 1) * LANE] = o.astype(o_ref.dtype)


def _mla_attention(qm, km, vm):
    B, S, _ = qm.shape
    t = MLA_TQ
    nh = MLA_HEADS_PER_STEP
    ngrp = MLA_HEADS // nh
    return pl.pallas_call(
        functools.partial(_mla_kernel, tq=t, tk=MLA_TK, nh=nh),
        grid=(B, ngrp, S // t),
        in_specs=[
            pl.BlockSpec((1, t, nh * LANE), lambda b, p, i: (b, i, p)),
            pl.BlockSpec((1, S, nh * LANE), lambda b, p, i: (b, 0, p)),
            pl.BlockSpec((1, S, nh * LANE), lambda b, p, i: (b, 0, p)),
        ],
        out_specs=pl.BlockSpec((1, t, nh * MLA_V), lambda b, p, i: (b, i, p)),
        out_shape=jax.ShapeDtypeStruct((B, S, MLA_HEADS * MLA_V), BF16),
        scratch_shapes=[pltpu.VMEM((nh, S // MLA_TK, t, MLA_TK), F32),
                        pltpu.VMEM((nh, t, LANE), F32), pltpu.VMEM((nh, t, LANE), F32)],
        compiler_params=_cparams(("parallel", "parallel", "arbitrary")),
        name="mla_attn",
    )(qm, km, vm)


def _stack_heads(q, lane):
    first = _first_head(lane)
    parts = []
    for h in range(NSA_REP):
        blk = q[:, (h // 2) * LANE:(h // 2 + 1) * LANE]
        parts.append(jnp.where(first if h % 2 == 0 else ~first, blk, jnp.zeros_like(blk)))
    return parts


def _pair_lanes(lane, a, b):
    return jnp.where(lane < NSA_HD, a, b)


def _tree_sum(terms):
    while len(terms) > 1:
        terms = [a + b for a, b in zip(terms[0::2], terms[1::2])] + (terms[-1:] if len(terms) % 2 else [])
    return terms[0]


def _nsa_cmp_kernel(q_ref, kc_ref, vc_ref, ovt_ref, ocmp_ref, bias_ref, *, t, n_slc):
    i = pl.program_id(1)
    gw = NSA_REP * NSA_HD
    lane = lax.broadcasted_iota(jnp.int32, (t, LANE), 1)
    row = lax.broadcasted_iota(jnp.int32, (t, LANE), 0)
    visible = lane * CMP_STRIDE + (CMP_LEN - 1) <= i * t + row
    blk = lax.broadcasted_iota(jnp.int32, (n_slc, t), 0)
    cur = (i * t + lax.broadcasted_iota(jnp.int32, (n_slc, t), 1)) // SLC_LEN
    forced = (blk == 0) | (blk == cur) | (blk == cur - 1)
    for g in range(NSA_GROUPS):
        kc = kc_ref[0, :, g * LANE:(g + 1) * LANE]
        vc = vc_ref[0, :, g * LANE:(g + 1) * LANE]
        psum = jnp.zeros((t, LANE), F32)
        outs = []
        for qh in _stack_heads(q_ref[0, :, g * gw:(g + 1) * gw], lane):
            s = jnp.where(visible, _dot_nt(qh, kc), NEG_INF)
            m = jnp.max(s, axis=-1, keepdims=True)
            e = jnp.where(visible, jnp.exp(s - m), 0.0)
            p = e / jnp.maximum(jnp.sum(e, axis=-1, keepdims=True), 1e-30)
            psum = psum + p
            outs.append(_dot(p.astype(BF16), vc))
        ocmp_ref[0, :, g * gw:g * gw + LANE] = _pair_lanes(lane, outs[0], outs[1])
        ocmp_ref[0, :, g * gw + LANE:(g + 1) * gw] = _pair_lanes(lane, outs[2], outs[3])
        hi = psum.astype(BF16)
        lo = (psum - hi.astype(F32)).astype(BF16)
        imp = (_dot_nt(ovt_ref[...], hi) + _dot_nt(ovt_ref[...], lo))[:n_slc]
        score = jnp.where(blk > cur, -1.0, jnp.where(forced, FORCE_SCORE, imp))
        terms = []
        for c in range(n_slc):
            sc = score[c:c + 1, :]
            before = (sc > score) | ((sc == score) & (blk > c))
            terms.append(jnp.where(before, 1.0, 0.0))
        bias_t = jnp.where(_tree_sum(terms) < float(min(SLC_TOPK, n_slc)), 0.0, NEG_INF)
        bias_t = jnp.concatenate([bias_t, jnp.full((LANE - n_slc, t), NEG_INF, F32)], axis=0)
        bias_ref[0, g] = bias_t.T.astype(BF16)


def _nsa_cmp(qn, kc, vc, overlap):
    B, S, _ = qn.shape
    t = CMP_T
    nkv = NSA_GROUPS * LANE
    return pl.pallas_call(
        functools.partial(_nsa_cmp_kernel, t=t, n_slc=S // SLC_LEN),
        grid=(B, S // t),
        in_specs=[
            pl.BlockSpec((1, t, NSA_HEADS * NSA_HD), lambda b, i: (b, i, 0)),
            pl.BlockSpec((1, kc.shape[1], nkv), lambda b, i: (b, 0, 0)),
            pl.BlockSpec((1, vc.shape[1], nkv), lambda b, i: (b, 0, 0)),
            pl.BlockSpec((LANE, LANE), lambda b, i: (0, 0)),
        ],
        out_specs=[pl.BlockSpec((1, t, NSA_HEADS * NSA_HD), lambda b, i: (b, i, 0)),
                   pl.BlockSpec((1, NSA_GROUPS, t, LANE), lambda b, i: (b, 0, i, 0))],
        out_shape=[jax.ShapeDtypeStruct((B, S, NSA_HEADS * NSA_HD), F32),
                   jax.ShapeDtypeStruct((B, NSA_GROUPS, S, LANE), BF16)],
        compiler_params=_cparams(("parallel", "parallel")),
        name="nsa_cmp",
    )(qn, kc, vc, overlap)


def _nsa_attn_kernel(q_ref, sk_ref, sv_ref, wk_ref, wv_ref, bias_ref, blk_ref, ocmp_ref, gate_ref,
                     o_ref, s_ref, mix_ref, mx_ref, acc_ref, *, t, tk):
    i = pl.program_id(1)
    r4 = NSA_REP * t
    gw = NSA_REP * NSA_HD
    groups = range(NSA_GROUPS)
    lane = lax.broadcasted_iota(jnp.int32, (t, LANE), 1)
    gl = [slice(g * LANE, (g + 1) * LANE) for g in groups]
    q4 = [jnp.concatenate(_stack_heads(q_ref[0, :, g * gw:(g + 1) * gw], lane), axis=0) for g in groups]
    q4b = [jnp.concatenate([q4[g], jnp.concatenate([bias_ref[0, g]] * NSA_REP, axis=0)], axis=1)
           for g in groups]

    rmc = ((lax.broadcasted_iota(jnp.int32, (r4, t), 0) & (t - 1))
           - lax.broadcasted_iota(jnp.int32, (r4, t), 1))

    def mask_blocks(s, visible):
        return jnp.concatenate([jnp.where(visible(c), s[:, c * t:(c + 1) * t], NEG_INF)
                                for c in range(s.shape[1] // t)], axis=1)

    tail = (i * t) // tk
    thr = i * t - tail * tk

    def score_tiles(js, masked):
        s = {}
        for n, j in enumerate(js):
            off = pl.multiple_of(j * tk, tk)
            onehot = blk_ref[pl.ds(off, tk), :]
            for g in groups:
                s[g, n] = _dot_nt(q4b[g], jnp.concatenate([sk_ref[0, pl.ds(off, tk), gl[g]], onehot], axis=1))
        for g in groups:
            mx = None if masked else mx_ref[g]
            for n, j in enumerate(js):
                sj = s[g, n]
                if masked:
                    sj = mask_blocks(sj, lambda c: rmc >= c * t - thr)
                s_ref[g, j] = sj
                mx = _lane_block_max(sj[:, :LANE] if mx is None else mx, sj)
            mx_ref[g] = mx

    score_tiles((tail,), True)
    _run_by_pairs(tail, lambda js: score_tiles(js, False))
    m_s = [jnp.max(mx_ref[g], axis=-1, keepdims=True) for g in groups]

    def gate_pair(g, br, p2):
        c0 = g * LANE + br * NSA_REP + 2 * p2
        return _pair_lanes(lane, gate_ref[0, :, c0:c0 + 1], gate_ref[0, :, c0 + 1:c0 + 2])

    def head_pair(acc, p2):
        a, b = 2 * p2, 2 * p2 + 1
        return _finish_pair(lane, acc[a * t:(a + 1) * t], acc[b * t:(b + 1) * t])

    wkeys = WIN + t
    start = pl.multiple_of(jnp.maximum(i * t - WIN, 0), t)
    lead = i * t - start

    def in_window(c):
        return pltpu.bitcast(rmc + (lead - c * t), jnp.uint32) < jnp.uint32(WIN)

    s_w = [mask_blocks(_dot_nt(q4[g], wk_ref[0, pl.ds(start, wkeys), gl[g]]), in_window) for g in groups]
    p_w = [jnp.exp(s_w[g] - jnp.max(s_w[g], axis=-1, keepdims=True)).astype(BF16) for g in groups]
    pv_w = [_dot(p_w[g], wv_ref[0, pl.ds(start, wkeys), gl[g]]) for g in groups]
    for g in groups:
        for p2 in range(NSA_REP // 2):
            cols = slice(g * gw + p2 * LANE, g * gw + (p2 + 1) * LANE)
            mix_ref[:, cols] = (gate_pair(g, 0, p2) * ocmp_ref[0, :, cols]
                                + gate_pair(g, 2, p2) * head_pair(pv_w[g], p2))

    def absorb(js, first):
        p = {(g, n): jnp.exp(s_ref[g, j] - m_s[g]).astype(BF16) for n, j in enumerate(js) for g in groups}
        for g in groups:
            pv = _tree_sum([_dot(p[g, n], sv_ref[0, pl.ds(pl.multiple_of(j * tk, tk), tk), gl[g]])
                            for n, j in enumerate(js)])
            acc_ref[g] = pv if first else acc_ref[g] + pv

    absorb((tail,), True)
    _run_by_pairs(tail, lambda js: absorb(js, False))
    for g in groups:
        acc_g = acc_ref[g]
        for p2 in range(NSA_REP // 2):
            cols = slice(g * gw + p2 * LANE, g * gw + (p2 + 1) * LANE)
            o_ref[0, :, cols] = (mix_ref[:, cols] + gate_pair(g, 1, p2) * head_pair(acc_g, p2)).astype(o_ref.dtype)


def _nsa_attention(qn, sk, sv, wk, wv, bias, blk_onehot, ocmp, gates):
    B, S, _ = qn.shape
    t = ATT_T
    hw = NSA_HEADS * NSA_HD
    nkv = NSA_GROUPS * LANE
    kv = pl.BlockSpec((1, S, nkv), lambda b, i: (b, 0, 0))
    return pl.pallas_call(
        functools.partial(_nsa_attn_kernel, t=t, tk=NSA_TK),
        grid=(B, S // t),
        in_specs=[
            pl.BlockSpec((1, t, hw), lambda b, i: (b, i, 0)),
            kv, kv, kv, kv,
            pl.BlockSpec((1, NSA_GROUPS, t, LANE), lambda b, i: (b, 0, i, 0)),
            pl.BlockSpec((S, LANE), lambda b, i: (0, 0)),
            pl.BlockSpec((1, t, hw), lambda b, i: (b, i, 0)),
            pl.BlockSpec((1, t, nkv), lambda b, i: (b, i, 0)),
        ],
        out_specs=pl.BlockSpec((1, t, hw), lambda b, i: (b, i, 0)),
        out_shape=jax.ShapeDtypeStruct((B, S, hw), BF16),
        scratch_shapes=[pltpu.VMEM((NSA_GROUPS, S // NSA_TK, NSA_REP * t, NSA_TK), F32),
                        pltpu.VMEM((t, hw), F32),
                        pltpu.VMEM((NSA_GROUPS, NSA_REP * t, LANE), F32),
                        pltpu.VMEM((NSA_GROUPS, NSA_REP * t, LANE), F32)],
        compiler_params=_cparams(("parallel", "arbitrary")),
        name="nsa_attn",
    )(qn, sk, sv, wk, wv, bias, blk_onehot, ocmp, gates)


def _merge_kernel(x_ref, g_ref, osb_ref, omla_ref, onsa_ref, wg_ref, psb_ref, pmla_ref, pnsa_ref,
                  wout_ref, o_ref):
    x = x_ref[...]
    h = _rms(x, g_ref[...]).astype(BF16)
    y = jnp.zeros(x.shape, F32)
    for br, (o_r, p_r) in enumerate(((osb_ref, psb_ref), (omla_ref, pmla_ref), (onsa_ref, pnsa_ref))):
        gate = jax.nn.sigmoid(_dot(h, wg_ref[:, br * D_MODEL:(br + 1) * D_MODEL]))
        y = y + gate * _dot(o_r[...], p_r[...])
    o_ref[...] = x + _dot(y.astype(BF16), wout_ref[...])


def _merge(x, g, osb, omla, onsa, w):
    T, D = x.shape
    tm = MERGE_TM
    tok = lambda width: pl.BlockSpec((tm, width), lambda i: (i, 0))
    full = lambda shape: pl.BlockSpec(shape, lambda i: (0,) * len(shape))
    return pl.pallas_call(
        _merge_kernel,
        grid=(T // tm,),
        in_specs=[tok(D), full((1, D)), tok(osb.shape[1]), tok(omla.shape[1]), tok(onsa.shape[1]),
                  full((D, N_BRANCH * D)), full(w["proj_sb"].shape), full(w["proj_mla"].shape),
                  full(w["proj_nsa"].shape), full((D, D))],
        out_specs=tok(D),
        out_shape=jax.ShapeDtypeStruct((T, D), F32),
        compiler_params=_cparams(("parallel",)),
        name="merge",
    )(x, g, osb, omla, onsa, w["w_gate"], w["proj_sb"], w["proj_mla"], w["proj_nsa"], w["w_out"])


def _cos_sin(pos, d):
    half = d // 2
    inv = jnp.exp(-math.log(ROPE_THETA) * jnp.arange(half, dtype=F32) * (2.0 / d))
    ang = pos.astype(F32)[:, None] * inv[None, :]
    return jnp.cos(ang), jnp.sin(ang)


def _pair_rope_tables(pos):
    cos, sin = _cos_sin(pos, NSA_HD)
    return jnp.concatenate([cos] * 4, axis=1), jnp.concatenate([-sin, -sin, sin, sin], axis=1)


def _mla_rope_tables(pos):
    cos, sin = _cos_sin(pos, MLA_ROPE)
    n, half = cos.shape
    one = lambda k: jnp.ones((n, k), F32)
    zero = lambda k: jnp.zeros((n, k), F32)
    rest = LANE // 2 - half
    c = jnp.concatenate([cos, one(rest), cos, one(MLA_NOPE - rest), zero(LANE - MLA_QK)], axis=1)
    s = jnp.concatenate([-sin, zero(rest), sin, zero(LANE // 2 - half)], axis=1)
    return c, s


def _perm_pair(a, b):
    h = NSA_HD // 2
    return jnp.concatenate([a[..., :h], b[..., :h], a[..., h:], b[..., h:]], axis=-1)


def _perm_mla(nope, rope):
    h = MLA_ROPE // 2
    rest = LANE // 2 - h
    pad = jnp.zeros(nope.shape[:-1] + (LANE - MLA_QK,), nope.dtype)
    return jnp.concatenate([rope[..., :h], nope[..., :rest], rope[..., h:], nope[..., rest:], pad], axis=-1)


def _tables(S):
    pos = jnp.arange(S, dtype=jnp.int32)
    nchunk = S // CMP_STRIDE
    ends = jnp.arange(nchunk, dtype=jnp.int32) * CMP_STRIDE + (CMP_LEN - 1)
    t = {}
    t["nsa_c"], t["nsa_s"] = _pair_rope_tables(pos)
    t["cmp_c"], t["cmp_s"] = _pair_rope_tables(ends)
    t["mla_c"], t["mla_s"] = _mla_rope_tables(pos)
    t["blk_onehot"] = (pos[:, None] // SLC_LEN == jnp.arange(LANE, dtype=jnp.int32)[None, :]).astype(BF16)
    n_cmp = (S - CMP_LEN) // CMP_STRIDE + 1
    n_slc = S // SLC_LEN
    c0 = np.arange(n_cmp)[:, None] * CMP_STRIDE
    s0 = np.arange(n_slc)[None, :] * SLC_LEN
    ov = np.clip(np.minimum(c0 + CMP_LEN, s0 + SLC_LEN) - np.maximum(c0, s0), 0, None) / CMP_LEN
    full = np.zeros((LANE, LANE), np.float32)
    full[:n_slc, :n_cmp] = ov.T
    t["overlap"] = jnp.asarray(full, BF16)
    return t


def _pad_cols(w, width):
    return jnp.pad(w, ((0, 0), (0, width - w.shape[1])))


def _dup_groups(w, permuted):
    parts = []
    for g in range(NSA_GROUPS):
        blk = w[:, g * NSA_HD:(g + 1) * NSA_HD]
        parts.append(_perm_pair(blk, blk) if permuted else jnp.concatenate([blk, blk], axis=1))
    return jnp.concatenate(parts, axis=1)


def _blockdiag2(a):
    z = jnp.zeros_like(a)
    return jnp.concatenate([jnp.concatenate([a, z], axis=1), jnp.concatenate([z, a], axis=1)], axis=0)


def _layer_weights(l, p):
    w = {}
    row = lambda v: v[l][None, :].astype(F32)
    offs = np.concatenate([[0], np.cumsum(IN_SIZES)])
    cols = [p["w_in"][l][:, offs[k]:offs[k + 1]] for k in range(len(IN_SIZES))]
    (sb_q, sb_k, sb_v, cq, ckv, kr, nq, ck, cv, sk, sv, wk, wv, ngate, mgate) = cols
    sb_scale = 1.0 / math.sqrt(SB_HD)
    w["w_sb"] = jnp.concatenate([sb_q * sb_scale, sb_k, sb_v], axis=1).astype(BF16)
    kr_blk = _perm_mla(jnp.zeros((D_MODEL, MLA_NOPE), kr.dtype), kr)
    nq = jnp.concatenate([_perm_pair(nq[:, b * LANE:b * LANE + NSA_HD], nq[:, b * LANE + NSA_HD:(b + 1) * LANE])
                          for b in range(NSA_HEADS * NSA_HD // LANE)], axis=1)
    ng = ngate.reshape(D_MODEL, N_BRANCH, NSA_GROUPS, NSA_REP)
    ng = jnp.concatenate([_pad_cols(ng[:, :, g, :].reshape(D_MODEL, N_BRANCH * NSA_REP), LANE)
                          for g in range(NSA_GROUPS)], axis=1)
    w["w_z2"] = jnp.concatenate([cq, ckv, kr_blk, nq, ck, cv, _dup_groups(sk, True), _dup_groups(sv, False),
                                 _dup_groups(wk, True), _dup_groups(wv, False), ng], axis=1).astype(BF16)
    w["w_gate"] = mgate.astype(BF16)
    w["mix_norm"] = row(p["mix_norm"])
    w["mla_q_norm"] = row(p["mla_q_norm"])
    w["mla_kv_norm"] = row(p["mla_kv_norm"])
    wuq = p["mla_w_uq"][l].reshape(MLA_Q_LORA, MLA_HEADS, MLA_QK)
    w["wuq"] = _perm_mla(wuq[:, :, :MLA_NOPE], wuq[:, :, MLA_NOPE:]).reshape(MLA_Q_LORA, -1).astype(BF16)
    wukv = p["mla_w_ukv"][l].reshape(MLA_KV_LORA, MLA_HEADS, MLA_NOPE + MLA_V)
    no_rope = jnp.zeros((MLA_KV_LORA, MLA_HEADS, MLA_ROPE), wukv.dtype)
    w["wuk"] = _perm_mla(wukv[:, :, :MLA_NOPE], no_rope).reshape(MLA_KV_LORA, -1).astype(BF16)
    w["wuv"] = jnp.concatenate(
        [jnp.pad(wukv[:, h, MLA_NOPE:], ((0, 0), (LANE - MLA_V, 0) if h % 2 else (0, LANE - MLA_V)))
         for h in range(MLA_HEADS)], axis=1).astype(BF16)
    gq, gk = row(p["mla_qk_gain_q"]), row(p["mla_qk_gain_k"])
    w["mla_gq"] = _perm_mla(gq[:, :MLA_NOPE], gq[:, MLA_NOPE:])
    w["mla_gk"] = _perm_mla(gk[:, :MLA_NOPE], gk[:, MLA_NOPE:])
    w["nsa_qg"] = _perm_pair(row(p["nsa_q_gain"]), row(p["nsa_q_gain"]))
    w["nsa_kg"] = _perm_pair(row(p["nsa_k_gain"]), row(p["nsa_k_gain"]))
    w["pe_k"] = jnp.tile(p["cmp_pos_k"][l], (1, 2)).astype(F32)
    w["pe_v"] = jnp.tile(p["cmp_pos_v"][l], (1, 2)).astype(F32)
    for nm, w1, w2 in (("k", p["cmp_wk1"][l], p["cmp_wk2"][l]), ("v", p["cmp_wv1"][l], p["cmp_wv2"][l])):
        w1 = w1.reshape(CMP_LEN, NSA_HD, CMP_HIDDEN)
        bd = jax.vmap(_blockdiag2)(w1).astype(BF16)
        w["w1%s_lo" % nm] = bd[:CMP_STRIDE]
        w["w1%s_hi" % nm] = bd[CMP_STRIDE:]
        w["w2" + nm] = _blockdiag2(_perm_pair(w2, w2) if nm == "k" else jnp.concatenate([w2, w2], axis=1)).astype(BF16)
    for nm in ("proj_sb", "proj_mla", "proj_nsa", "w_out"):
        w[nm] = p[nm][l].astype(BF16)
    for nm in ("ffn1", "ffn2"):
        w[nm + "_norm"] = row(p[nm + "_norm"])
    return w


def kernel(x, ffn1_norm, ffn1_wi, ffn1_wo, mix_norm, w_in, mla_q_norm, mla_w_uq, mla_kv_norm, mla_w_ukv, mla_qk_gain_q, mla_qk_gain_k, nsa_q_gain, nsa_k_gain, cmp_pos_k, cmp_pos_v, cmp_wk1, cmp_wk2, cmp_wv1, cmp_wv2, proj_sb, proj_mla, proj_nsa, w_out, ffn2_norm, ffn2_wi, ffn2_wo):
    p = dict(ffn1_norm=ffn1_norm, ffn1_wi=ffn1_wi, ffn1_wo=ffn1_wo, mix_norm=mix_norm, w_in=w_in,
             mla_q_norm=mla_q_norm, mla_w_uq=mla_w_uq, mla_kv_norm=mla_kv_norm, mla_w_ukv=mla_w_ukv,
             mla_qk_gain_q=mla_qk_gain_q, mla_qk_gain_k=mla_qk_gain_k, nsa_q_gain=nsa_q_gain,
             nsa_k_gain=nsa_k_gain, cmp_pos_k=cmp_pos_k, cmp_pos_v=cmp_pos_v, cmp_wk1=cmp_wk1,
             cmp_wk2=cmp_wk2, cmp_wv1=cmp_wv1, cmp_wv2=cmp_wv2, proj_sb=proj_sb, proj_mla=proj_mla,
             proj_nsa=proj_nsa, w_out=w_out, ffn2_norm=ffn2_norm, ffn2_wi=ffn2_wi, ffn2_wo=ffn2_wo)
    B, S, D = x.shape
    assert S // CMP_STRIDE == LANE and S // SLC_LEN <= LANE and D == D_MODEL
    tabs = _tables(S)
    xt = x.reshape(B * S, D)
    for l in range(ffn1_wi.shape[0]):
        w = _layer_weights(l, p)
        xt = _ffn(xt, w["ffn1_norm"], ffn1_wi, ffn1_wo, l)
        zsb, z2 = _inproj(xt, w["mix_norm"], w["w_sb"], w["w_z2"])
        zsb = zsb.reshape(B, S, -1)
        z2 = z2.reshape(B, S, -1)
        qm, km, vm, qn, sk, sv, wk, wv, gates = _prep(z2, w, tabs)
        kc, vc = _compress(z2, w, tabs)
        o_sb = _sb_attention(zsb)
        o_mla = _mla_attention(qm, km, vm)
        o_cmp, bias = _nsa_cmp(qn, kc, vc, tabs["overlap"])
        o_nsa = _nsa_attention(qn, sk, sv, wk, wv, bias, tabs["blk_onehot"], o_cmp, gates)
        xt = _merge(xt, w["mix_norm"], o_sb.reshape(B * S, -1), o_mla.reshape(B * S, -1),
                    o_nsa.reshape(B * S, -1), w)
        xt = _ffn(xt, w["ffn2_norm"], ffn2_wi, ffn2_wo, l)
    return xt.reshape(B, S, D)
```
